```python
import math
import jax
import jax.numpy as jnp
from jax import lax
import numpy as np

D_MODEL = 2048
BATCH = 16
SEQ = 256
DEPTH = 4
DEC_BATCH = 8
DEC_SEQ = 2048
PAST_LEN = 256

GRID_W = 64
N_MIXERS = 3
Q_BLOCK = 128
ROPE_THETA = 10000.0
EPS = 1e-6

ATTN_HEADS = 16
ATTN_KV_HEADS = 4
ATTN_HD = D_MODEL // ATTN_HEADS
ATTN_IN = (ATTN_HEADS + 2 * ATTN_KV_HEADS) * ATTN_HD
DIFF_HEADS = 8
DIFF_HD = D_MODEL // (2 * DIFF_HEADS)
DIFF_IN = 6 * DIFF_HEADS * DIFF_HD
SSD_INNER = 2 * D_MODEL
SSD_HEADDIM = 64
SSD_HEADS = SSD_INNER // SSD_HEADDIM
SSD_GROUPS = 8
SSD_STATE = 128
SSD_CONV = 5
SSD_CHUNK = 128
SSD_GN = SSD_GROUPS * SSD_STATE
SSD_CONV_DIM = SSD_INNER + 2 * SSD_GN
SSD_IN = SSD_INNER + SSD_CONV_DIM + 2 * SSD_HEADS
FFN_DIM = 5632
N_EXPERTS = 8
TOP_K = 2
EXPERT_DIM = 7168
EXPERT_BLOCK = 128

N_ATTN_LAYERS = (DEPTH + 2) // 3
N_DIFF_LAYERS = (DEPTH + 1) // 3
N_SSD_LAYERS = DEPTH // 3
N_DENSE_LAYERS = (DEPTH + 1) // 2
N_MOE_LAYERS = DEPTH // 2

kernel_name = 'hybrid_diffusion_prefix_trunk_step'

F32 = jnp.float32


def rms_norm(x, g):
    xf = x.astype(F32)
    y = xf * lax.rsqrt(jnp.mean(xf * xf, axis=-1, keepdims=True) + EPS)
    return (y * g.astype(F32)).astype(x.dtype)


def modulation(cond, w_ada, b_ada):
    m = jax.nn.silu(cond) @ w_ada + b_ada
    if cond.ndim == 2:
        m = m[:, None, :]
    return jnp.split(m, 6, axis=-1)


def modulate(x, shift, scale):
    return x * (1 + scale) + shift


def axial_rope_tables(rows, hd):
    nf = hd // 4
    inv = ROPE_THETA ** (-jnp.arange(nf, dtype=F32) / nf)
    row = jnp.repeat(jnp.arange(rows, dtype=F32), GRID_W)
    col = jnp.tile(jnp.arange(GRID_W, dtype=F32), rows)
    ar = row[:, None] * inv
    ac = col[:, None] * inv
    ang = jnp.concatenate([ar, ar, ac, ac], axis=-1)
    return jnp.cos(ang), jnp.sin(ang)


def apply_rope(x, rope):
    cos, sin = rope
    xf = x.astype(F32)
    x1, x2, x3, x4 = jnp.split(xf, 4, axis=-1)
    rot = jnp.concatenate([-x2, x1, -x4, x3], axis=-1)
    return (xf * cos[:, None, :] + rot * sin[:, None, :]).astype(x.dtype)


def sweep_query_blocks(fn, q):
    b, l = q.shape[:2]
    nb = l // Q_BLOCK
    qb = jnp.moveaxis(q.reshape((b, nb, Q_BLOCK) + q.shape[2:]), 1, 0)
    out = lax.map(fn, qb)
    return jnp.moveaxis(out, 0, 1).reshape((b, l) + out.shape[3:])


def gqa_attend(q, k, v):
    b, _, h, hd = q.shape
    kvh = k.shape[2]
    grp = h // kvh
    scale = 1.0 / math.sqrt(hd)

    def block(qb):
        qg = qb.reshape(b, Q_BLOCK, kvh, grp, hd)
        s = jnp.einsum('bqkgd,btkd->bkgqt', qg, k, preferred_element_type=F32) * scale
        p = jax.nn.softmax(s, axis=-1).astype(v.dtype)
        o = jnp.einsum('bkgqt,btkd->bqkgd', p, v)
        return o.reshape(b, Q_BLOCK, h * hd)

    return sweep_query_blocks(block, q)


def gqa_project(h, w_in, g_q, g_k):
    b, l, _ = h.shape
    q, k, v = jnp.split(h @ w_in, [ATTN_HEADS * ATTN_HD, (ATTN_HEADS + ATTN_KV_HEADS) * ATTN_HD], axis=-1)
    q = rms_norm(q.reshape(b, l, ATTN_HEADS, ATTN_HD), g_q)
    k = rms_norm(k.reshape(b, l, ATTN_KV_HEADS, ATTN_HD), g_k)
    v = v.reshape(b, l, ATTN_KV_HEADS, ATTN_HD)
    return q, k, v


def gqa_context(h, w_in, g_q, g_k, w_out):
    q, k, v = gqa_project(h, w_in, g_q, g_k)
    return gqa_attend(q, k, v) @ w_out, k, v


def gqa_latent(h, ck, cv, rope, w_in, g_q, g_k, w_out):
    q, k, v = gqa_project(h, w_in, g_q, g_k)
    q = apply_rope(q, rope)
    k = apply_rope(k, rope)
    o = gqa_attend(q, jnp.concatenate([ck, k], axis=1), jnp.concatenate([cv, v], axis=1))
    return o @ w_out


def diff_lambda_value(lam_vec, lam_init):
    lf = lam_vec.astype(F32)
    return jnp.exp(jnp.sum(lf[0] * lf[1])) - jnp.exp(jnp.sum(lf[2] * lf[3])) + lam_init


def diff_attend(q, k, v, lam):
    b = q.shape[0]
    scale = 1.0 / math.sqrt(DIFF_HD)

    def block(qb):
        s = jnp.einsum('bqhd,bthd->bhqt', qb, k, preferred_element_type=F32) * scale
        p = jax.nn.softmax(s, axis=-1).reshape(b, DIFF_HEADS, 2, Q_BLOCK, s.shape[-1])
        a = (p[:, :, 0] - lam * p[:, :, 1]).astype(v.dtype)
        return jnp.einsum('bhqt,bthe->bqhe', a, v)

    return sweep_query_blocks(block, q)


def diff_project(h, w_in):
    b, l, _ = h.shape
    q, k, v = jnp.split(h @ w_in, 3, axis=-1)
    return (q.reshape(b, l, 2 * DIFF_HEADS, DIFF_HD),
            k.reshape(b, l, 2 * DIFF_HEADS, DIFF_HD),
            v.reshape(b, l, DIFF_HEADS, 2 * DIFF_HD))


def diff_output(o, lam_init, g_sub, w_out):
    b, l = o.shape[:2]
    o = rms_norm(o, g_sub) * (1.0 - lam_init)
    return o.reshape(b, l, DIFF_HEADS * 2 * DIFF_HD) @ w_out


def diff_context(h, w_in, lam_vec, g_sub, w_out, lam_init):
    q, k, v = diff_project(h, w_in)
    lam = diff_lambda_value(lam_vec, lam_init)
    return diff_output(diff_attend(q, k, v, lam), lam_init, g_sub, w_out), k, v


def diff_latent(h, ck, cv, rope, w_in, lam_vec, g_sub, w_out, lam_init):
    q, k, v = diff_project(h, w_in)
    q = apply_rope(q, rope)
    k = apply_rope(k, rope)
    lam = diff_lambda_value(lam_vec, lam_init)
    o = diff_attend(q, jnp.concatenate([ck, k], axis=1), jnp.concatenate([cv, v], axis=1), lam)
    return diff_output(o, lam_init, g_sub, w_out)


def depthwise_conv(x, w, bias):
    y = lax.conv_general_dilated(x, w[:, None, :], window_strides=(1,),
                                 padding=[(SSD_CONV // 2, SSD_CONV // 2)],
                                 dimension_numbers=('NWC', 'WIO', 'NWC'),
                                 feature_group_count=x.shape[-1])
    return y + bias


def ssd_project(h, w_in, conv_w, conv_b):
    b, l, _ = h.shape
    z, xbc, dt = jnp.split(h @ w_in, [SSD_INNER, SSD_INNER + SSD_CONV_DIM], axis=-1)
    xbc = jax.nn.silu(depthwise_conv(xbc, conv_w, conv_b))
    x, bm, cm = jnp.split(xbc, [SSD_INNER, SSD_INNER + SSD_GN], axis=-1)
    return (z, x.reshape(b, l, SSD_HEADS, SSD_HEADDIM),
            bm.reshape(b, l, SSD_GROUPS, SSD_STATE), cm.reshape(b, l, SSD_GROUPS, SSD_STATE),
            dt.reshape(b, l, 2, SSD_HEADS))


def ssd_chunked_scan(x, dt, a, bm, cm, h0):
    b, l, nh, p = x.shape
    g, n = bm.shape[2], bm.shape[3]
    r = nh // g
    q = SSD_CHUNK
    nc = l // q

    def chunks(t):
        return jnp.moveaxis(t.reshape((b, nc, q) + t.shape[2:]), 1, 0)

    log_a = dt * a
    xdt = x * dt[..., None]
    lower = jnp.tril(jnp.ones((q, q), dtype=bool))

    def step(h, inp):
        xc, lac, bc, cc = inp
        acs = jnp.cumsum(lac, axis=1)
        seg = acs[:, :, None, :] - acs[:, None, :, :]
        decay = jnp.exp(jnp.where(lower[None, :, :, None], seg, -jnp.inf)).reshape(b, q, q, g, r)
        xg = xc.reshape(b, q, g, r, p)
        cb = jnp.einsum('bign,bjgn->bgij', cc, bc)
        y = jnp.einsum('bgij,bijgr,bjgrp->bigrp', cb, decay, xg)
        hg = h.reshape(b, g, r, p, n)
        y = y + jnp.einsum('bign,bgrpn->bigrp', cc, hg) * jnp.exp(acs).reshape(b, q, g, r)[..., None]
        to_end = jnp.exp(acs[:, -1:, :] - acs).reshape(b, q, g, r)
        h_new = (jnp.exp(acs[:, -1]).reshape(b, g, r)[..., None, None] * hg
                 + jnp.einsum('bjgr,bjgn,bjgrp->bgrpn', to_end, bc, xg))
        return h_new.reshape(b, nh, p, n), y.reshape(b, q, nh, p)

    h_last, ys = lax.scan(step, h0, (chunks(xdt), chunks(log_a), chunks(bm), chunks(cm)))
    return jnp.moveaxis(ys, 0, 1).reshape(b, l, nh, p), h_last


def flip_seq(t, reverse):
    return jnp.flip(t, axis=1) if reverse else t


def ssd_bidirectional(x, bm, cm, dt_raw, dt_bias, a_log, d_skip, h0_f, h0_b):
    xf, bf, cf = x.astype(F32), bm.astype(F32), cm.astype(F32)
    y = jnp.zeros(xf.shape, F32)
    finals = []
    for d, h0 in enumerate((h0_f, h0_b)):
        rev = d == 1
        dt = jax.nn.softplus(dt_raw[:, :, d].astype(F32) + dt_bias[d].astype(F32))
        a = -jnp.exp(a_log[d].astype(F32))
        yd, hd = ssd_chunked_scan(flip_seq(xf, rev), flip_seq(dt, rev), a,
                                  flip_seq(bf, rev), flip_seq(cf, rev), h0.astype(F32))
        y = y + flip_seq(yd, rev) + d_skip[d].astype(F32)[:, None] * xf
        finals.append(hd)
    return y, finals[0], finals[1]


def ssd_output(y, z, g_norm, w_out, dtype):
    b, l = y.shape[:2]
    gated = (y.reshape(b, l, SSD_INNER) * jax.nn.silu(z.astype(F32))).reshape(b, l, SSD_GROUPS, SSD_INNER // SSD_GROUPS)
    normed = gated * lax.rsqrt(jnp.mean(gated * gated, axis=-1, keepdims=True) + EPS)
    out = (normed.reshape(b, l, SSD_INNER) * g_norm.astype(F32)).astype(dtype)
    return out @ w_out


def ssd_context(h, w_in, conv_w, conv_b, dt_bias, a_log, d_skip, g_norm, w_out):
    z, x, bm, cm, dt = ssd_project(h, w_in, conv_w, conv_b)
    zeros = jnp.zeros((h.shape[0], SSD_HEADS, SSD_HEADDIM, SSD_STATE), F32)
    y, hf, hb = ssd_bidirectional(x, bm, cm, dt, dt_bias, a_log, d_skip, zeros, zeros)
    return ssd_output(y, z, g_norm, w_out, h.dtype), hf.astype(h.dtype), hb.astype(h.dtype)


def ssd_latent(h, s_f, s_b, w_in, conv_w, conv_b, dt_bias, a_log, d_skip, g_norm, w_out):
    z, x, bm, cm, dt = ssd_project(h, w_in, conv_w, conv_b)
    y, _, _ = ssd_bidirectional(x, bm, cm, dt, dt_bias, a_log, d_skip, s_f, s_b)
    return ssd_output(y, z, g_norm, w_out, h.dtype)


def swiglu(h, w_gu, w_down):
    a, u = jnp.split(h @ w_gu, 2, axis=-1)
    return (jax.nn.silu(a) * u) @ w_down


def moe_swiglu(h, w_router, w_gu, w_down):
    shape = h.shape
    x = h.reshape(-1, D_MODEL)
    n = x.shape[0]
    logits = jnp.dot(x, w_router, preferred_element_type=F32)
    top_logit, top_e = lax.top_k(logits, TOP_K)
    gates = jax.nn.softmax(top_logit, axis=-1)
    n_assign = n * TOP_K
    flat_e = top_e.reshape(-1).astype(jnp.int32)
    flat_tok = jnp.arange(n_assign, dtype=jnp.int32) // TOP_K
    flat_g = gates.reshape(-1)
    order = jnp.argsort(flat_e)
    sorted_e = flat_e[order]
    counts = jnp.bincount(flat_e, length=N_EXPERTS).astype(jnp.int32)
    padded = (counts + EXPERT_BLOCK - 1) // EXPERT_BLOCK * EXPERT_BLOCK
    start = jnp.cumsum(counts) - counts
    pad_end = jnp.cumsum(padded)
    pad_start = pad_end - padded
    dest = pad_start[sorted_e] + jnp.arange(n_assign, dtype=jnp.int32) - start[sorted_e]
    n_blocks = n_assign // EXPERT_BLOCK + N_EXPERTS
    slots = n_blocks * EXPERT_BLOCK
    slot_tok = jnp.zeros((slots,), jnp.int32).at[dest].set(flat_tok[order])
    slot_g = jnp.zeros((slots,), F32).at[dest].set(flat_g[order])
    block_start = jnp.arange(n_blocks, dtype=jnp.int32) * EXPERT_BLOCK
    block_e = jnp.minimum(jnp.searchsorted(pad_end, block_start, side='right'), N_EXPERTS - 1)

    def block(inp):
        tok, e = inp
        a, u = jnp.split(x[tok] @ w_gu[e], 2, axis=-1)
        return (jax.nn.silu(a) * u) @ w_down[e]

    yb = lax.map(block, (slot_tok.reshape(n_blocks, EXPERT_BLOCK), block_e))
    y = jnp.zeros((n, D_MODEL), F32).at[slot_tok].add(yb.reshape(slots, D_MODEL).astype(F32) * slot_g[:, None])
    return y.astype(h.dtype).reshape(shape)


def setup_inputs(seed: int = 0) -> dict:
    key = jax.random.key(seed)
    ks = iter(jax.random.split(key, 64))

    def nrm(shape, scale):
        return jax.random.normal(next(ks), shape, F32) * scale

    def gain(shape):
        return 1.0 + nrm(shape, 0.05)

    dt0 = jnp.exp(jax.random.uniform(next(ks), (N_SSD_LAYERS, 2, SSD_HEADS), F32, math.log(1e-3), math.log(1e-1)))
    inp = {}
    inp['x_prompt'] = nrm((BATCH, SEQ, D_MODEL), 1.0)
    inp['x_sample'] = nrm((DEC_BATCH, DEC_SEQ, D_MODEL), 1.0)
    inp['cache_attn_k'] = nrm((DEC_BATCH, N_ATTN_LAYERS, PAST_LEN, ATTN_KV_HEADS, ATTN_HD), 1.0)
    inp['cache_attn_v'] = nrm((DEC_BATCH, N_ATTN_LAYERS, PAST_LEN, ATTN_KV_HEADS, ATTN_HD), 1.0)
    inp['cache_diff_k'] = nrm((DEC_BATCH, N_DIFF_LAYERS, PAST_LEN, 2 * DIFF_HEADS, DIFF_HD), 1.0)
    inp['cache_diff_v'] = nrm((DEC_BATCH, N_DIFF_LAYERS, PAST_LEN, DIFF_HEADS, 2 * DIFF_HD), 1.0)
    inp['state_ssd_fwd'] = nrm((DEC_BATCH, N_SSD_LAYERS, SSD_HEADS, SSD_HEADDIM, SSD_STATE), 0.1)
    inp['state_ssd_bwd'] = nrm((DEC_BATCH, N_SSD_LAYERS, SSD_HEADS, SSD_HEADDIM, SSD_STATE), 0.1)
    inp['c'] = nrm((DEC_BATCH, D_MODEL), 1.0)
    inp['c_ctx'] = nrm((D_MODEL,), 1.0)
    inp['w_ada'] = nrm((DEPTH, D_MODEL, 6 * D_MODEL), 0.5 * D_MODEL ** -0.5)
    inp['b_ada'] = nrm((DEPTH, 6 * D_MODEL), 0.02)
    inp['g_mix'] = gain((DEPTH, D_MODEL))
    inp['g_ffn'] = gain((DEPTH, D_MODEL))
    inp['g_final'] = gain((D_MODEL,))
    inp['attn_w_in'] = nrm((N_ATTN_LAYERS, D_MODEL, ATTN_IN), D_MODEL ** -0.5)
    inp['attn_g_q'] = gain((N_ATTN_LAYERS, ATTN_HD))
    inp['attn_g_k'] = gain((N_ATTN_LAYERS, ATTN_HD))
    inp['attn_w_out'] = nrm((N_ATTN_LAYERS, ATTN_HEADS * ATTN_HD, D_MODEL), (ATTN_HEADS * ATTN_HD) ** -0.5)
    inp['diff_w_in'] = nrm((N_DIFF_LAYERS, D_MODEL, DIFF_IN), D_MODEL ** -0.5)
    inp['diff_lambda'] = nrm((N_DIFF_LAYERS, 4, DIFF_HD), 0.1)
    inp['diff_g_sub'] = gain((N_DIFF_LAYERS, 2 * DIFF_HD))
    inp['diff_w_out'] = nrm((N_DIFF_LAYERS, 2 * DIFF_HEADS * DIFF_HD, D_MODEL), (2 * DIFF_HEADS * DIFF_HD) ** -0.5)
    inp['ssd_w_in'] = nrm((N_SSD_LAYERS, D_MODEL, SSD_IN), D_MODEL ** -0.5)
    inp['ssd_conv_w'] = nrm((N_SSD_LAYERS, SSD_CONV, SSD_CONV_DIM), SSD_CONV ** -0.5)
    inp['ssd_conv_b'] = nrm((N_SSD_LAYERS, SSD_CONV_DIM), 0.02)
    inp['ssd_dt_bias'] = dt0 + jnp.log(-jnp.expm1(-dt0))
    inp['ssd_a_log'] = jnp.log(jax.random.uniform(next(ks), (N_SSD_LAYERS, 2, SSD_HEADS), F32, 1.0, 16.0))
    inp['ssd_d'] = gain((N_SSD_LAYERS, 2, SSD_HEADS))
    inp['ssd_g_norm'] = gain((N_SSD_LAYERS, SSD_INNER))
    inp['ssd_w_out'] = nrm((N_SSD_LAYERS, SSD_INNER, D_MODEL), SSD_INNER ** -0.5)
    inp['ffn_w_gu'] = nrm((N_DENSE_LAYERS, D_MODEL, 2 * FFN_DIM), D_MODEL ** -0.5)
    inp['ffn_w_down'] = nrm((N_DENSE_LAYERS, FFN_DIM, D_MODEL), FFN_DIM ** -0.5)
    inp['moe_w_router'] = nrm((N_MOE_LAYERS, D_MODEL, N_EXPERTS), D_MODEL ** -0.5)
    inp['moe_w_gu'] = nrm((N_MOE_LAYERS, N_EXPERTS, D_MODEL, 2 * EXPERT_DIM), D_MODEL ** -0.5)
    inp['moe_w_down'] = nrm((N_MOE_LAYERS, N_EXPERTS, EXPERT_DIM, D_MODEL), EXPERT_DIM ** -0.5)
    return inp


def reference(x_prompt, x_sample, cache_attn_k, cache_attn_v, cache_diff_k, cache_diff_v,
              state_ssd_fwd, state_ssd_bwd, c, c_ctx, w_ada, b_ada, g_mix, g_ffn, g_final,
              attn_w_in, attn_g_q, attn_g_k, attn_w_out,
              diff_w_in, diff_lambda, diff_g_sub, diff_w_out,
              ssd_w_in, ssd_conv_w, ssd_conv_b, ssd_dt_bias, ssd_a_log, ssd_d, ssd_g_norm, ssd_w_out,
              ffn_w_gu, ffn_w_down, moe_w_router, moe_w_gu, moe_w_down):
    rows = x_sample.shape[1] // GRID_W
    rope_attn = axial_rope_tables(rows, ATTN_HD)
    rope_diff = axial_rope_tables(rows, DIFF_HD)
    xp, xs = x_prompt, x_sample
    new_attn_k, new_attn_v, new_diff_k, new_diff_v, new_ssd_f, new_ssd_b = [], [], [], [], [], []
    for i in range(DEPTH):
        kind, j = i % N_MIXERS, i // N_MIXERS
        mp = modulation(c_ctx, w_ada[i], b_ada[i])
        ms = modulation(c, w_ada[i], b_ada[i])
        hp = modulate(rms_norm(xp, g_mix[i]), mp[0], mp[1])
        hs = modulate(rms_norm(xs, g_mix[i]), ms[0], ms[1])
        if kind == 0:
            op, k, v = gqa_context(hp, attn_w_in[j], attn_g_q[j], attn_g_k[j], attn_w_out[j])
            os_ = gqa_latent(hs, cache_attn_k[:, j], cache_attn_v[:, j], rope_attn,
                             attn_w_in[j], attn_g_q[j], attn_g_k[j], attn_w_out[j])
            new_attn_k.append(k)
            new_attn_v.append(v)
        elif kind == 1:
            lam_init = 0.8 - 0.6 * math.exp(-0.3 * i)
            op, k, v = diff_context(hp, diff_w_in[j], diff_lambda[j], diff_g_sub[j], diff_w_out[j], lam_init)
            os_ = diff_latent(hs, cache_diff_k[:, j], cache_diff_v[:, j], rope_diff,
                              diff_w_in[j], diff_lambda[j], diff_g_sub[j], diff_w_out[j], lam_init)
            new_diff_k.append(k)
            new_diff_v.append(v)
        else:
            op, s_f, s_b = ssd_context(hp, ssd_w_in[j], ssd_conv_w[j], ssd_conv_b[j], ssd_dt_bias[j],
                                       ssd_a_log[j], ssd_d[j], ssd_g_norm[j], ssd_w_out[j])
            os_ = ssd_latent(hs, state_ssd_fwd[:, j], state_ssd_bwd[:, j], ssd_w_in[j], ssd_conv_w[j],
                             ssd_conv_b[j], ssd_dt_bias[j], ssd_a_log[j], ssd_d[j], ssd_g_norm[j], ssd_w_out[j])
            new_ssd_f.append(s_f)
            new_ssd_b.append(s_b)
        xp = xp + mp[2] * op
        xs = xs + ms[2] * os_
        hp = modulate(rms_norm(xp, g_ffn[i]), mp[3], mp[4])
        hs = modulate(rms_norm(xs, g_ffn[i]), ms[3], ms[4])
        f = i // 2
        if i % 2 == 0:
            fp = swiglu(hp, ffn_w_gu[f], ffn_w_down[f])
            fs = swiglu(hs, ffn_w_gu[f], ffn_w_down[f])
        else:
            fp = moe_swiglu(hp, moe_w_router[f], moe_w_gu[f], moe_w_down[f])
            fs = moe_swiglu(hs, moe_w_router[f], moe_w_gu[f], moe_w_down[f])
        xp = xp + mp[5] * fp
        xs = xs + ms[5] * fs
    y_prompt = rms_norm(xp, g_final)
    y_sample = rms_norm(xs, g_final)
    return (y_prompt, y_sample,
            jnp.stack(new_attn_k, axis=1), jnp.stack(new_attn_v, axis=1),
            jnp.stack(new_diff_k, axis=1), jnp.stack(new_diff_v, axis=1),
            jnp.stack(new_ssd_f, axis=1), jnp.stack(new_ssd_b, axis=1))
```

```python
import functools
import math

import jax
import jax.numpy as jnp
from jax import lax
from jax.experimental import pallas as pl
from jax.experimental.pallas import tpu as pltpu

F32 = jnp.float32
BF16 = jnp.bfloat16

D_MODEL = 2048
DEPTH = 4
GRID_W = 64
ROPE_THETA = 10000.0
EPS = 1e-6
HD = 128
ATTN_HEADS = 16
ATTN_KV_HEADS = 4
DIFF_HEADS = 8
SSD_INNER = 2 * D_MODEL
SSD_HEADDIM = 64
SSD_HEADS = SSD_INNER // SSD_HEADDIM
SSD_GROUPS = 8
SSD_STATE = 128
SSD_CONV = 5
SSD_CHUNK = 128
SSD_GN = SSD_GROUPS * SSD_STATE
SSD_CONV_DIM = SSD_INNER + 2 * SSD_GN
FFN_DIM = 5632
N_EXPERTS = 8
TOP_K = 2
EXPERT_DIM = 7168

MOD_ROWS = 16
CTX_ROW = 8
MOE_BLOCK = 512
VMEM_LIMIT = 56 * 1024 * 1024


def _cparams(sem):
    return pltpu.CompilerParams(dimension_semantics=sem, vmem_limit_bytes=VMEM_LIMIT)


def _silu(x):
    return x / (1.0 + jnp.exp(-x))


def _split3(x):
    hi = x.astype(BF16)
    r = x - hi.astype(F32)
    mid = r.astype(BF16)
    lo = (r - mid.astype(F32)).astype(BF16)
    return hi, mid, lo


def _dot3(a3, b):
    out = jnp.dot(a3[0], b, preferred_element_type=F32)
    out = out + jnp.dot(a3[1], b, preferred_element_type=F32)
    return out + jnp.dot(a3[2], b, preferred_element_type=F32)


class _Stream:
    def __init__(self, n_prompt_batch, prompt_len, n_sample_batch, sample_len):
        self.pb, self.pl_, self.sb, self.sl = n_prompt_batch, prompt_len, n_sample_batch, sample_len
        self.n_prompt = n_prompt_batch * prompt_len
        self.n_sample = n_sample_batch * sample_len
        self.n = self.n_prompt + self.n_sample

    def mod_row(self, i, tm):
        pt = self.n_prompt // tm
        per = self.sl // tm
        return jnp.where(i < pt, CTX_ROW, (i - pt) // per)

    def tile(self, want):
        t = want
        while self.n_prompt % t or self.sl % t:
            t //= 2
        return t


def _mod_index(layer, which, row):
    return (layer * MOD_ROWS + row) * 6 + which


def _norm_mod_kernel(x_ref, g_ref, shift_ref, scale_ref, *rest, router):
    x = x_ref[...]
    y = x * lax.rsqrt(jnp.mean(x * x, axis=-1, keepdims=True) + EPS) * g_ref[...]
    h = y * (1.0 + scale_ref[...]) + shift_ref[...]
    if router:
        wr_ref, o_ref, lg_ref = rest
        w = wr_ref[...]
        whi = w.astype(BF16)
        wlo = (w - whi.astype(F32)).astype(BF16)
        h3 = _split3(h)
        lg = _dot3(h3, whi)
        lg = lg + jnp.dot(h3[0], wlo, preferred_element_type=F32)
        lg = lg + jnp.dot(h3[1], wlo, preferred_element_type=F32)
        lg_ref[...] = lg
    else:
        (o_ref,) = rest
    o_ref[...] = h.astype(o_ref.dtype)


def _norm_mod(st, x, g, mods, layer, which_shift, w_router=None):
    tm = st.tile(512)
    nt = st.n // tm
    g2 = g.reshape(1, D_MODEL)
    in_specs = [
        pl.BlockSpec((tm, D_MODEL), lambda i: (i, 0)),
        pl.BlockSpec((1, D_MODEL), lambda i: (0, 0)),
        pl.BlockSpec((None, 1, D_MODEL),
                     lambda i: (_mod_index(layer, which_shift, st.mod_row(i, tm)), 0, 0)),
        pl.BlockSpec((None, 1, D_MODEL),
                     lambda i: (_mod_index(layer, which_shift + 1, st.mod_row(i, tm)), 0, 0)),
    ]
    args = [x, g2, mods, mods]
    out_shape = [jax.ShapeDtypeStruct((st.n, D_MODEL), BF16)]
    out_specs = [pl.BlockSpec((tm, D_MODEL), lambda i: (i, 0))]
    if w_router is not None:
        wr = jnp.zeros((D_MODEL, 128), F32).at[:, :N_EXPERTS].set(w_router)
        in_specs.append(pl.BlockSpec((D_MODEL, 128), lambda i: (0, 0)))
        args.append(wr)
        out_shape.append(jax.ShapeDtypeStruct((st.n, 128), F32))
        out_specs.append(pl.BlockSpec((tm, 128), lambda i: (i, 0)))
    res = pl.pallas_call(
        functools.partial(_norm_mod_kernel, router=w_router is not None),
        grid=(nt,), in_specs=in_specs, out_specs=out_specs, out_shape=out_shape,
        compiler_params=_cparams(("parallel",)), name="norm_mod",
    )(*args)
    return res if w_router is not None else res[0]


def _final_norm_kernel(x_ref, g_ref, o_ref):
    x = x_ref[...]
    o_ref[...] = x * lax.rsqrt(jnp.mean(x * x, axis=-1, keepdims=True) + EPS) * g_ref[...]


def _final_norm(st, x, g):
    tm = st.tile(512)
    return pl.pallas_call(
        _final_norm_kernel, grid=(st.n // tm,),
        in_specs=[pl.BlockSpec((tm, D_MODEL), lambda i: (i, 0)),
                  pl.BlockSpec((1, D_MODEL), lambda i: (0, 0))],
        out_specs=pl.BlockSpec((tm, D_MODEL), lambda i: (i, 0)),
        out_shape=jax.ShapeDtypeStruct((st.n, D_MODEL), F32),
        compiler_params=_cparams(("parallel",)), name="final_norm",
    )(x, g.reshape(1, D_MODEL))


def _mm_kernel(*refs, nk, epilogue):
    a_ref, w_ref = refs[0], refs[1]
    pos = 2
    w2_ref = x_ref = gate_ref = None
    if epilogue == "swiglu":
        w2_ref = refs[pos]; pos += 1
    if epilogue == "resid":
        x_ref, gate_ref = refs[pos], refs[pos + 1]; pos += 2
    o_ref = refs[pos]; pos += 1
    acc_refs = refs[pos:]
    k = pl.program_id(2)
    a = a_ref[...]
    parts = [jnp.dot(a, w_ref[...].astype(BF16), preferred_element_type=F32)]
    if w2_ref is not None:
        parts.append(jnp.dot(a, w2_ref[...].astype(BF16), preferred_element_type=F32))

    def finish(vals):
        if epilogue == "swiglu":
            o_ref[...] = (_silu(vals[0]) * vals[1]).astype(o_ref.dtype)
        elif epilogue == "resid":
            o_ref[...] = (x_ref[...] + gate_ref[...] * vals[0]).astype(o_ref.dtype)
        else:
            o_ref[...] = vals[0].astype(o_ref.dtype)

    if nk == 1:
        finish(parts)
        return

    @pl.when(k == 0)
    def _():
        for acc, p in zip(acc_refs, parts):
            acc[...] = p

    @pl.when(jnp.logical_and(k > 0, k < nk - 1))
    def _():
        for acc, p in zip(acc_refs, parts):
            acc[...] += p

    @pl.when(k == nk - 1)
    def _():
        finish([acc[...] + p for acc, p in zip(acc_refs, parts)])


def _matmul(a, w, widx, *, n_off, n_out, tm, tn, tk, out_dtype, epilogue="plain",
            up_off=None, resid=None, mods=None, gate_index=None, name="mm"):
    m, kdim = a.shape
    nk = kdim // tk
    assert m % tm == 0 and kdim % tk == 0 and n_out % tn == 0 and n_off % tn == 0
    nb = n_off // tn
    lead = (None,) * len(widx)
    in_specs = [
        pl.BlockSpec((tm, tk), lambda i, j, k: (i, k)),
        pl.BlockSpec(lead + (tk, tn), lambda i, j, k: widx + (k, j + nb)),
    ]
    args = [a, w]
    if epilogue == "swiglu":
        ub = up_off // tn
        in_specs.append(pl.BlockSpec(lead + (tk, tn), lambda i, j, k: widx + (k, j + ub)))
        args.append(w)
    if epilogue == "resid":
        in_specs.append(pl.BlockSpec((tm, tn), lambda i, j, k: (i, j)))
        in_specs.append(pl.BlockSpec((None, 1, tn), lambda i, j, k: (gate_index(i), 0, j)))
        args += [resid, mods]
    n_acc = 0 if nk == 1 else (2 if epilogue == "swiglu" else 1)
    return pl.pallas_call(
        functools.partial(_mm_kernel, nk=nk, epilogue=epilogue),
        grid=(m // tm, n_out // tn, nk),
        in_specs=in_specs,
        out_specs=pl.BlockSpec((tm, tn), lambda i, j, k: (i, j)),
        out_shape=jax.ShapeDtypeStruct((m, n_out), out_dtype),
        scratch_shapes=[pltpu.VMEM((tm, tn), F32)] * n_acc,
        compiler_params=_cparams(("parallel", "parallel", "arbitrary")), name=name,
    )(*args)


def _rope_tables(length):
    nf = HD // 4
    inv = ROPE_THETA ** (-jnp.arange(nf, dtype=F32) / nf)
    rows = length // GRID_W
    row = jnp.repeat(jnp.arange(rows, dtype=F32), GRID_W)
    col = jnp.tile(jnp.arange(GRID_W, dtype=F32), rows)
    ar = row[:, None] * inv
    ac = col[:, None] * inv
    ang = jnp.concatenate([ar, ar, ac, ac], axis=-1)
    cos, sin = jnp.cos(ang), jnp.sin(ang)
    lane = jnp.arange(HD) % (HD // 2)
    second = lane >= HD // 4
    sin_a = jnp.where(second, sin, 0.0)
    sin_b = jnp.where(second, 0.0, -sin)
    return cos, sin_a, sin_b


def _rope(y, cos, sin_a, sin_b):
    return y * cos + pltpu.roll(y, HD // 4, 1) * sin_a + pltpu.roll(y, HD - HD // 4, 1) * sin_b


def _gqa_prep_kernel(q_ref, k_ref, gq_ref, gk_ref, *rest, rope):
    if rope:
        cos_ref, sa_ref, sb_ref, qo_ref, ko_ref = rest
        cos, sa, sb = cos_ref[...], sa_ref[...], sb_ref[...]
    else:
        qo_ref, ko_ref = rest
    scale = 1.0 / math.sqrt(HD)

    def norm(x, g):
        return x * lax.rsqrt(jnp.mean(x * x, axis=-1, keepdims=True) + EPS) * g

    for h in range(ATTN_HEADS):
        y = norm(q_ref[:, h * HD:(h + 1) * HD], gq_ref[...])
        if rope:
            y = _rope(y, cos, sa, sb)
        qo_ref[:, h * HD:(h + 1) * HD] = (y * scale).astype(qo_ref.dtype)
    for h in range(ATTN_KV_HEADS):
        y = norm(k_ref[:, h * HD:(h + 1) * HD], gk_ref[...])
        if rope:
            y = _rope(y, cos, sa, sb)
        ko_ref[:, h * HD:(h + 1) * HD] = y


def _gqa_prep(qkv, g_q, g_k, row0, nrows, seq, tables):
    tm = min(256, seq)
    r0 = row0 // tm
    per = seq // tm
    qw = ATTN_HEADS * HD
    kw = ATTN_KV_HEADS * HD
    in_specs = [
        pl.BlockSpec((tm, qw), lambda i: (i + r0, 0)),
        pl.BlockSpec((tm, kw), lambda i: (i + r0, qw // kw)),
        pl.BlockSpec((1, HD), lambda i: (0, 0)),
        pl.BlockSpec((1, HD), lambda i: (0, 0)),
    ]
    args = [qkv, qkv, g_q.reshape(1, HD), g_k.reshape(1, HD)]
    if tables is not None:
        in_specs += [pl.BlockSpec((tm, HD), lambda i: (i % per, 0))] * 3
        args += list(tables)
    return pl.pallas_call(
        functools.partial(_gqa_prep_kernel, rope=tables is not None),
        grid=(nrows // tm,), in_specs=in_specs,
        out_specs=[pl.BlockSpec((tm, qw), lambda i: (i, 0)),
                   pl.BlockSpec((tm, kw), lambda i: (i, 0))],
        out_shape=[jax.ShapeDtypeStruct((nrows, qw), BF16),
                   jax.ShapeDtypeStruct((nrows, kw), F32)],
        compiler_params=_cparams(("parallel",)), name="gqa_prep",
    )(*args)


def _softmax_parts(q, ks):
    ss = [lax.dot_general(q, k, (((1,), (1,)), ((), ())), preferred_element_type=F32) for k in ks]
    m = ss[0].max(axis=-1, keepdims=True)
    for s in ss[1:]:
        m = jnp.maximum(m, s.max(axis=-1, keepdims=True))
    ps = [jnp.exp(s - m) for s in ss]
    l = ps[0].sum(axis=-1, keepdims=True)
    for p in ps[1:]:
        l = l + p.sum(axis=-1, keepdims=True)
    return ps, l


def _gqa_attn_kernel(q_ref, k_ref, v_ref, *rest, cache):
    if cache:
        ck_ref, cv_ref, o_ref = rest
        ks = [ck_ref[...].astype(BF16), k_ref[...].astype(BF16)]
        vs = [cv_ref[...].astype(BF16), v_ref[...].astype(BF16)]
    else:
        (o_ref,) = rest
        ks = [k_ref[...].astype(BF16)]
        vs = [v_ref[...].astype(BF16)]
    grp = ATTN_HEADS // ATTN_KV_HEADS
    for h in range(grp):
        q = q_ref[:, h * HD:(h + 1) * HD]
        ps, l = _softmax_parts(q, ks)
        o = jnp.dot(ps[0].astype(BF16), vs[0], preferred_element_type=F32)
        for p, v in zip(ps[1:], vs[1:]):
            o = o + jnp.dot(p.astype(BF16), v, preferred_element_type=F32)
        o_ref[:, h * HD:(h + 1) * HD] = (o / l).astype(o_ref.dtype)


def _gqa_attn(q, k, qkv, row0, nb, seq, cache_k, cache_v, layer_j):
    tq = min(512, seq)
    nq = seq // tq
    grp_w = (ATTN_HEADS // ATTN_KV_HEADS) * HD
    v_col0 = (ATTN_HEADS + ATTN_KV_HEADS)
    vb0 = row0 // seq
    in_specs = [
        pl.BlockSpec((tq, grp_w), lambda b, g, i: (b * nq + i, g)),
        pl.BlockSpec((seq, HD), lambda b, g, i: (b, g)),
        pl.BlockSpec((seq, HD), lambda b, g, i: (b + vb0, v_col0 + g)),
    ]
    args = [q, k, qkv]
    cache = cache_k is not None
    if cache:
        past = cache_k.shape[2]
        ck = cache_k.reshape(cache_k.shape[0], cache_k.shape[1], past, ATTN_KV_HEADS * HD)
        cv = cache_v.reshape(ck.shape)
        in_specs += [pl.BlockSpec((None, None, past, HD), lambda b, g, i: (b, layer_j, 0, g))] * 2
        args += [ck, cv]
    return pl.pallas_call(
        functools.partial(_gqa_attn_kernel, cache=cache),
        grid=(nb, ATTN_KV_HEADS, nq), in_specs=in_specs,
        out_specs=pl.BlockSpec((tq, grp_w), lambda b, g, i: (b * nq + i, g)),
        out_shape=jax.ShapeDtypeStruct((nb * seq, ATTN_HEADS * HD), BF16),
        compiler_params=_cparams(("parallel", "parallel", "parallel")), name="gqa_attn",
    )(*args)


def _diff_prep_kernel(q_ref, k_ref, *rest, rope):
    if rope:
        cos_ref, sa_ref, sb_ref, qo_ref, ko_ref = rest
        cos, sa, sb = cos_ref[...], sa_ref[...], sb_ref[...]
    else:
        qo_ref, ko_ref = rest
    scale = 1.0 / math.sqrt(HD)
    for h in range(2 * DIFF_HEADS):
        y = q_ref[:, h * HD:(h + 1) * HD]
        if rope:
            y = _rope(y, cos, sa, sb)
        qo_ref[:, h * HD:(h + 1) * HD] = (y * scale).astype(qo_ref.dtype)
    for h in range(2 * DIFF_HEADS):
        y = k_ref[:, h * HD:(h + 1) * HD]
        if rope:
            y = _rope(y, cos, sa, sb)
        ko_ref[:, h * HD:(h + 1) * HD] = y.astype(ko_ref.dtype)


def _diff_prep(qkv, row0, nrows, seq, tables):
    tm = min(256, seq)
    r0 = row0 // tm
    per = seq // tm
    w = 2 * DIFF_HEADS * HD
    in_specs = [pl.BlockSpec((tm, w), lambda i: (i + r0, 0)),
                pl.BlockSpec((tm, w), lambda i: (i + r0, 1))]
    args = [qkv, qkv]
    if tables is not None:
        in_specs += [pl.BlockSpec((tm, HD), lambda i: (i % per, 0))] * 3
        args += list(tables)
    return pl.pallas_call(
        functools.partial(_diff_prep_kernel, rope=tables is not None),
        grid=(nrows // tm,), in_specs=in_specs,
        out_specs=[pl.BlockSpec((tm, w), lambda i: (i, 0))] * 2,
        out_shape=[jax.ShapeDtypeStruct((nrows, w), BF16)] * 2,
        compiler_params=_cparams(("parallel",)), name="diff_prep",
    )(*args)


def _diff_attn_kernel(lam_ref, q_ref, k_ref, v_ref, g_ref, *rest, cache, out_scale):
    if cache:
        ck_ref, cv_ref, o_ref = rest
        vs = [cv_ref[...].astype(BF16), v_ref[...].astype(BF16)]
    else:
        (o_ref,) = rest
        vs = [v_ref[...].astype(BF16)]
    lam = lam_ref[0]
    outs = []
    for m in range(2):
        q = q_ref[:, m * HD:(m + 1) * HD]
        ks = [k_ref[:, m * HD:(m + 1) * HD]]
        if cache:
            ks = [ck_ref[:, m * HD:(m + 1) * HD].astype(BF16)] + ks
        ps, l = _softmax_parts(q, ks)
        o = jnp.dot(ps[0].astype(BF16), vs[0], preferred_element_type=F32)
        for p, v in zip(ps[1:], vs[1:]):
            o = o + jnp.dot(p.astype(BF16), v, preferred_element_type=F32)
        outs.append(o / l)
    o = outs[0] - lam * outs[1]
    o = o * lax.rsqrt(jnp.mean(o * o, axis=-1, keepdims=True) + EPS) * g_ref[...]
    o_ref[...] = (o * out_scale).astype(o_ref.dtype)


def _diff_attn(lam, q, k, qkv, g_sub, lam_init, row0, nb, seq, cache_k, cache_v, layer_j):
    tq = min(512, seq)
    nq = seq // tq
    hw = 2 * HD
    vb0 = row0 // seq
    v_col0 = 2 * (2 * DIFF_HEADS * HD) // hw
    in_specs = [
        pl.BlockSpec(memory_space=pltpu.SMEM),
        pl.BlockSpec((tq, hw), lambda b, h, i: (b * nq + i, h)),
        pl.BlockSpec((seq, hw), lambda b, h, i: (b, h)),
        pl.BlockSpec((seq, hw), lambda b, h, i: (b + vb0, v_col0 + h)),
        pl.BlockSpec((1, hw), lambda b, h, i: (0, 0)),
    ]
    args = [lam, q, k, qkv, g_sub.reshape(1, hw)]
    cache = cache_k is not None
    if cache:
        past = cache_k.shape[2]
        ck = cache_k.reshape(cache_k.shape[0], cache_k.shape[1], past, 2 * DIFF_HEADS * HD)
        cv = cache_v.reshape(ck.shape)
        in_specs += [pl.BlockSpec((None, None, past, hw), lambda b, h, i: (b, layer_j, 0, h))] * 2
        args += [ck, cv]
    return pl.pallas_call(
        functools.partial(_diff_attn_kernel, cache=cache, out_scale=1.0 - lam_init),
        grid=(nb, DIFF_HEADS, nq), in_specs=in_specs,
        out_specs=pl.BlockSpec((tq, hw), lambda b, h, i: (b * nq + i, h)),
        out_shape=jax.ShapeDtypeStruct((nb * seq, DIFF_HEADS * hw), BF16),
        compiler_params=_cparams(("parallel", "parallel", "parallel")), name="diff_attn",
    )(*args)


def _conv_kernel(x_ref, w_ref, b_ref, o_ref, pad_ref):
    seq = x_ref.shape[0]
    half = SSD_CONV // 2
    zeros = jnp.zeros((8, x_ref.shape[1]), F32)
    pad_ref[0:8, :] = zeros
    pad_ref[seq + 8:seq + 16, :] = zeros
    pad_ref[8:seq + 8, :] = x_ref[...]
    acc = jnp.broadcast_to(b_ref[...], x_ref.shape)
    for k in range(SSD_CONV):
        acc = acc + w_ref[k:k + 1, :] * pad_ref[8 - half + k:8 - half + k + seq, :]
    o_ref[...] = _silu(acc)


def _ssd_conv(xbc, conv_w, conv_b, row0, nb, seq):
    tc = 512
    b0 = row0 // seq
    return pl.pallas_call(
        _conv_kernel, grid=(nb, SSD_CONV_DIM // tc),
        in_specs=[pl.BlockSpec((seq, tc), lambda b, c: (b + b0, c)),
                  pl.BlockSpec((SSD_CONV, tc), lambda b, c: (0, c)),
                  pl.BlockSpec((1, tc), lambda b, c: (0, c))],
        out_specs=pl.BlockSpec((seq, tc), lambda b, c: (b, c)),
        out_shape=jax.ShapeDtypeStruct((nb * seq, SSD_CONV_DIM), F32),
        scratch_shapes=[pltpu.VMEM((seq + 16, tc), F32)],
        compiler_params=_cparams(("parallel", "parallel")), name="ssd_conv",
    )(xbc, conv_w, conv_b.reshape(1, SSD_CONV_DIM))


def _ssd_scan_kernel(x_ref, b_ref, c_ref, dt_ref, bias_ref, alog_ref, *rest,
                     rev, nc, has_init, want_final):
    pos = 0
    h0_ref = None
    if has_init:
        h0_ref = rest[pos]; pos += 1
    y_ref = rest[pos]; pos += 1
    hf_ref = None
    if want_final:
        hf_ref = rest[pos]; pos += 1
    state_ref, acol_ref, arow_ref, dtrow_ref, wrow_ref, dec_ref = rest[pos:]
    c = pl.program_id(1)
    g = pl.program_id(2)
    q = SSD_CHUNK
    lane0 = SSD_HEADS if rev else 0
    gw = (SSD_HEADS // SSD_GROUPS) * SSD_HEADDIM
    r_heads = SSD_HEADS // SSD_GROUPS

    ii = lax.broadcasted_iota(jnp.int32, (q, q), 0)
    jj = lax.broadcasted_iota(jnp.int32, (q, q), 1)
    causal = (jj >= ii) if rev else (jj <= ii)

    @pl.when(jnp.logical_and(c == 0, g == 0))
    def _():
        if has_init:
            for blk in range(SSD_INNER // 128):
                state_ref[:, blk * 128:(blk + 1) * 128] = h0_ref[blk * 128:(blk + 1) * 128, :].T
        else:
            state_ref[...] = jnp.zeros(state_ref.shape, F32)

    @pl.when(g == 0)
    def _():
        x = dt_ref[...] + bias_ref[...]
        dt = jnp.maximum(x, 0.0) + jnp.log(1.0 + jnp.exp(-jnp.abs(x)))
        la = dt * (-jnp.exp(alog_ref[...]))
        tri = causal.astype(BF16)
        tri_t = ((ii >= jj) if rev else (ii <= jj)).astype(BF16)
        la3 = _split3(la)
        acol = (jnp.dot(tri, la3[0], preferred_element_type=F32)
                + jnp.dot(tri, la3[1], preferred_element_type=F32)
                + jnp.dot(tri, la3[2], preferred_element_type=F32))
        la_t = la.T
        arow = _dot3(_split3(la_t), tri_t)
        end = q - 1 if not rev else 0
        a_end = jnp.broadcast_to(arow[:, end:end + 1], arow.shape)
        dt_t = dt.T
        acol_ref[...] = acol
        arow_ref[...] = arow
        dtrow_ref[...] = dt_t
        wrow_ref[...] = jnp.exp(a_end - arow) * dt_t
        dec_ref[...] = jnp.exp(a_end)

    kk = lax.broadcasted_iota(jnp.int32, (128, 128), 0)
    ll = lax.broadcasted_iota(jnp.int32, (128, 128), 1)
    sel = (kk == lane0 + g * r_heads + (ll % r_heads)).astype(BF16)
    acol_g = _dot3(_split3(acol_ref[...]), sel)
    row0 = pl.multiple_of(lane0 + g * r_heads, r_heads)
    arow_g = arow_ref[pl.ds(row0, r_heads), :]
    dtrow_g = dtrow_ref[pl.ds(row0, r_heads), :]
    wrow_g = wrow_ref[pl.ds(row0, r_heads), :]
    dec_g = dec_ref[pl.ds(row0, r_heads), :]

    bmat = b_ref[...]
    cmat = c_ref[...]
    b16 = bmat.astype(BF16)
    c16 = cmat.astype(BF16)
    cb = lax.dot_general(c16, b16, (((1,), (1,)), ((), ())), preferred_element_type=F32)
    b_t = bmat.T
    col0 = pl.multiple_of(g * gw, gw)
    lane_lo = lax.broadcasted_iota(jnp.int32, (1, 128), 1) < SSD_HEADDIM
    zeros_blk = jnp.zeros((SSD_STATE, q), BF16)

    for pair in range(r_heads // 2):
        x_pair = x_ref[:, pair * 128:(pair + 1) * 128].astype(BF16)
        st_old = state_ref[:, pl.ds(col0 + pair * 128, 128)]
        rhs = jnp.concatenate([x_pair, st_old.astype(BF16)], axis=0)
        res = []
        for sub in range(2):
            r = 2 * pair + sub
            a_i = jnp.broadcast_to(acol_g[:, r:r + 1], (q, q))
            seg = a_i - arow_g[r:r + 1, :]
            decay = jnp.exp(jnp.where(causal, seg, -jnp.inf))
            m_intra = (cb * decay * dtrow_g[r:r + 1, :]).astype(BF16)
            c_scaled = (cmat * jnp.exp(a_i)).astype(BF16)
            bw_t = (b_t * wrow_g[r:r + 1, :]).astype(BF16)
            lhs = jnp.concatenate(
                [jnp.concatenate([m_intra, c_scaled], axis=1),
                 jnp.concatenate([bw_t, zeros_blk], axis=1)], axis=0)
            res.append(jnp.dot(lhs, rhs, preferred_element_type=F32))
        y_ref[:, pair * 128:(pair + 1) * 128] = jnp.where(lane_lo, res[0][:q], res[1][:q])
        dec_pair = jnp.where(lane_lo, dec_g[2 * pair:2 * pair + 1, :],
                             dec_g[2 * pair + 1:2 * pair + 2, :])
        upd = jnp.where(lane_lo, res[0][q:], res[1][q:])
        state_ref[:, pl.ds(col0 + pair * 128, 128)] = dec_pair * st_old + upd

    if want_final:
        @pl.when(jnp.logical_and(c == nc - 1, g == SSD_GROUPS - 1))
        def _():
            for blk in range(SSD_INNER // 128):
                hf_ref[blk * 128:(blk + 1) * 128, :] = state_ref[:, blk * 128:(blk + 1) * 128].T


def _ssd_scan(xbc, dt, dt_bias, a_log, nb, seq, rev, h0, layer_j, want_final):
    nc = seq // SSD_CHUNK
    gw = SSD_INNER // SSD_GROUPS
    b_col0 = SSD_INNER // SSD_STATE
    c_col0 = (SSD_INNER + SSD_GN) // SSD_STATE

    def chunk(b, c):
        return b * nc + (nc - 1 - c if rev else c)

    in_specs = [
        pl.BlockSpec((SSD_CHUNK, gw), lambda b, c, g: (chunk(b, c), g)),
        pl.BlockSpec((SSD_CHUNK, SSD_STATE), lambda b, c, g: (chunk(b, c), b_col0 + g)),
        pl.BlockSpec((SSD_CHUNK, SSD_STATE), lambda b, c, g: (chunk(b, c), c_col0 + g)),
        pl.BlockSpec((SSD_CHUNK, 128), lambda b, c, g: (chunk(b, c), 0)),
        pl.BlockSpec((1, 128), lambda b, c, g: (0, 0)),
        pl.BlockSpec((1, 128), lambda b, c, g: (0, 0)),
    ]
    args = [xbc, xbc, xbc, dt, dt_bias.reshape(1, 128), a_log.reshape(1, 128)]
    if h0 is not None:
        h0r = h0.reshape(h0.shape[0], h0.shape[1], SSD_INNER, SSD_STATE)
        in_specs.append(pl.BlockSpec((None, None, SSD_INNER, SSD_STATE),
                                     lambda b, c, g: (b, layer_j, 0, 0)))
        args.append(h0r)
    out_shape = [jax.ShapeDtypeStruct((nb * seq, SSD_INNER), F32)]
    out_specs = [pl.BlockSpec((SSD_CHUNK, gw), lambda b, c, g: (chunk(b, c), g))]
    if want_final:
        out_shape.append(jax.ShapeDtypeStruct((nb, SSD_INNER, SSD_STATE), F32))
        out_specs.append(pl.BlockSpec((None, SSD_INNER, SSD_STATE), lambda b, c, g: (b, 0, 0)))
    res = pl.pallas_call(
        functools.partial(_ssd_scan_kernel, rev=rev, nc=nc, has_init=h0 is not None,
                          want_final=want_final),
        grid=(nb, nc, SSD_GROUPS), in_specs=in_specs, out_specs=out_specs, out_shape=out_shape,
        scratch_shapes=[pltpu.VMEM((SSD_STATE, SSD_INNER), F32),
                        pltpu.VMEM((SSD_CHUNK, 128), F32)] + [pltpu.VMEM((128, SSD_CHUNK), F32)] * 4,
        compiler_params=_cparams(("parallel", "arbitrary", "arbitrary")), name="ssd_scan",
    )(*args)
    return res


def _ssd_gate_kernel(yf_ref, yb_ref, x_ref, z_ref, d_ref, g_ref, o_ref):
    gw = SSD_INNER // SSD_GROUPS
    for grp in range(SSD_GROUPS):
        sl = slice(grp * gw, (grp + 1) * gw)
        y = yf_ref[:, sl] + yb_ref[:, sl] + d_ref[:, sl] * x_ref[:, sl]
        gated = y * _silu(z_ref[:, sl])
        normed = gated * lax.rsqrt(jnp.mean(gated * gated, axis=-1, keepdims=True) + EPS)
        o_ref[:, sl] = (normed * g_ref[:, sl]).astype(o_ref.dtype)


def _ssd_gate(yf, yb, xbc, z, d_sum, g_norm):
    n = yf.shape[0]
    tm = 256
    row = pl.BlockSpec((tm, SSD_INNER), lambda i: (i, 0))
    vec = pl.BlockSpec((1, SSD_INNER), lambda i: (0, 0))
    return pl.pallas_call(
        _ssd_gate_kernel, grid=(n // tm,),
        in_specs=[row, row, row, row, vec, vec], out_specs=row,
        out_shape=jax.ShapeDtypeStruct((n, SSD_INNER), BF16),
        compiler_params=_cparams(("parallel",)), name="ssd_gate",
    )(yf, yb, xbc, z, d_sum, g_norm.reshape(1, SSD_INNER))


def _moe_up_kernel(be_ref, nu_ref, a_ref, wg_ref, wu_ref, o_ref):
    m = pl.program_id(1)

    @pl.when(m < nu_ref[0])
    def _():
        a = a_ref[...]
        gate = jnp.dot(a, wg_ref[...].astype(BF16), preferred_element_type=F32)
        up = jnp.dot(a, wu_ref[...].astype(BF16), preferred_element_type=F32)
        o_ref[...] = (_silu(gate) * up).astype(o_ref.dtype)

    @pl.when(m >= nu_ref[0])
    def _():
        o_ref[...] = jnp.zeros(o_ref.shape, o_ref.dtype)


def _moe_down_kernel(be_ref, nu_ref, a_ref, w_ref, sg_ref, o_ref, acc_ref, *, nk):
    m = pl.program_id(1)
    k = pl.program_id(2)
    live = m < nu_ref[0]

    @pl.when(live)
    def _():
        p = jnp.dot(a_ref[...], w_ref[...].astype(BF16), preferred_element_type=F32)

        @pl.when(k == 0)
        def _():
            acc_ref[...] = p

        @pl.when(jnp.logical_and(k > 0, k < nk - 1))
        def _():
            acc_ref[...] += p

        @pl.when(k == nk - 1)
        def _():
            o_ref[...] = (acc_ref[...] + p) * sg_ref[...]

    @pl.when(jnp.logical_and(jnp.logical_not(live), k == nk - 1))
    def _():
        o_ref[...] = jnp.zeros(o_ref.shape, o_ref.dtype)


def _moe(st, h, logits, w_gu, w_down, f):
    n = st.n
    tb = MOE_BLOCK
    n_assign = n * TOP_K
    n_blocks = n_assign // tb + N_EXPERTS
    slots = n_blocks * tb
    top_logit, top_e = lax.top_k(logits, TOP_K)
    gates = jax.nn.softmax(top_logit, axis=-1)
    flat_e = top_e.reshape(-1).astype(jnp.int32)
    onehot = (flat_e[:, None] == jnp.arange(N_EXPERTS, dtype=jnp.int32)[None, :]).astype(jnp.int32)
    rank = jnp.sum((jnp.cumsum(onehot, axis=0) - onehot) * onehot, axis=1)
    counts = jnp.sum(onehot, axis=0)
    padded = (counts + tb - 1) // tb * tb
    pad_end = jnp.cumsum(padded)
    pad_start = pad_end - padded
    dest = pad_start[flat_e] + rank
    slot_tok = jnp.zeros((slots,), jnp.int32).at[dest].set(jnp.arange(n_assign, dtype=jnp.int32) // TOP_K)
    slot_g = jnp.zeros((slots,), F32).at[dest].set(gates.reshape(-1))
    block_start = jnp.arange(n_blocks, dtype=jnp.int32) * tb
    block_e = jnp.minimum(jnp.searchsorted(pad_end, block_start, side="right"),
                          N_EXPERTS - 1).astype(jnp.int32)
    n_used = (pad_end[-1] // tb).astype(jnp.int32).reshape(1)

    xs = jnp.take(h, slot_tok, axis=0)

    tn = 512
    nf = EXPERT_DIM // tn

    def live_block(m, nu):
        return jnp.minimum(m, nu[0] - 1)

    act = pl.pallas_call(
        _moe_up_kernel,
        grid_spec=pltpu.PrefetchScalarGridSpec(
            num_scalar_prefetch=2, grid=(nf, n_blocks),
            in_specs=[
                pl.BlockSpec((tb, D_MODEL), lambda j, m, be, nu: (live_block(m, nu), 0)),
                pl.BlockSpec((None, None, D_MODEL, tn),
                             lambda j, m, be, nu: (f, be[live_block(m, nu)], 0, j)),
                pl.BlockSpec((None, None, D_MODEL, tn),
                             lambda j, m, be, nu: (f, be[live_block(m, nu)], 0, j + nf)),
            ],
            out_specs=pl.BlockSpec((tb, tn), lambda j, m, be, nu: (m, j)),
        ),
        out_shape=jax.ShapeDtypeStruct((slots, EXPERT_DIM), BF16),
        compiler_params=_cparams(("parallel", "arbitrary")), name="moe_up",
    )(block_e, n_used, xs, w_gu, w_gu)

    tk = 1792
    nk = EXPERT_DIM // tk
    tn2 = 1024
    yb = pl.pallas_call(
        functools.partial(_moe_down_kernel, nk=nk),
        grid_spec=pltpu.PrefetchScalarGridSpec(
            num_scalar_prefetch=2, grid=(D_MODEL // tn2, n_blocks, nk),
            in_specs=[
                pl.BlockSpec((tb, tk), lambda j, m, k, be, nu: (live_block(m, nu), k)),
                pl.BlockSpec((None, None, tk, tn2),
                             lambda j, m, k, be, nu: (f, be[live_block(m, nu)], k, j)),
                pl.BlockSpec((tb, 1), lambda j, m, k, be, nu: (m, 0)),
            ],
            out_specs=pl.BlockSpec((tb, tn2), lambda j, m, k, be, nu: (m, j)),
            scratch_shapes=[pltpu.VMEM((tb, tn2), F32)],
        ),
        out_shape=jax.ShapeDtypeStruct((slots, D_MODEL), F32),
        compiler_params=_cparams(("parallel", "arbitrary", "arbitrary")), name="moe_down",
    )(block_e, n_used, act, w_down, slot_g.reshape(slots, 1))

    dest2 = dest.reshape(n, TOP_K)
    return jnp.take(yb, dest2[:, 0], axis=0), jnp.take(yb, dest2[:, 1], axis=0)


def _combine_kernel(x_ref, y0_ref, y1_ref, gate_ref, o_ref):
    o_ref[...] = x_ref[...] + gate_ref[...] * (y0_ref[...] + y1_ref[...])


def _combine(st, x, y0, y1, mods, layer):
    tm = st.tile(512)
    row = pl.BlockSpec((tm, D_MODEL), lambda i: (i, 0))
    return pl.pallas_call(
        _combine_kernel, grid=(st.n // tm,),
        in_specs=[row, row, row,
                  pl.BlockSpec((None, 1, D_MODEL),
                               lambda i: (_mod_index(layer, 5, st.mod_row(i, tm)), 0, 0))],
        out_specs=row, out_shape=jax.ShapeDtypeStruct((st.n, D_MODEL), F32),
        compiler_params=_cparams(("parallel",)), name="moe_combine",
    )(x, y0, y1, mods)


def kernel(x_prompt, x_sample, cache_attn_k, cache_attn_v, cache_diff_k, cache_diff_v, state_ssd_fwd, state_ssd_bwd, c, c_ctx, w_ada, b_ada, g_mix, g_ffn, g_final, attn_w_in, attn_g_q, attn_g_k, attn_w_out, diff_w_in, diff_lambda, diff_g_sub, diff_w_out, ssd_w_in, ssd_conv_w, ssd_conv_b, ssd_dt_bias, ssd_a_log, ssd_d, ssd_g_norm, ssd_w_out, ffn_w_gu, ffn_w_down, moe_w_router, moe_w_gu, moe_w_down):
    pb, pl_len, _ = x_prompt.shape
    sb, sl_len, _ = x_sample.shape
    st = _Stream(pb, pl_len, sb, sl_len)
    np_, ns = st.n_prompt, st.n_sample
    tm = st.tile(1024)

    x = jnp.concatenate([x_prompt.reshape(np_, D_MODEL), x_sample.reshape(ns, D_MODEL)], axis=0)

    cond = jnp.zeros((MOD_ROWS, D_MODEL), F32).at[:sb].set(c).at[CTX_ROW].set(c_ctx)
    cond_act = jax.nn.silu(cond).astype(BF16)
    mods = []
    for i in range(DEPTH):
        m = _matmul(cond_act, w_ada, (i,), n_off=0, n_out=6 * D_MODEL, tm=MOD_ROWS, tn=1024,
                    tk=D_MODEL, out_dtype=F32, name="ada")
        mods.append(m + b_ada[i][None, :])
    mods = jnp.stack(mods).reshape(DEPTH * MOD_ROWS * 6, 1, D_MODEL)

    tables = _rope_tables(sl_len)

    def gate_index(layer, which):
        return lambda i: _mod_index(layer, which, st.mod_row(i, tm))

    new_attn_k, new_attn_v, new_diff_k, new_diff_v, new_ssd_f, new_ssd_b = [], [], [], [], [], []
    for i in range(DEPTH):
        kind, j = i % 3, i // 3
        h = _norm_mod(st, x, g_mix[i], mods, i, 0)
        if kind == 0:
            qkv = _matmul(h, attn_w_in, (j,), n_off=0, n_out=attn_w_in.shape[-1], tm=tm, tn=1024,
                          tk=D_MODEL, out_dtype=F32, name="attn_in")
            qp, kp = _gqa_prep(qkv, attn_g_q[j], attn_g_k[j], 0, np_, pl_len, None)
            qs, ks = _gqa_prep(qkv, attn_g_q[j], attn_g_k[j], np_, ns, sl_len, tables)
            op = _gqa_attn(qp, kp, qkv, 0, pb, pl_len, None, None, j)
            os_ = _gqa_attn(qs, ks, qkv, np_, sb, sl_len, cache_attn_k, cache_attn_v, j)
            o = jnp.concatenate([op, os_], axis=0)
            new_attn_k.append(kp.reshape(pb, pl_len, ATTN_KV_HEADS, HD))
            new_attn_v.append(qkv[:np_, (ATTN_HEADS + ATTN_KV_HEADS) * HD:]
                              .reshape(pb, pl_len, ATTN_KV_HEADS, HD))
            x = _matmul(o, attn_w_out, (j,), n_off=0, n_out=D_MODEL, tm=tm, tn=512, tk=2048,
                        out_dtype=F32, epilogue="resid", resid=x, mods=mods,
                        gate_index=gate_index(i, 2), name="attn_out")
        elif kind == 1:
            lam_init = 0.8 - 0.6 * math.exp(-0.3 * i)
            lf = diff_lambda[j]
            lam = (jnp.exp(jnp.sum(lf[0] * lf[1])) - jnp.exp(jnp.sum(lf[2] * lf[3])) + lam_init).reshape(1)
            qkv = _matmul(h, diff_w_in, (j,), n_off=0, n_out=diff_w_in.shape[-1], tm=tm, tn=1024,
                          tk=D_MODEL, out_dtype=F32, name="diff_in")
            w = 2 * DIFF_HEADS * HD
            qp, kp = _diff_prep(qkv, 0, np_, pl_len, None)
            qs, ks = _diff_prep(qkv, np_, ns, sl_len, tables)
            op = _diff_attn(lam, qp, kp, qkv, diff_g_sub[j], lam_init, 0, pb, pl_len, None, None, j)
            os_ = _diff_attn(lam, qs, ks, qkv, diff_g_sub[j], lam_init, np_, sb, sl_len,
                             cache_diff_k, cache_diff_v, j)
            o = jnp.concatenate([op, os_], axis=0)
            new_diff_k.append(qkv[:np_, w:2 * w].reshape(pb, pl_len, 2 * DIFF_HEADS, HD))
            new_diff_v.append(qkv[:np_, 2 * w:].reshape(pb, pl_len, DIFF_HEADS, 2 * HD))
            x = _matmul(o, diff_w_out, (j,), n_off=0, n_out=D_MODEL, tm=tm, tn=512, tk=2048,
                        out_dtype=F32, epilogue="resid", resid=x, mods=mods,
                        gate_index=gate_index(i, 2), name="diff_out")
        else:
            z = _matmul(h, ssd_w_in, (j,), n_off=0, n_out=SSD_INNER, tm=tm, tn=1024, tk=D_MODEL,
                        out_dtype=F32, name="ssd_in_z")
            xbc = _matmul(h, ssd_w_in, (j,), n_off=SSD_INNER, n_out=SSD_CONV_DIM, tm=tm, tn=1024,
                          tk=D_MODEL, out_dtype=F32, name="ssd_in_xbc")
            dt = _matmul(h, ssd_w_in, (j,), n_off=SSD_INNER + SSD_CONV_DIM, n_out=2 * SSD_HEADS,
                         tm=tm, tn=128, tk=D_MODEL, out_dtype=F32, name="ssd_in_dt")
            xc_p = _ssd_conv(xbc, ssd_conv_w[j], ssd_conv_b[j], 0, pb, pl_len)
            xc_s = _ssd_conv(xbc, ssd_conv_w[j], ssd_conv_b[j], np_, sb, sl_len)
            dt_p, dt_s = dt[:np_], dt[np_:]
            ys = []
            for rev, h0 in ((False, state_ssd_fwd), (True, state_ssd_bwd)):
                yp, hfin = _ssd_scan(xc_p, dt_p, ssd_dt_bias[j], ssd_a_log[j], pb, pl_len, rev,
                                     None, j, True)
                (ysm,) = _ssd_scan(xc_s, dt_s, ssd_dt_bias[j], ssd_a_log[j], sb, sl_len, rev,
                                   h0, j, False)
                ys.append(jnp.concatenate([yp, ysm], axis=0))
                (new_ssd_b if rev else new_ssd_f).append(
                    hfin.reshape(pb, SSD_HEADS, SSD_HEADDIM, SSD_STATE))
            xc = jnp.concatenate([xc_p, xc_s], axis=0)
            d_sum = jnp.repeat(ssd_d[j][0] + ssd_d[j][1], SSD_HEADDIM).reshape(1, SSD_INNER)
            o = _ssd_gate(ys[0], ys[1], xc, z, d_sum, ssd_g_norm[j])
            x = _matmul(o, ssd_w_out, (j,), n_off=0, n_out=D_MODEL, tm=tm, tn=512, tk=2048,
                        out_dtype=F32, epilogue="resid", resid=x, mods=mods,
                        gate_index=gate_index(i, 2), name="ssd_out")

        f = i // 2
        if i % 2 == 0:
            h = _norm_mod(st, x, g_ffn[i], mods, i, 3)
            act = _matmul(h, ffn_w_gu, (f,), n_off=0, n_out=FFN_DIM, tm=tm, tn=512, tk=D_MODEL,
                          out_dtype=BF16, epilogue="swiglu", up_off=FFN_DIM, name="ffn_up")
            x = _matmul(act, ffn_w_down, (f,), n_off=0, n_out=D_MODEL, tm=tm, tn=512, tk=2816,
                        out_dtype=F32, epilogue="resid", resid=x, mods=mods,
                        gate_index=gate_index(i, 5), name="ffn_down")
        else:
            h, logits = _norm_mod(st, x, g_ffn[i], mods, i, 3, w_router=moe_w_router[f])
            y0, y1 = _moe(st, h, logits[:, :N_EXPERTS], moe_w_gu, moe_w_down, f)
            x = _combine(st, x, y0, y1, mods, i)

    y = _final_norm(st, x, g_final)
    y_prompt = y[:np_].reshape(pb, pl_len, D_MODEL)
    y_sample = y[np_:].reshape(sb, sl_len, D_MODEL)
    return (y_prompt, y_sample,
            jnp.stack(new_attn_k, axis=1), jnp.stack(new_attn_v, axis=1),
            jnp.stack(new_diff_k, axis=1), jnp.stack(new_diff_v, axis=1),
            jnp.stack(new_ssd_f, axis=1), jnp.stack(new_ssd_b, axis=1))
```

```python
import functools
import math

import jax
import jax.numpy as jnp
from jax import lax
from jax.experimental import pallas as pl
from jax.experimental.pallas import tpu as pltpu

F32 = jnp.float32
BF16 = jnp.bfloat16

D_MODEL = 2048
DEPTH = 4
GRID_W = 64
ROPE_THETA = 10000.0
EPS = 1e-6
HD = 128
ATTN_HEADS = 16
ATTN_KV_HEADS = 4
DIFF_HEADS = 8
SSD_INNER = 2 * D_MODEL
SSD_HEADDIM = 64
SSD_HEADS = SSD_INNER // SSD_HEADDIM
SSD_GROUPS = 8
SSD_STATE = 128
SSD_CONV = 5
SSD_CHUNK = 128
SSD_GN = SSD_GROUPS * SSD_STATE
SSD_CONV_DIM = SSD_INNER + 2 * SSD_GN
FFN_DIM = 5632
N_EXPERTS = 8
TOP_K = 2
EXPERT_DIM = 7168

MOD_ROWS = 16
CTX_ROW = 8
MOE_BLOCK = 1024
PREP_ROWS = 256
VMEM_LIMIT = 56 * 1024 * 1024


def _cparams(sem):
    return pltpu.CompilerParams(dimension_semantics=sem, vmem_limit_bytes=VMEM_LIMIT)


def _silu(x):
    return x / (1.0 + jnp.exp(-x))


def _split3(x):
    hi = x.astype(BF16)
    r = x - hi.astype(F32)
    mid = r.astype(BF16)
    lo = (r - mid.astype(F32)).astype(BF16)
    return hi, mid, lo


def _dot3(a3, b):
    out = jnp.dot(a3[0], b, preferred_element_type=F32)
    out = out + jnp.dot(a3[1], b, preferred_element_type=F32)
    return out + jnp.dot(a3[2], b, preferred_element_type=F32)


class _Stream:
    def __init__(self, n_prompt_batch, prompt_len, n_sample_batch, sample_len):
        self.pb, self.pl_, self.sb, self.sl = n_prompt_batch, prompt_len, n_sample_batch, sample_len
        self.n_prompt = n_prompt_batch * prompt_len
        self.n_sample = n_sample_batch * sample_len
        self.n = self.n_prompt + self.n_sample

    def mod_row(self, i, tm):
        pt = self.n_prompt // tm
        per = self.sl // tm
        return jnp.where(i < pt, CTX_ROW, (i - pt) // per)

    def tile(self, want):
        t = want
        while self.n_prompt % t or self.sl % t:
            t //= 2
        return t


def _mod_index(layer, which, row):
    return (layer * MOD_ROWS + row) * 6 + which


def _norm_mod_kernel(x_ref, g_ref, shift_ref, scale_ref, *rest, router):
    x = x_ref[...]
    y = x * lax.rsqrt(jnp.mean(x * x, axis=-1, keepdims=True) + EPS) * g_ref[...]
    h = y * (1.0 + scale_ref[...]) + shift_ref[...]
    if router:
        wr_ref, o_ref, lg_ref = rest
        w = wr_ref[...]
        whi = w.astype(BF16)
        wlo = (w - whi.astype(F32)).astype(BF16)
        h3 = _split3(h)
        lg = _dot3(h3, whi)
        lg = lg + jnp.dot(h3[0], wlo, preferred_element_type=F32)
        lg = lg + jnp.dot(h3[1], wlo, preferred_element_type=F32)
        lg_ref[...] = lg
    else:
        (o_ref,) = rest
    o_ref[...] = h.astype(o_ref.dtype)


def _norm_mod(st, x, g, mods, layer, which_shift, w_router=None):
    tm = st.tile(512)
    nt = st.n // tm
    g2 = g.reshape(1, D_MODEL)
    in_specs = [
        pl.BlockSpec((tm, D_MODEL), lambda i: (i, 0)),
        pl.BlockSpec((1, D_MODEL), lambda i: (0, 0)),
        pl.BlockSpec((None, 1, D_MODEL),
                     lambda i: (_mod_index(layer, which_shift, st.mod_row(i, tm)), 0, 0)),
        pl.BlockSpec((None, 1, D_MODEL),
                     lambda i: (_mod_index(layer, which_shift + 1, st.mod_row(i, tm)), 0, 0)),
    ]
    args = [x, g2, mods, mods]
    out_shape = [jax.ShapeDtypeStruct((st.n, D_MODEL), BF16)]
    out_specs = [pl.BlockSpec((tm, D_MODEL), lambda i: (i, 0))]
    if w_router is not None:
        wr = jnp.zeros((D_MODEL, 128), F32).at[:, :N_EXPERTS].set(w_router)
        in_specs.append(pl.BlockSpec((D_MODEL, 128), lambda i: (0, 0)))
        args.append(wr)
        out_shape.append(jax.ShapeDtypeStruct((st.n, 128), F32))
        out_specs.append(pl.BlockSpec((tm, 128), lambda i: (i, 0)))
    res = pl.pallas_call(
        functools.partial(_norm_mod_kernel, router=w_router is not None),
        grid=(nt,), in_specs=in_specs, out_specs=out_specs, out_shape=out_shape,
        compiler_params=_cparams(("parallel",)), name="norm_mod",
    )(*args)
    return res if w_router is not None else res[0]


def _final_norm_kernel(x_ref, g_ref, o_ref):
    x = x_ref[...]
    o_ref[...] = x * lax.rsqrt(jnp.mean(x * x, axis=-1, keepdims=True) + EPS) * g_ref[...]


def _final_norm(st, x, g):
    tm = st.tile(512)
    return pl.pallas_call(
        _final_norm_kernel, grid=(st.n // tm,),
        in_specs=[pl.BlockSpec((tm, D_MODEL), lambda i: (i, 0)),
                  pl.BlockSpec((1, D_MODEL), lambda i: (0, 0))],
        out_specs=pl.BlockSpec((tm, D_MODEL), lambda i: (i, 0)),
        out_shape=jax.ShapeDtypeStruct((st.n, D_MODEL), F32),
        compiler_params=_cparams(("parallel",)), name="final_norm",
    )(x, g.reshape(1, D_MODEL))


def _mm_kernel(*refs, nk, epilogue):
    a_ref, w_ref = refs[0], refs[1]
    pos = 2
    w2_ref = x_ref = gate_ref = None
    if epilogue == "swiglu":
        w2_ref = refs[pos]; pos += 1
    if epilogue == "resid":
        x_ref, gate_ref = refs[pos], refs[pos + 1]; pos += 2
    o_ref = refs[pos]; pos += 1
    acc_refs = refs[pos:]
    k = pl.program_id(2)
    a = a_ref[...]
    parts = [jnp.dot(a, w_ref[...].astype(BF16), preferred_element_type=F32)]
    if w2_ref is not None:
        parts.append(jnp.dot(a, w2_ref[...].astype(BF16), preferred_element_type=F32))

    def finish(vals):
        if epilogue == "swiglu":
            o_ref[...] = (_silu(vals[0]) * vals[1]).astype(o_ref.dtype)
        elif epilogue == "resid":
            o_ref[...] = (x_ref[...] + gate_ref[...] * vals[0]).astype(o_ref.dtype)
        else:
            o_ref[...] = vals[0].astype(o_ref.dtype)

    if nk == 1:
        finish(parts)
        return

    @pl.when(k == 0)
    def _():
        for acc, p in zip(acc_refs, parts):
            acc[...] = p

    @pl.when(jnp.logical_and(k > 0, k < nk - 1))
    def _():
        for acc, p in zip(acc_refs, parts):
            acc[...] += p

    @pl.when(k == nk - 1)
    def _():
        finish([acc[...] + p for acc, p in zip(acc_refs, parts)])


def _matmul(a, w, widx, *, n_off, n_out, tm, tn, tk, out_dtype, epilogue="plain",
            up_off=None, resid=None, mods=None, gate_index=None, name="mm"):
    m, kdim = a.shape
    nk = kdim // tk
    assert m % tm == 0 and kdim % tk == 0 and n_out % tn == 0 and n_off % tn == 0
    nb = n_off // tn
    lead = (None,) * len(widx)
    in_specs = [
        pl.BlockSpec((tm, tk), lambda i, j, k: (i, k)),
        pl.BlockSpec(lead + (tk, tn), lambda i, j, k: widx + (k, j + nb)),
    ]
    args = [a, w]
    if epilogue == "swiglu":
        ub = up_off // tn
        in_specs.append(pl.BlockSpec(lead + (tk, tn), lambda i, j, k: widx + (k, j + ub)))
        args.append(w)
    if epilogue == "resid":
        in_specs.append(pl.BlockSpec((tm, tn), lambda i, j, k: (i, j)))
        in_specs.append(pl.BlockSpec((None, 1, tn), lambda i, j, k: (gate_index(i), 0, j)))
        args += [resid, mods]
    n_acc = 0 if nk == 1 else (2 if epilogue == "swiglu" else 1)
    return pl.pallas_call(
        functools.partial(_mm_kernel, nk=nk, epilogue=epilogue),
        grid=(m // tm, n_out // tn, nk),
        in_specs=in_specs,
        out_specs=pl.BlockSpec((tm, tn), lambda i, j, k: (i, j)),
        out_shape=jax.ShapeDtypeStruct((m, n_out), out_dtype),
        scratch_shapes=[pltpu.VMEM((tm, tn), F32)] * n_acc,
        compiler_params=_cparams(("parallel", "parallel", "arbitrary")), name=name,
    )(*args)


def _rope_tables(length, ident_rows):
    nf = HD // 4
    inv = ROPE_THETA ** (-jnp.arange(nf, dtype=F32) / nf)
    rows = length // GRID_W
    row = jnp.repeat(jnp.arange(rows, dtype=F32), GRID_W)
    col = jnp.tile(jnp.arange(GRID_W, dtype=F32), rows)
    ar = row[:, None] * inv
    ac = col[:, None] * inv
    ang = jnp.concatenate([ar, ar, ac, ac], axis=-1)
    cos, sin = jnp.cos(ang), jnp.sin(ang)
    lane = jnp.arange(HD) % (HD // 2)
    second = lane >= HD // 4
    sin_a = jnp.where(second, sin, 0.0)
    sin_b = jnp.where(second, 0.0, -sin)
    ones = jnp.ones((ident_rows, HD), F32)
    zeros = jnp.zeros((ident_rows, HD), F32)
    return (jnp.concatenate([cos, ones]), jnp.concatenate([sin_a, zeros]),
            jnp.concatenate([sin_b, zeros]))


def _table_block(st, i):
    pt = st.n_prompt // PREP_ROWS
    per = st.sl // PREP_ROWS
    return jnp.where(i < pt, per, (i - pt) % per)


def _rope(y, cos, sin_a, sin_b):
    return y * cos + pltpu.roll(y, HD // 4, 1) * sin_a + pltpu.roll(y, HD - HD // 4, 1) * sin_b


def _gqa_prep_kernel(q_ref, k_ref, gq_ref, gk_ref, cos_ref, sa_ref, sb_ref, qo_ref, ko_ref):
    cos, sa, sb = cos_ref[...], sa_ref[...], sb_ref[...]
    scale = 1.0 / math.sqrt(HD)

    def norm(x, g):
        return x * lax.rsqrt(jnp.mean(x * x, axis=-1, keepdims=True) + EPS) * g

    for h in range(ATTN_HEADS):
        y = _rope(norm(q_ref[:, h * HD:(h + 1) * HD], gq_ref[...]), cos, sa, sb)
        qo_ref[:, h * HD:(h + 1) * HD] = (y * scale).astype(qo_ref.dtype)
    for h in range(ATTN_KV_HEADS):
        ko_ref[:, h * HD:(h + 1) * HD] = _rope(norm(k_ref[:, h * HD:(h + 1) * HD], gk_ref[...]),
                                               cos, sa, sb)


def _gqa_prep(st, qkv, g_q, g_k, tables):
    tm = PREP_ROWS
    qw = ATTN_HEADS * HD
    kw = ATTN_KV_HEADS * HD
    tab = pl.BlockSpec((tm, HD), lambda i: (_table_block(st, i), 0))
    return pl.pallas_call(
        _gqa_prep_kernel, grid=(st.n // tm,),
        in_specs=[pl.BlockSpec((tm, qw), lambda i: (i, 0)),
                  pl.BlockSpec((tm, kw), lambda i: (i, qw // kw)),
                  pl.BlockSpec((1, HD), lambda i: (0, 0)),
                  pl.BlockSpec((1, HD), lambda i: (0, 0)), tab, tab, tab],
        out_specs=[pl.BlockSpec((tm, qw), lambda i: (i, 0)),
                   pl.BlockSpec((tm, kw), lambda i: (i, 0))],
        out_shape=[jax.ShapeDtypeStruct((st.n, qw), BF16),
                   jax.ShapeDtypeStruct((st.n, kw), F32)],
        compiler_params=_cparams(("parallel",)), name="gqa_prep",
    )(qkv, qkv, g_q.reshape(1, HD), g_k.reshape(1, HD), *tables)


def _softmax_parts(q, ks):
    ss = [lax.dot_general(q, k, (((1,), (1,)), ((), ())), preferred_element_type=F32) for k in ks]
    m = ss[0].max(axis=-1, keepdims=True)
    for s in ss[1:]:
        m = jnp.maximum(m, s.max(axis=-1, keepdims=True))
    ps = [jnp.exp(s - m) for s in ss]
    l = ps[0].sum(axis=-1, keepdims=True)
    for p in ps[1:]:
        l = l + p.sum(axis=-1, keepdims=True)
    return ps, l


def _gqa_attn_kernel(q_ref, k_ref, v_ref, *rest, cache, aliased):
    if aliased:
        rest = rest[:-2] + rest[-1:]
    if cache:
        ck_ref, cv_ref, o_ref = rest
        ks = [ck_ref[...].astype(BF16), k_ref[...].astype(BF16)]
        vs = [cv_ref[...].astype(BF16), v_ref[...].astype(BF16)]
    else:
        (o_ref,) = rest
        ks = [k_ref[...].astype(BF16)]
        vs = [v_ref[...].astype(BF16)]
    grp = ATTN_HEADS // ATTN_KV_HEADS
    for h in range(grp):
        q = q_ref[:, h * HD:(h + 1) * HD]
        ps, l = _softmax_parts(q, ks)
        o = jnp.dot(ps[0].astype(BF16), vs[0], preferred_element_type=F32)
        for p, v in zip(ps[1:], vs[1:]):
            o = o + jnp.dot(p.astype(BF16), v, preferred_element_type=F32)
        o_ref[:, h * HD:(h + 1) * HD] = (o / l).astype(o_ref.dtype)


def _gqa_attn(st, q, k, qkv, row0, nb, seq, cache_k, cache_v, layer_j, prev):
    tq = min(512, seq)
    nq = seq // tq
    grp_w = (ATTN_HEADS // ATTN_KV_HEADS) * HD
    v_col0 = (ATTN_HEADS + ATTN_KV_HEADS)
    b0 = row0 // seq
    q0 = row0 // tq
    in_specs = [
        pl.BlockSpec((tq, grp_w), lambda b, g, i: (q0 + b * nq + i, g)),
        pl.BlockSpec((seq, HD), lambda b, g, i: (b + b0, g)),
        pl.BlockSpec((seq, HD), lambda b, g, i: (b + b0, v_col0 + g)),
    ]
    args = [q, k, qkv]
    cache = cache_k is not None
    if cache:
        past = cache_k.shape[2]
        ck = cache_k.reshape(cache_k.shape[0], cache_k.shape[1], past, ATTN_KV_HEADS * HD)
        cv = cache_v.reshape(ck.shape)
        in_specs += [pl.BlockSpec((None, None, past, HD), lambda b, g, i: (b, layer_j, 0, g))] * 2
        args += [ck, cv]
    aliases = {}
    if prev is not None:
        in_specs.append(pl.BlockSpec(memory_space=pl.ANY))
        args.append(prev)
        aliases = {len(args) - 1: 0}
    return pl.pallas_call(
        functools.partial(_gqa_attn_kernel, cache=cache, aliased=prev is not None),
        grid=(nb, ATTN_KV_HEADS, nq), in_specs=in_specs,
        out_specs=pl.BlockSpec((tq, grp_w), lambda b, g, i: (q0 + b * nq + i, g)),
        out_shape=jax.ShapeDtypeStruct((st.n, ATTN_HEADS * HD), BF16),
        input_output_aliases=aliases,
        compiler_params=_cparams(("parallel", "parallel", "parallel")), name="gqa_attn",
    )(*args)


def _diff_prep_kernel(q_ref, k_ref, cos_ref, sa_ref, sb_ref, qo_ref, ko_ref):
    cos, sa, sb = cos_ref[...], sa_ref[...], sb_ref[...]
    scale = 1.0 / math.sqrt(HD)
    for h in range(2 * DIFF_HEADS):
        y = _rope(q_ref[:, h * HD:(h + 1) * HD], cos, sa, sb)
        qo_ref[:, h * HD:(h + 1) * HD] = (y * scale).astype(qo_ref.dtype)
    for h in range(2 * DIFF_HEADS):
        y = _rope(k_ref[:, h * HD:(h + 1) * HD], cos, sa, sb)
        ko_ref[:, h * HD:(h + 1) * HD] = y.astype(ko_ref.dtype)


def _diff_prep(st, qkv, tables):
    tm = PREP_ROWS
    w = 2 * DIFF_HEADS * HD
    tab = pl.BlockSpec((tm, HD), lambda i: (_table_block(st, i), 0))
    return pl.pallas_call(
        _diff_prep_kernel, grid=(st.n // tm,),
        in_specs=[pl.BlockSpec((tm, w), lambda i: (i, 0)),
                  pl.BlockSpec((tm, w), lambda i: (i, 1)), tab, tab, tab],
        out_specs=[pl.BlockSpec((tm, w), lambda i: (i, 0))] * 2,
        out_shape=[jax.ShapeDtypeStruct((st.n, w), BF16)] * 2,
        compiler_params=_cparams(("parallel",)), name="diff_prep",
    )(qkv, qkv, *tables)


def _diff_attn_kernel(lam_ref, q_ref, k_ref, v_ref, g_ref, *rest, cache, out_scale, aliased):
    if aliased:
        rest = rest[:-2] + rest[-1:]
    if cache:
        ck_ref, cv_ref, o_ref = rest
        vs = [cv_ref[...].astype(BF16), v_ref[...].astype(BF16)]
    else:
        (o_ref,) = rest
        vs = [v_ref[...].astype(BF16)]
    lam = lam_ref[0]
    outs = []
    for m in range(2):
        q = q_ref[:, m * HD:(m + 1) * HD]
        ks = [k_ref[:, m * HD:(m + 1) * HD]]
        if cache:
            ks = [ck_ref[:, m * HD:(m + 1) * HD].astype(BF16)] + ks
        ps, l = _softmax_parts(q, ks)
        o = jnp.dot(ps[0].astype(BF16), vs[0], preferred_element_type=F32)
        for p, v in zip(ps[1:], vs[1:]):
            o = o + jnp.dot(p.astype(BF16), v, preferred_element_type=F32)
        outs.append(o / l)
    o = outs[0] - lam * outs[1]
    o = o * lax.rsqrt(jnp.mean(o * o, axis=-1, keepdims=True) + EPS) * g_ref[...]
    o_ref[...] = (o * out_scale).astype(o_ref.dtype)


def _diff_attn(st, lam, q, k, qkv, g_sub, lam_init, row0, nb, seq, cache_k, cache_v, layer_j, prev):
    tq = min(512, seq)
    nq = seq // tq
    hw = 2 * HD
    b0 = row0 // seq
    q0 = row0 // tq
    v_col0 = 2 * (2 * DIFF_HEADS * HD) // hw
    in_specs = [
        pl.BlockSpec(memory_space=pltpu.SMEM),
        pl.BlockSpec((tq, hw), lambda b, h, i: (q0 + b * nq + i, h)),
        pl.BlockSpec((seq, hw), lambda b, h, i: (b + b0, h)),
        pl.BlockSpec((seq, hw), lambda b, h, i: (b + b0, v_col0 + h)),
        pl.BlockSpec((1, hw), lambda b, h, i: (0, 0)),
    ]
    args = [lam, q, k, qkv, g_sub.reshape(1, hw)]
    cache = cache_k is not None
    if cache:
        past = cache_k.shape[2]
        ck = cache_k.reshape(cache_k.shape[0], cache_k.shape[1], past, 2 * DIFF_HEADS * HD)
        cv = cache_v.reshape(ck.shape)
        in_specs += [pl.BlockSpec((None, None, past, hw), lambda b, h, i: (b, layer_j, 0, h))] * 2
        args += [ck, cv]
    aliases = {}
    if prev is not None:
        in_specs.append(pl.BlockSpec(memory_space=pl.ANY))
        args.append(prev)
        aliases = {len(args) - 1: 0}
    return pl.pallas_call(
        functools.partial(_diff_attn_kernel, cache=cache, out_scale=1.0 - lam_init,
                          aliased=prev is not None),
        grid=(nb, DIFF_HEADS, nq), in_specs=in_specs,
        out_specs=pl.BlockSpec((tq, hw), lambda b, h, i: (q0 + b * nq + i, h)),
        out_shape=jax.ShapeDtypeStruct((st.n, DIFF_HEADS * hw), BF16),
        input_output_aliases=aliases,
        compiler_params=_cparams(("parallel", "parallel", "parallel")), name="diff_attn",
    )(*args)


def _conv_kernel(x_ref, w_ref, b_ref, *rest):
    o_ref, pad_ref = rest[-2:]
    seq = x_ref.shape[0]
    half = SSD_CONV // 2
    zeros = jnp.zeros((8, x_ref.shape[1]), F32)
    pad_ref[0:8, :] = zeros
    pad_ref[seq + 8:seq + 16, :] = zeros
    pad_ref[8:seq + 8, :] = x_ref[...]
    acc = jnp.broadcast_to(b_ref[...], x_ref.shape)
    for k in range(SSD_CONV):
        acc = acc + w_ref[k:k + 1, :] * pad_ref[8 - half + k:8 - half + k + seq, :]
    o_ref[...] = _silu(acc)


def _ssd_conv(st, xbc, conv_w, conv_b, row0, nb, seq, prev):
    tc = 512
    b0 = row0 // seq
    in_specs = [pl.BlockSpec((seq, tc), lambda b, c: (b + b0, c)),
                pl.BlockSpec((SSD_CONV, tc), lambda b, c: (0, c)),
                pl.BlockSpec((1, tc), lambda b, c: (0, c))]
    args = [xbc, conv_w, conv_b.reshape(1, SSD_CONV_DIM)]
    aliases = {}
    if prev is not None:
        in_specs.append(pl.BlockSpec(memory_space=pl.ANY))
        args.append(prev)
        aliases = {len(args) - 1: 0}
    return pl.pallas_call(
        _conv_kernel, grid=(nb, SSD_CONV_DIM // tc),
        in_specs=in_specs,
        out_specs=pl.BlockSpec((seq, tc), lambda b, c: (b + b0, c)),
        out_shape=jax.ShapeDtypeStruct((st.n, SSD_CONV_DIM), F32),
        scratch_shapes=[pltpu.VMEM((seq + 16, tc), F32)],
        input_output_aliases=aliases,
        compiler_params=_cparams(("parallel", "parallel")), name="ssd_conv",
    )(*args)


def _ssd_scan_kernel(x_ref, b_ref, c_ref, dt_ref, bias_ref, alog_ref, *rest,
                     rev, nc, has_init, want_final, aliased):
    pos = 0
    h0_ref = None
    if has_init:
        h0_ref = rest[pos]; pos += 1
    if aliased:
        pos += 1
    y_ref = rest[pos]; pos += 1
    hf_ref = None
    if want_final:
        hf_ref = rest[pos]; pos += 1
    state_ref, acol_ref, arow_ref, dtrow_ref, wrow_ref, dec_ref = rest[pos:]
    c = pl.program_id(1)
    q = SSD_CHUNK
    lane0 = SSD_HEADS if rev else 0
    r_heads = SSD_HEADS // SSD_GROUPS
    gw = r_heads * SSD_HEADDIM

    ii = lax.broadcasted_iota(jnp.int32, (q, q), 0)
    jj = lax.broadcasted_iota(jnp.int32, (q, q), 1)
    causal = (jj >= ii) if rev else (jj <= ii)

    @pl.when(c == 0)
    def _():
        if has_init:
            for blk in range(SSD_INNER // 128):
                state_ref[:, blk * 128:(blk + 1) * 128] = h0_ref[blk * 128:(blk + 1) * 128, :].T
        else:
            state_ref[...] = jnp.zeros(state_ref.shape, F32)

    xdt = dt_ref[...] + bias_ref[...]
    dt = jnp.maximum(xdt, 0.0) + jnp.log(1.0 + jnp.exp(-jnp.abs(xdt)))
    la = dt * (-jnp.exp(alog_ref[...]))
    tri = causal.astype(BF16)
    tri_t = ((ii >= jj) if rev else (ii <= jj)).astype(BF16)
    la3 = _split3(la)
    acol_ref[...] = (jnp.dot(tri, la3[0], preferred_element_type=F32)
                     + jnp.dot(tri, la3[1], preferred_element_type=F32)
                     + jnp.dot(tri, la3[2], preferred_element_type=F32))
    arow = _dot3(_split3(la.T), tri_t)
    end = 0 if rev else q - 1
    a_end = jnp.broadcast_to(arow[:, end:end + 1], arow.shape)
    dt_t = dt.T
    arow_ref[...] = arow
    dtrow_ref[...] = dt_t
    wrow_ref[...] = jnp.exp(a_end - arow) * dt_t
    dec_ref[...] = jnp.exp(a_end)

    lane_lo = lax.broadcasted_iota(jnp.int32, (1, 128), 1) < SSD_HEADDIM
    zeros_blk = jnp.zeros((SSD_STATE, q), BF16)

    for g in range(SSD_GROUPS):
        bmat = b_ref[:, g * SSD_STATE:(g + 1) * SSD_STATE]
        cmat = c_ref[:, g * SSD_STATE:(g + 1) * SSD_STATE]
        cb = lax.dot_general(cmat.astype(BF16), bmat.astype(BF16), (((1,), (1,)), ((), ())),
                             preferred_element_type=F32)
        b_t = bmat.T
        for pair in range(r_heads // 2):
            col = g * gw + pair * 128
            x_pair = x_ref[:, col:col + 128].astype(BF16)
            st_old = state_ref[:, col:col + 128]
            rhs = jnp.concatenate([x_pair, st_old.astype(BF16)], axis=0)
            res = []
            for sub in range(2):
                hd = lane0 + g * r_heads + 2 * pair + sub
                a_i = jnp.broadcast_to(acol_ref[:, hd:hd + 1], (q, q))
                seg = a_i - arow_ref[hd:hd + 1, :]
                decay = jnp.exp(jnp.where(causal, seg, -jnp.inf))
                m_intra = (cb * decay * dtrow_ref[hd:hd + 1, :]).astype(BF16)
                c_scaled = (cmat * jnp.exp(a_i)).astype(BF16)
                bw_t = (b_t * wrow_ref[hd:hd + 1, :]).astype(BF16)
                lhs = jnp.concatenate(
                    [jnp.concatenate([m_intra, c_scaled], axis=1),
                     jnp.concatenate([bw_t, zeros_blk], axis=1)], axis=0)
                res.append(jnp.dot(lhs, rhs, preferred_element_type=F32))
            hd0 = lane0 + g * r_heads + 2 * pair
            y_ref[:, col:col + 128] = jnp.where(lane_lo, res[0][:q], res[1][:q])
            dec_pair = jnp.where(lane_lo, dec_ref[hd0:hd0 + 1, :], dec_ref[hd0 + 1:hd0 + 2, :])
            upd = jnp.where(lane_lo, res[0][q:], res[1][q:])
            state_ref[:, col:col + 128] = dec_pair * st_old + upd

    if want_final:
        @pl.when(c == nc - 1)
        def _():
            for blk in range(SSD_INNER // 128):
                hf_ref[blk * 128:(blk + 1) * 128, :] = state_ref[:, blk * 128:(blk + 1) * 128].T


def _ssd_scan(st, xbc, dt, dt_bias, a_log, row0, nb, seq, rev, h0, layer_j, want_final, prev):
    nc = seq // SSD_CHUNK
    c0 = row0 // SSD_CHUNK

    def chunk(b, c):
        return c0 + b * nc + (nc - 1 - c if rev else c)

    in_specs = [
        pl.BlockSpec((SSD_CHUNK, SSD_INNER), lambda b, c: (chunk(b, c), 0)),
        pl.BlockSpec((SSD_CHUNK, SSD_GN), lambda b, c: (chunk(b, c), SSD_INNER // SSD_GN)),
        pl.BlockSpec((SSD_CHUNK, SSD_GN), lambda b, c: (chunk(b, c), SSD_INNER // SSD_GN + 1)),
        pl.BlockSpec((SSD_CHUNK, 128), lambda b, c: (chunk(b, c), 0)),
        pl.BlockSpec((1, 128), lambda b, c: (0, 0)),
        pl.BlockSpec((1, 128), lambda b, c: (0, 0)),
    ]
    args = [xbc, xbc, xbc, dt, dt_bias.reshape(1, 128), a_log.reshape(1, 128)]
    if h0 is not None:
        h0r = h0.reshape(h0.shape[0], h0.shape[1], SSD_INNER, SSD_STATE)
        in_specs.append(pl.BlockSpec((None, None, SSD_INNER, SSD_STATE),
                                     lambda b, c: (b, layer_j, 0, 0)))
        args.append(h0r)
    aliases = {}
    if prev is not None:
        in_specs.append(pl.BlockSpec(memory_space=pl.ANY))
        args.append(prev)
        aliases = {len(args) - 1: 0}
    out_shape = [jax.ShapeDtypeStruct((st.n, SSD_INNER), F32)]
    out_specs = [pl.BlockSpec((SSD_CHUNK, SSD_INNER), lambda b, c: (chunk(b, c), 0))]
    if want_final:
        out_shape.append(jax.ShapeDtypeStruct((nb, SSD_INNER, SSD_STATE), F32))
        out_specs.append(pl.BlockSpec((None, SSD_INNER, SSD_STATE), lambda b, c: (b, 0, 0)))
    res = pl.pallas_call(
        functools.partial(_ssd_scan_kernel, rev=rev, nc=nc, has_init=h0 is not None,
                          want_final=want_final, aliased=prev is not None),
        grid=(nb, nc), in_specs=in_specs, out_specs=out_specs, out_shape=out_shape,
        scratch_shapes=[pltpu.VMEM((SSD_STATE, SSD_INNER), F32),
                        pltpu.VMEM((SSD_CHUNK, 128), F32)] + [pltpu.VMEM((128, SSD_CHUNK), F32)] * 4,
        input_output_aliases=aliases,
        compiler_params=_cparams(("parallel", "arbitrary")), name="ssd_scan",
    )(*args)
    return res


def _ssd_gate_kernel(yf_ref, yb_ref, x_ref, z_ref, d_ref, g_ref, o_ref):
    gw = SSD_INNER // SSD_GROUPS
    for grp in range(SSD_GROUPS):
        sl = slice(grp * gw, (grp + 1) * gw)
        y = yf_ref[:, sl] + yb_ref[:, sl] + d_ref[:, sl] * x_ref[:, sl]
        gated = y * _silu(z_ref[:, sl])
        normed = gated * lax.rsqrt(jnp.mean(gated * gated, axis=-1, keepdims=True) + EPS)
        o_ref[:, sl] = (normed * g_ref[:, sl]).astype(o_ref.dtype)


def _ssd_gate(yf, yb, xbc, z, d_sum, g_norm):
    n = yf.shape[0]
    tm = 256
    row = pl.BlockSpec((tm, SSD_INNER), lambda i: (i, 0))
    vec = pl.BlockSpec((1, SSD_INNER), lambda i: (0, 0))
    return pl.pallas_call(
        _ssd_gate_kernel, grid=(n // tm,),
        in_specs=[row, row, row, row, vec, vec], out_specs=row,
        out_shape=jax.ShapeDtypeStruct((n, SSD_INNER), BF16),
        compiler_params=_cparams(("parallel",)), name="ssd_gate",
    )(yf, yb, xbc, z, d_sum, g_norm.reshape(1, SSD_INNER))


def _moe_up_kernel(be_ref, nu_ref, a_ref, wg_ref, wu_ref, o_ref):
    m = pl.program_id(1)

    @pl.when(m < nu_ref[0])
    def _():
        a = a_ref[...]
        gate = jnp.dot(a, wg_ref[...].astype(BF16), preferred_element_type=F32)
        up = jnp.dot(a, wu_ref[...].astype(BF16), preferred_element_type=F32)
        o_ref[...] = (_silu(gate) * up).astype(o_ref.dtype)

    @pl.when(m >= nu_ref[0])
    def _():
        o_ref[...] = jnp.zeros(o_ref.shape, o_ref.dtype)


def _moe_down_kernel(be_ref, nu_ref, a_ref, w_ref, o_ref, acc_ref, *, nk):
    m = pl.program_id(1)
    k = pl.program_id(2)
    live = m < nu_ref[0]

    @pl.when(live)
    def _():
        p = jnp.dot(a_ref[...], w_ref[...].astype(BF16), preferred_element_type=F32)

        @pl.when(k == 0)
        def _():
            acc_ref[...] = p

        @pl.when(jnp.logical_and(k > 0, k < nk - 1))
        def _():
            acc_ref[...] += p

        @pl.when(k == nk - 1)
        def _():
            o_ref[...] = acc_ref[...] + p

    @pl.when(jnp.logical_and(jnp.logical_not(live), k == nk - 1))
    def _():
        o_ref[...] = jnp.zeros(o_ref.shape, o_ref.dtype)


def _moe(st, h, logits, w_gu, w_down, f):
    n = st.n
    tb = MOE_BLOCK
    n_assign = n * TOP_K
    n_blocks = n_assign // tb + N_EXPERTS
    slots = n_blocks * tb
    top_logit, top_e = lax.top_k(logits, TOP_K)
    gates = jax.nn.softmax(top_logit, axis=-1)
    flat_e = top_e.reshape(-1).astype(jnp.int32)
    onehot = (flat_e[:, None] == jnp.arange(N_EXPERTS, dtype=jnp.int32)[None, :]).astype(jnp.int32)
    rank = jnp.sum((jnp.cumsum(onehot, axis=0) - onehot) * onehot, axis=1)
    counts = jnp.sum(onehot, axis=0)
    padded = (counts + tb - 1) // tb * tb
    pad_end = jnp.cumsum(padded)
    pad_start = pad_end - padded
    dest = pad_start[flat_e] + rank
    slot_tok = jnp.zeros((slots,), jnp.int32).at[dest].set(jnp.arange(n_assign, dtype=jnp.int32) // TOP_K)
    block_start = jnp.arange(n_blocks, dtype=jnp.int32) * tb
    block_e = jnp.minimum(jnp.searchsorted(pad_end, block_start, side="right"),
                          N_EXPERTS - 1).astype(jnp.int32)
    n_used = (pad_end[-1] // tb).astype(jnp.int32).reshape(1)

    xs = h.at[slot_tok].get(mode="promise_in_bounds")

    tn = 512
    nf = EXPERT_DIM // tn

    def live_block(m, nu):
        return jnp.minimum(m, nu[0] - 1)

    act = pl.pallas_call(
        _moe_up_kernel,
        grid_spec=pltpu.PrefetchScalarGridSpec(
            num_scalar_prefetch=2, grid=(nf, n_blocks),
            in_specs=[
                pl.BlockSpec((tb, D_MODEL), lambda j, m, be, nu: (live_block(m, nu), 0)),
                pl.BlockSpec((None, None, D_MODEL, tn),
                             lambda j, m, be, nu: (f, be[live_block(m, nu)], 0, j)),
                pl.BlockSpec((None, None, D_MODEL, tn),
                             lambda j, m, be, nu: (f, be[live_block(m, nu)], 0, j + nf)),
            ],
            out_specs=pl.BlockSpec((tb, tn), lambda j, m, be, nu: (m, j)),
        ),
        out_shape=jax.ShapeDtypeStruct((slots, EXPERT_DIM), BF16),
        compiler_params=_cparams(("parallel", "arbitrary")), name="moe_up",
    )(block_e, n_used, xs, w_gu, w_gu)

    tk = 1792
    nk = EXPERT_DIM // tk
    tn2 = 1024
    yb = pl.pallas_call(
        functools.partial(_moe_down_kernel, nk=nk),
        grid_spec=pltpu.PrefetchScalarGridSpec(
            num_scalar_prefetch=2, grid=(D_MODEL // tn2, n_blocks, nk),
            in_specs=[
                pl.BlockSpec((tb, tk), lambda j, m, k, be, nu: (live_block(m, nu), k)),
                pl.BlockSpec((None, None, tk, tn2),
                             lambda j, m, k, be, nu: (f, be[live_block(m, nu)], k, j)),
            ],
            out_specs=pl.BlockSpec((tb, tn2), lambda j, m, k, be, nu: (m, j)),
            scratch_shapes=[pltpu.VMEM((tb, tn2), F32)],
        ),
        out_shape=jax.ShapeDtypeStruct((slots, D_MODEL), F32),
        compiler_params=_cparams(("parallel", "arbitrary", "arbitrary")), name="moe_down",
    )(block_e, n_used, act, w_down)

    dest2 = dest.reshape(n, TOP_K)
    y0 = yb.at[dest2[:, 0]].get(mode="promise_in_bounds")
    y1 = yb.at[dest2[:, 1]].get(mode="promise_in_bounds")
    return y0, y1, gates


def _combine_kernel(x_ref, y0_ref, y1_ref, g_ref, gate_ref, o_ref):
    y = g_ref[:, 0:1] * y0_ref[...] + g_ref[:, 1:2] * y1_ref[...]
    o_ref[...] = x_ref[...] + gate_ref[...] * y


def _combine(st, x, y0, y1, gates, mods, layer):
    tm = st.tile(512)
    row = pl.BlockSpec((tm, D_MODEL), lambda i: (i, 0))
    return pl.pallas_call(
        _combine_kernel, grid=(st.n // tm,),
        in_specs=[row, row, row, pl.BlockSpec((tm, TOP_K), lambda i: (i, 0)),
                  pl.BlockSpec((None, 1, D_MODEL),
                               lambda i: (_mod_index(layer, 5, st.mod_row(i, tm)), 0, 0))],
        out_specs=row, out_shape=jax.ShapeDtypeStruct((st.n, D_MODEL), F32),
        compiler_params=_cparams(("parallel",)), name="moe_combine",
    )(x, y0, y1, gates, mods)


def kernel(x_prompt, x_sample, cache_attn_k, cache_attn_v, cache_diff_k, cache_diff_v, state_ssd_fwd, state_ssd_bwd, c, c_ctx, w_ada, b_ada, g_mix, g_ffn, g_final, attn_w_in, attn_g_q, attn_g_k, attn_w_out, diff_w_in, diff_lambda, diff_g_sub, diff_w_out, ssd_w_in, ssd_conv_w, ssd_conv_b, ssd_dt_bias, ssd_a_log, ssd_d, ssd_g_norm, ssd_w_out, ffn_w_gu, ffn_w_down, moe_w_router, moe_w_gu, moe_w_down):
    pb, pl_len, _ = x_prompt.shape
    sb, sl_len, _ = x_sample.shape
    st = _Stream(pb, pl_len, sb, sl_len)
    np_, ns = st.n_prompt, st.n_sample
    tm = st.tile(1024)

    x = jnp.concatenate([x_prompt.reshape(np_, D_MODEL), x_sample.reshape(ns, D_MODEL)], axis=0)

    cond = jnp.zeros((MOD_ROWS, D_MODEL), F32).at[:sb].set(c).at[CTX_ROW].set(c_ctx)
    cond_act = jax.nn.silu(cond).astype(BF16)
    mods = []
    for i in range(DEPTH):
        m = _matmul(cond_act, w_ada, (i,), n_off=0, n_out=6 * D_MODEL, tm=MOD_ROWS, tn=1024,
                    tk=D_MODEL, out_dtype=F32, name="ada")
        mods.append(m + b_ada[i][None, :])
    mods = jnp.stack(mods).reshape(DEPTH * MOD_ROWS * 6, 1, D_MODEL)

    tables = _rope_tables(sl_len, PREP_ROWS)

    def gate_index(layer, which):
        return lambda i: _mod_index(layer, which, st.mod_row(i, tm))

    new_attn_k, new_attn_v, new_diff_k, new_diff_v, new_ssd_f, new_ssd_b = [], [], [], [], [], []
    for i in range(DEPTH):
        kind, j = i % 3, i // 3
        h = _norm_mod(st, x, g_mix[i], mods, i, 0)
        if kind == 0:
            qkv = _matmul(h, attn_w_in, (j,), n_off=0, n_out=attn_w_in.shape[-1], tm=tm, tn=1024,
                          tk=D_MODEL, out_dtype=F32, name="attn_in")
            q, k = _gqa_prep(st, qkv, attn_g_q[j], attn_g_k[j], tables)
            o = _gqa_attn(st, q, k, qkv, 0, pb, pl_len, None, None, j, None)
            o = _gqa_attn(st, q, k, qkv, np_, sb, sl_len, cache_attn_k, cache_attn_v, j, o)
            new_attn_k.append(k[:np_].reshape(pb, pl_len, ATTN_KV_HEADS, HD))
            new_attn_v.append(qkv[:np_, (ATTN_HEADS + ATTN_KV_HEADS) * HD:]
                              .reshape(pb, pl_len, ATTN_KV_HEADS, HD))
            x = _matmul(o, attn_w_out, (j,), n_off=0, n_out=D_MODEL, tm=tm, tn=512, tk=2048,
                        out_dtype=F32, epilogue="resid", resid=x, mods=mods,
                        gate_index=gate_index(i, 2), name="attn_out")
        elif kind == 1:
            lam_init = 0.8 - 0.6 * math.exp(-0.3 * i)
            lf = diff_lambda[j]
            lam = (jnp.exp(jnp.sum(lf[0] * lf[1])) - jnp.exp(jnp.sum(lf[2] * lf[3])) + lam_init).reshape(1)
            qkv = _matmul(h, diff_w_in, (j,), n_off=0, n_out=diff_w_in.shape[-1], tm=tm, tn=1024,
                          tk=D_MODEL, out_dtype=F32, name="diff_in")
            w = 2 * DIFF_HEADS * HD
            q, k = _diff_prep(st, qkv, tables)
            o = _diff_attn(st, lam, q, k, qkv, diff_g_sub[j], lam_init, 0, pb, pl_len, None, None, j, None)
            o = _diff_attn(st, lam, q, k, qkv, diff_g_sub[j], lam_init, np_, sb, sl_len,
                           cache_diff_k, cache_diff_v, j, o)
            new_diff_k.append(qkv[:np_, w:2 * w].reshape(pb, pl_len, 2 * DIFF_HEADS, HD))
            new_diff_v.append(qkv[:np_, 2 * w:].reshape(pb, pl_len, DIFF_HEADS, 2 * HD))
            x = _matmul(o, diff_w_out, (j,), n_off=0, n_out=D_MODEL, tm=tm, tn=512, tk=2048,
                        out_dtype=F32, epilogue="resid", resid=x, mods=mods,
                        gate_index=gate_index(i, 2), name="diff_out")
        else:
            z = _matmul(h, ssd_w_in, (j,), n_off=0, n_out=SSD_INNER, tm=tm, tn=1024, tk=D_MODEL,
                        out_dtype=F32, name="ssd_in_z")
            xbc = _matmul(h, ssd_w_in, (j,), n_off=SSD_INNER, n_out=SSD_CONV_DIM, tm=tm, tn=1024,
                          tk=D_MODEL, out_dtype=F32, name="ssd_in_xbc")
            dt = _matmul(h, ssd_w_in, (j,), n_off=SSD_INNER + SSD_CONV_DIM, n_out=2 * SSD_HEADS,
                         tm=tm, tn=128, tk=D_MODEL, out_dtype=F32, name="ssd_in_dt")
            xc = _ssd_conv(st, xbc, ssd_conv_w[j], ssd_conv_b[j], 0, pb, pl_len, None)
            xc = _ssd_conv(st, xbc, ssd_conv_w[j], ssd_conv_b[j], np_, sb, sl_len, xc)
            ys = []
            for rev, h0 in ((False, state_ssd_fwd), (True, state_ssd_bwd)):
                y, hfin = _ssd_scan(st, xc, dt, ssd_dt_bias[j], ssd_a_log[j], 0, pb, pl_len, rev,
                                    None, j, True, None)
                (y,) = _ssd_scan(st, xc, dt, ssd_dt_bias[j], ssd_a_log[j], np_, sb, sl_len, rev,
                                 h0, j, False, y)
                ys.append(y)
                (new_ssd_b if rev else new_ssd_f).append(
                    hfin.reshape(pb, SSD_HEADS, SSD_HEADDIM, SSD_STATE))
            d_sum = jnp.repeat(ssd_d[j][0] + ssd_d[j][1], SSD_HEADDIM).reshape(1, SSD_INNER)
            o = _ssd_gate(ys[0], ys[1], xc, z, d_sum, ssd_g_norm[j])
            x = _matmul(o, ssd_w_out, (j,), n_off=0, n_out=D_MODEL, tm=tm, tn=512, tk=2048,
                        out_dtype=F32, epilogue="resid", resid=x, mods=mods,
                        gate_index=gate_index(i, 2), name="ssd_out")

        f = i // 2
        if i % 2 == 0:
            h = _norm_mod(st, x, g_ffn[i], mods, i, 3)
            act = _matmul(h, ffn_w_gu, (f,), n_off=0, n_out=FFN_DIM, tm=tm, tn=512, tk=D_MODEL,
                          out_dtype=BF16, epilogue="swiglu", up_off=FFN_DIM, name="ffn_up")
            x = _matmul(act, ffn_w_down, (f,), n_off=0, n_out=D_MODEL, tm=tm, tn=512, tk=2816,
                        out_dtype=F32, epilogue="resid", resid=x, mods=mods,
                        gate_index=gate_index(i, 5), name="ffn_down")
        else:
            h, logits = _norm_mod(st, x, g_ffn[i], mods, i, 3, w_router=moe_w_router[f])
            y0, y1, gates = _moe(st, h, logits[:, :N_EXPERTS], moe_w_gu, moe_w_down, f)
            x = _combine(st, x, y0, y1, gates, mods, i)

    y = _final_norm(st, x, g_final)
    y_prompt = y[:np_].reshape(pb, pl_len, D_MODEL)
    y_sample = y[np_:].reshape(sb, sl_len, D_MODEL)
    return (y_prompt, y_sample,
            jnp.stack(new_attn_k, axis=1), jnp.stack(new_attn_v, axis=1),
            jnp.stack(new_diff_k, axis=1), jnp.stack(new_diff_v, axis=1),
            jnp.stack(new_ssd_f, axis=1), jnp.stack(new_ssd_b, axis=1))
```

```python
import functools
import math

import jax
import jax.numpy as jnp
from jax import lax
from jax.experimental import pallas as pl
from jax.experimental.pallas import tpu as pltpu

F32 = jnp.float32
BF16 = jnp.bfloat16

D_MODEL = 2048
DEPTH = 4
GRID_W = 64
ROPE_THETA = 10000.0
EPS = 1e-6
HD = 128
ATTN_HEADS = 16
ATTN_KV_HEADS = 4
DIFF_HEADS = 8
SSD_INNER = 2 * D_MODEL
SSD_HEADDIM = 64
SSD_HEADS = SSD_INNER // SSD_HEADDIM
SSD_GROUPS = 8
SSD_STATE = 128
SSD_CONV = 5
SSD_CHUNK = 128
SSD_GN = SSD_GROUPS * SSD_STATE
SSD_CONV_DIM = SSD_INNER + 2 * SSD_GN
FFN_DIM = 5632
N_EXPERTS = 8
TOP_K = 2
EXPERT_DIM = 7168

MOD_ROWS = 16
CTX_ROW = 8
MOE_BLOCK = 1024
PREP_ROWS = 256
VMEM_LIMIT = 56 * 1024 * 1024


def _cparams(sem):
    return pltpu.CompilerParams(dimension_semantics=sem, vmem_limit_bytes=VMEM_LIMIT)


def _silu(x):
    return x / (1.0 + jnp.exp(-x))


def _split3(x):
    hi = x.astype(BF16)
    r = x - hi.astype(F32)
    mid = r.astype(BF16)
    lo = (r - mid.astype(F32)).astype(BF16)
    return hi, mid, lo


def _dot3(a3, b):
    out = jnp.dot(a3[0], b, preferred_element_type=F32)
    out = out + jnp.dot(a3[1], b, preferred_element_type=F32)
    return out + jnp.dot(a3[2], b, preferred_element_type=F32)


class _Stream:
    def __init__(self, n_prompt_batch, prompt_len, n_sample_batch, sample_len):
        self.pb, self.pl_, self.sb, self.sl = n_prompt_batch, prompt_len, n_sample_batch, sample_len
        self.n_prompt = n_prompt_batch * prompt_len
        self.n_sample = n_sample_batch * sample_len
        self.n = self.n_prompt + self.n_sample

    def mod_row(self, i, tm):
        pt = self.n_prompt // tm
        per = self.sl // tm
        return jnp.where(i < pt, CTX_ROW, (i - pt) // per)

    def tile(self, want):
        t = want
        while self.n_prompt % t or self.sl % t:
            t //= 2
        return t


def _mod_index(layer, which, row):
    return (layer * MOD_ROWS + row) * 6 + which


def _norm_mod_kernel(x_ref, g_ref, shift_ref, scale_ref, *rest, router):
    x = x_ref[...]
    y = x * lax.rsqrt(jnp.mean(x * x, axis=-1, keepdims=True) + EPS) * g_ref[...]
    h = y * (1.0 + scale_ref[...]) + shift_ref[...]
    if router:
        wr_ref, o_ref, lg_ref = rest
        w = wr_ref[...]
        whi = w.astype(BF16)
        wlo = (w - whi.astype(F32)).astype(BF16)
        h3 = _split3(h)
        lg = _dot3(h3, whi)
        lg = lg + jnp.dot(h3[0], wlo, preferred_element_type=F32)
        lg = lg + jnp.dot(h3[1], wlo, preferred_element_type=F32)
        lg_ref[...] = lg
    else:
        (o_ref,) = rest
    o_ref[...] = h.astype(o_ref.dtype)


def _norm_mod(st, x, g, mods, layer, which_shift, w_router=None):
    tm = st.tile(512)
    nt = st.n // tm
    g2 = g.reshape(1, D_MODEL)
    in_specs = [
        pl.BlockSpec((tm, D_MODEL), lambda i: (i, 0)),
        pl.BlockSpec((1, D_MODEL), lambda i: (0, 0)),
        pl.BlockSpec((None, 1, D_MODEL),
                     lambda i: (_mod_index(layer, which_shift, st.mod_row(i, tm)), 0, 0)),
        pl.BlockSpec((None, 1, D_MODEL),
                     lambda i: (_mod_index(layer, which_shift + 1, st.mod_row(i, tm)), 0, 0)),
    ]
    args = [x, g2, mods, mods]
    out_shape = [jax.ShapeDtypeStruct((st.n, D_MODEL), BF16)]
    out_specs = [pl.BlockSpec((tm, D_MODEL), lambda i: (i, 0))]
    if w_router is not None:
        wr = jnp.zeros((D_MODEL, 128), F32).at[:, :N_EXPERTS].set(w_router)
        in_specs.append(pl.BlockSpec((D_MODEL, 128), lambda i: (0, 0)))
        args.append(wr)
        out_shape.append(jax.ShapeDtypeStruct((st.n, 128), F32))
        out_specs.append(pl.BlockSpec((tm, 128), lambda i: (i, 0)))
    res = pl.pallas_call(
        functools.partial(_norm_mod_kernel, router=w_router is not None),
        grid=(nt,), in_specs=in_specs, out_specs=out_specs, out_shape=out_shape,
        compiler_params=_cparams(("parallel",)), name="norm_mod",
    )(*args)
    return res if w_router is not None else res[0]


def _final_norm_kernel(x_ref, g_ref, o_ref):
    x = x_ref[...]
    o_ref[...] = x * lax.rsqrt(jnp.mean(x * x, axis=-1, keepdims=True) + EPS) * g_ref[...]


def _final_norm(x, g, row0, nrows):
    tm = 512
    r0 = row0 // tm
    return pl.pallas_call(
        _final_norm_kernel, grid=(nrows // tm,),
        in_specs=[pl.BlockSpec((tm, D_MODEL), lambda i: (i + r0, 0)),
                  pl.BlockSpec((1, D_MODEL), lambda i: (0, 0))],
        out_specs=pl.BlockSpec((tm, D_MODEL), lambda i: (i, 0)),
        out_shape=jax.ShapeDtypeStruct((nrows, D_MODEL), F32),
        compiler_params=_cparams(("parallel",)), name="final_norm",
    )(x, g.reshape(1, D_MODEL))


def _mm_kernel(*refs, nk, epilogue):
    a_ref, w_ref = refs[0], refs[1]
    pos = 2
    w2_ref = x_ref = gate_ref = None
    if epilogue == "swiglu":
        w2_ref = refs[pos]; pos += 1
    if epilogue == "resid":
        x_ref, gate_ref = refs[pos], refs[pos + 1]; pos += 2
    o_ref = refs[pos]; pos += 1
    acc_refs = refs[pos:]
    k = pl.program_id(2)
    a = a_ref[...]
    parts = [jnp.dot(a, w_ref[...].astype(BF16), preferred_element_type=F32)]
    if w2_ref is not None:
        parts.append(jnp.dot(a, w2_ref[...].astype(BF16), preferred_element_type=F32))

    def finish(vals):
        if epilogue == "swiglu":
            o_ref[...] = (_silu(vals[0]) * vals[1]).astype(o_ref.dtype)
        elif epilogue == "resid":
            o_ref[...] = (x_ref[...] + gate_ref[...] * vals[0]).astype(o_ref.dtype)
        else:
            o_ref[...] = vals[0].astype(o_ref.dtype)

    if nk == 1:
        finish(parts)
        return

    @pl.when(k == 0)
    def _():
        for acc, p in zip(acc_refs, parts):
            acc[...] = p

    @pl.when(jnp.logical_and(k > 0, k < nk - 1))
    def _():
        for acc, p in zip(acc_refs, parts):
            acc[...] += p

    @pl.when(k == nk - 1)
    def _():
        finish([acc[...] + p for acc, p in zip(acc_refs, parts)])


def _matmul(a, w, widx, *, n_off, n_out, tm, tn, tk, out_dtype, epilogue="plain",
            up_off=None, resid=None, mods=None, gate_index=None, name="mm"):
    m, kdim = a.shape
    nk = kdim // tk
    assert m % tm == 0 and kdim % tk == 0 and n_out % tn == 0 and n_off % tn == 0
    nb = n_off // tn
    lead = (None,) * len(widx)
    in_specs = [
        pl.BlockSpec((tm, tk), lambda i, j, k: (i, k)),
        pl.BlockSpec(lead + (tk, tn), lambda i, j, k: widx + (k, j + nb)),
    ]
    args = [a, w]
    if epilogue == "swiglu":
        ub = up_off // tn
        in_specs.append(pl.BlockSpec(lead + (tk, tn), lambda i, j, k: widx + (k, j + ub)))
        args.append(w)
    if epilogue == "resid":
        in_specs.append(pl.BlockSpec((tm, tn), lambda i, j, k: (i, j)))
        in_specs.append(pl.BlockSpec((None, 1, tn), lambda i, j, k: (gate_index(i), 0, j)))
        args += [resid, mods]
    n_acc = 0 if nk == 1 else (2 if epilogue == "swiglu" else 1)
    return pl.pallas_call(
        functools.partial(_mm_kernel, nk=nk, epilogue=epilogue),
        grid=(m // tm, n_out // tn, nk),
        in_specs=in_specs,
        out_specs=pl.BlockSpec((tm, tn), lambda i, j, k: (i, j)),
        out_shape=jax.ShapeDtypeStruct((m, n_out), out_dtype),
        scratch_shapes=[pltpu.VMEM((tm, tn), F32)] * n_acc,
        compiler_params=_cparams(("parallel", "parallel", "arbitrary")), name=name,
    )(*args)


def _rope_tables(length, ident_rows):
    nf = HD // 4
    inv = ROPE_THETA ** (-jnp.arange(nf, dtype=F32) / nf)
    rows = length // GRID_W
    row = jnp.repeat(jnp.arange(rows, dtype=F32), GRID_W)
    col = jnp.tile(jnp.arange(GRID_W, dtype=F32), rows)
    ar = row[:, None] * inv
    ac = col[:, None] * inv
    ang = jnp.concatenate([ar, ar, ac, ac], axis=-1)
    cos, sin = jnp.cos(ang), jnp.sin(ang)
    lane = jnp.arange(HD) % (HD // 2)
    sin_signed = jnp.where(lane >= HD // 4, sin, -sin)
    ones = jnp.ones((ident_rows, HD), F32)
    zeros = jnp.zeros((ident_rows, HD), F32)
    return jnp.concatenate([cos, ones]), jnp.concatenate([sin_signed, zeros])


def _table_block(st, i):
    pt = st.n_prompt // PREP_ROWS
    per = st.sl // PREP_ROWS
    return jnp.where(i < pt, per, (i - pt) % per)


def _rope(y, cos, sin_signed):
    partner = lax.broadcasted_iota(jnp.int32, y.shape, 1) ^ (HD // 4)
    return y * cos + jnp.take_along_axis(y, partner, axis=1) * sin_signed


def _gqa_prep_kernel(q_ref, k_ref, gq_ref, gk_ref, cos_ref, sin_ref, qo_ref, ko_ref):
    cos, sin = cos_ref[...], sin_ref[...]
    scale = 1.0 / math.sqrt(HD)

    def norm(x, g):
        return x * lax.rsqrt(jnp.mean(x * x, axis=-1, keepdims=True) + EPS) * g

    for h in range(ATTN_HEADS):
        y = _rope(norm(q_ref[:, h * HD:(h + 1) * HD], gq_ref[...]), cos, sin)
        qo_ref[:, h * HD:(h + 1) * HD] = (y * scale).astype(qo_ref.dtype)
    for h in range(ATTN_KV_HEADS):
        ko_ref[:, h * HD:(h + 1) * HD] = _rope(norm(k_ref[:, h * HD:(h + 1) * HD], gk_ref[...]),
                                               cos, sin)


def _gqa_prep(st, qkv, g_q, g_k, tables):
    tm = PREP_ROWS
    qw = ATTN_HEADS * HD
    kw = ATTN_KV_HEADS * HD
    tab = pl.BlockSpec((tm, HD), lambda i: (_table_block(st, i), 0))
    return pl.pallas_call(
        _gqa_prep_kernel, grid=(st.n // tm,),
        in_specs=[pl.BlockSpec((tm, qw), lambda i: (i, 0)),
                  pl.BlockSpec((tm, kw), lambda i: (i, qw // kw)),
                  pl.BlockSpec((1, HD), lambda i: (0, 0)),
                  pl.BlockSpec((1, HD), lambda i: (0, 0)), tab, tab],
        out_specs=[pl.BlockSpec((tm, qw), lambda i: (i, 0)),
                   pl.BlockSpec((tm, kw), lambda i: (i, 0))],
        out_shape=[jax.ShapeDtypeStruct((st.n, qw), BF16),
                   jax.ShapeDtypeStruct((st.n, kw), F32)],
        compiler_params=_cparams(("parallel",)), name="gqa_prep",
    )(qkv, qkv, g_q.reshape(1, HD), g_k.reshape(1, HD), *tables)


def _softmax_parts(q, ks):
    ss = [lax.dot_general(q, k, (((1,), (1,)), ((), ())), preferred_element_type=F32) for k in ks]
    m = ss[0].max(axis=-1, keepdims=True)
    for s in ss[1:]:
        m = jnp.maximum(m, s.max(axis=-1, keepdims=True))
    ps = [jnp.exp(s - m) for s in ss]
    l = ps[0].sum(axis=-1, keepdims=True)
    for p in ps[1:]:
        l = l + p.sum(axis=-1, keepdims=True)
    return ps, l


def _gqa_attn_kernel(q_ref, k_ref, v_ref, *rest, cache, aliased):
    if aliased:
        rest = rest[:-2] + rest[-1:]
    if cache:
        ck_ref, cv_ref, o_ref = rest
        ks = [ck_ref[...].astype(BF16), k_ref[...].astype(BF16)]
        vs = [cv_ref[...].astype(BF16), v_ref[...].astype(BF16)]
    else:
        (o_ref,) = rest
        ks = [k_ref[...].astype(BF16)]
        vs = [v_ref[...].astype(BF16)]
    grp = ATTN_HEADS // ATTN_KV_HEADS
    for h in range(grp):
        q = q_ref[:, h * HD:(h + 1) * HD]
        ps, l = _softmax_parts(q, ks)
        o = jnp.dot(ps[0].astype(BF16), vs[0], preferred_element_type=F32)
        for p, v in zip(ps[1:], vs[1:]):
            o = o + jnp.dot(p.astype(BF16), v, preferred_element_type=F32)
        o_ref[:, h * HD:(h + 1) * HD] = (o / l).astype(o_ref.dtype)


def _gqa_attn(st, q, k, qkv, row0, nb, seq, cache_k, cache_v, layer_j, prev):
    tq = min(512, seq)
    nq = seq // tq
    grp_w = (ATTN_HEADS // ATTN_KV_HEADS) * HD
    v_col0 = (ATTN_HEADS + ATTN_KV_HEADS)
    b0 = row0 // seq
    q0 = row0 // tq
    in_specs = [
        pl.BlockSpec((tq, grp_w), lambda b, g, i: (q0 + b * nq + i, g)),
        pl.BlockSpec((seq, HD), lambda b, g, i: (b + b0, g)),
        pl.BlockSpec((seq, HD), lambda b, g, i: (b + b0, v_col0 + g)),
    ]
    args = [q, k, qkv]
    cache = cache_k is not None
    if cache:
        past = cache_k.shape[2]
        ck = cache_k.reshape(cache_k.shape[0], cache_k.shape[1], past, ATTN_KV_HEADS * HD)
        cv = cache_v.reshape(ck.shape)
        in_specs += [pl.BlockSpec((None, None, past, HD), lambda b, g, i: (b, layer_j, 0, g))] * 2
        args += [ck, cv]
    aliases = {}
    if prev is not None:
        in_specs.append(pl.BlockSpec(memory_space=pl.ANY))
        args.append(prev)
        aliases = {len(args) - 1: 0}
    return pl.pallas_call(
        functools.partial(_gqa_attn_kernel, cache=cache, aliased=prev is not None),
        grid=(nb, ATTN_KV_HEADS, nq), in_specs=in_specs,
        out_specs=pl.BlockSpec((tq, grp_w), lambda b, g, i: (q0 + b * nq + i, g)),
        out_shape=jax.ShapeDtypeStruct((st.n, ATTN_HEADS * HD), BF16),
        input_output_aliases=aliases,
        compiler_params=_cparams(("parallel", "parallel", "parallel")), name="gqa_attn",
    )(*args)


def _diff_prep_kernel(q_ref, k_ref, cos_ref, sin_ref, qo_ref, ko_ref):
    cos, sin = cos_ref[...], sin_ref[...]
    scale = 1.0 / math.sqrt(HD)
    for h in range(2 * DIFF_HEADS):
        y = _rope(q_ref[:, h * HD:(h + 1) * HD], cos, sin)
        qo_ref[:, h * HD:(h + 1) * HD] = (y * scale).astype(qo_ref.dtype)
    for h in range(2 * DIFF_HEADS):
        y = _rope(k_ref[:, h * HD:(h + 1) * HD], cos, sin)
        ko_ref[:, h * HD:(h + 1) * HD] = y.astype(ko_ref.dtype)


def _diff_prep(st, qkv, tables):
    tm = PREP_ROWS
    w = 2 * DIFF_HEADS * HD
    tab = pl.BlockSpec((tm, HD), lambda i: (_table_block(st, i), 0))
    return pl.pallas_call(
        _diff_prep_kernel, grid=(st.n // tm,),
        in_specs=[pl.BlockSpec((tm, w), lambda i: (i, 0)),
                  pl.BlockSpec((tm, w), lambda i: (i, 1)), tab, tab],
        out_specs=[pl.BlockSpec((tm, w), lambda i: (i, 0))] * 2,
        out_shape=[jax.ShapeDtypeStruct((st.n, w), BF16)] * 2,
        compiler_params=_cparams(("parallel",)), name="diff_prep",
    )(qkv, qkv, *tables)


def _diff_attn_kernel(lam_ref, q_ref, k_ref, v_ref, g_ref, *rest, cache, out_scale, aliased):
    if aliased:
        rest = rest[:-2] + rest[-1:]
    if cache:
        ck_ref, cv_ref, o_ref = rest
        vs = [cv_ref[...].astype(BF16), v_ref[...].astype(BF16)]
    else:
        (o_ref,) = rest
        vs = [v_ref[...].astype(BF16)]
    lam = lam_ref[0]
    outs = []
    for m in range(2):
        q = q_ref[:, m * HD:(m + 1) * HD]
        ks = [k_ref[:, m * HD:(m + 1) * HD]]
        if cache:
            ks = [ck_ref[:, m * HD:(m + 1) * HD].astype(BF16)] + ks
        ps, l = _softmax_parts(q, ks)
        o = jnp.dot(ps[0].astype(BF16), vs[0], preferred_element_type=F32)
        for p, v in zip(ps[1:], vs[1:]):
            o = o + jnp.dot(p.astype(BF16), v, preferred_element_type=F32)
        outs.append(o / l)
    o = outs[0] - lam * outs[1]
    o = o * lax.rsqrt(jnp.mean(o * o, axis=-1, keepdims=True) + EPS) * g_ref[...]
    o_ref[...] = (o * out_scale).astype(o_ref.dtype)


def _diff_attn(st, lam, q, k, qkv, g_sub, lam_init, row0, nb, seq, cache_k, cache_v, layer_j, prev):
    tq = min(512, seq)
    nq = seq // tq
    hw = 2 * HD
    b0 = row0 // seq
    q0 = row0 // tq
    v_col0 = 2 * (2 * DIFF_HEADS * HD) // hw
    in_specs = [
        pl.BlockSpec(memory_space=pltpu.SMEM),
        pl.BlockSpec((tq, hw), lambda b, h, i: (q0 + b * nq + i, h)),
        pl.BlockSpec((seq, hw), lambda b, h, i: (b + b0, h)),
        pl.BlockSpec((seq, hw), lambda b, h, i: (b + b0, v_col0 + h)),
        pl.BlockSpec((1, hw), lambda b, h, i: (0, 0)),
    ]
    args = [lam, q, k, qkv, g_sub.reshape(1, hw)]
    cache = cache_k is not None
    if cache:
        past = cache_k.shape[2]
        ck = cache_k.reshape(cache_k.shape[0], cache_k.shape[1], past, 2 * DIFF_HEADS * HD)
        cv = cache_v.reshape(ck.shape)
        in_specs += [pl.BlockSpec((None, None, past, hw), lambda b, h, i: (b, layer_j, 0, h))] * 2
        args += [ck, cv]
    aliases = {}
    if prev is not None:
        in_specs.append(pl.BlockSpec(memory_space=pl.ANY))
        args.append(prev)
        aliases = {len(args) - 1: 0}
    return pl.pallas_call(
        functools.partial(_diff_attn_kernel, cache=cache, out_scale=1.0 - lam_init,
                          aliased=prev is not None),
        grid=(nb, DIFF_HEADS, nq), in_specs=in_specs,
        out_specs=pl.BlockSpec((tq, hw), lambda b, h, i: (q0 + b * nq + i, h)),
        out_shape=jax.ShapeDtypeStruct((st.n, DIFF_HEADS * hw), BF16),
        input_output_aliases=aliases,
        compiler_params=_cparams(("parallel", "parallel", "parallel")), name="diff_attn",
    )(*args)


def _conv_kernel(x_ref, w_ref, b_ref, *rest):
    o_ref, pad_ref = rest[-2:]
    seq = x_ref.shape[0]
    half = SSD_CONV // 2
    zeros = jnp.zeros((8, x_ref.shape[1]), F32)
    pad_ref[0:8, :] = zeros
    pad_ref[seq + 8:seq + 16, :] = zeros
    pad_ref[8:seq + 8, :] = x_ref[...]
    xp = pad_ref[...]
    rows = seq + 16
    acc = jnp.broadcast_to(b_ref[...], x_ref.shape)
    for k in range(SSD_CONV):
        shifted = xp if k == half else pltpu.roll(xp, (half - k) % rows, 0)
        acc = acc + w_ref[k:k + 1, :] * shifted[8:8 + seq, :]
    o_ref[...] = _silu(acc)


def _ssd_conv(st, xbc, conv_w, conv_b, row0, nb, seq, prev):
    tc = 512
    b0 = row0 // seq
    in_specs = [pl.BlockSpec((seq, tc), lambda b, c: (b + b0, c)),
                pl.BlockSpec((SSD_CONV, tc), lambda b, c: (0, c)),
                pl.BlockSpec((1, tc), lambda b, c: (0, c))]
    args = [xbc, conv_w, conv_b.reshape(1, SSD_CONV_DIM)]
    aliases = {}
    if prev is not None:
        in_specs.append(pl.BlockSpec(memory_space=pl.ANY))
        args.append(prev)
        aliases = {len(args) - 1: 0}
    return pl.pallas_call(
        _conv_kernel, grid=(nb, SSD_CONV_DIM // tc),
        in_specs=in_specs,
        out_specs=pl.BlockSpec((seq, tc), lambda b, c: (b + b0, c)),
        out_shape=jax.ShapeDtypeStruct((st.n, SSD_CONV_DIM), F32),
        scratch_shapes=[pltpu.VMEM((seq + 16, tc), F32)],
        input_output_aliases=aliases,
        compiler_params=_cparams(("parallel", "parallel")), name="ssd_conv",
    )(*args)


def _ssd_scan_kernel(x_ref, b_ref, c_ref, dt_ref, bias_ref, alog_ref, *rest,
                     rev, nc, has_init, want_final, aliased):
    pos = 0
    h0_ref = None
    if has_init:
        h0_ref = rest[pos]; pos += 1
    if aliased:
        pos += 1
    y_ref = rest[pos]; pos += 1
    hf_ref = None
    if want_final:
        hf_ref = rest[pos]; pos += 1
    state_ref, acol_ref, arow_ref, dtrow_ref, wrow_ref, dec_ref = rest[pos:]
    c = pl.program_id(1)
    q = SSD_CHUNK
    lane0 = SSD_HEADS if rev else 0
    r_heads = SSD_HEADS // SSD_GROUPS
    gw = r_heads * SSD_HEADDIM

    ii = lax.broadcasted_iota(jnp.int32, (q, q), 0)
    jj = lax.broadcasted_iota(jnp.int32, (q, q), 1)
    causal = (jj >= ii) if rev else (jj <= ii)

    @pl.when(c == 0)
    def _():
        if has_init:
            for blk in range(SSD_INNER // 128):
                state_ref[:, blk * 128:(blk + 1) * 128] = h0_ref[blk * 128:(blk + 1) * 128, :].T
        else:
            state_ref[...] = jnp.zeros(state_ref.shape, F32)

    xdt = dt_ref[...] + bias_ref[...]
    dt = jnp.maximum(xdt, 0.0) + jnp.log(1.0 + jnp.exp(-jnp.abs(xdt)))
    la = dt * (-jnp.exp(alog_ref[...]))
    tri = causal.astype(BF16)
    tri_t = ((ii >= jj) if rev else (ii <= jj)).astype(BF16)
    la3 = _split3(la)
    acol_ref[...] = (jnp.dot(tri, la3[0], preferred_element_type=F32)
                     + jnp.dot(tri, la3[1], preferred_element_type=F32)
                     + jnp.dot(tri, la3[2], preferred_element_type=F32))
    arow = _dot3(_split3(la.T), tri_t)
    end = 0 if rev else q - 1
    a_end = jnp.broadcast_to(arow[:, end:end + 1], arow.shape)
    dt_t = dt.T
    arow_ref[...] = arow
    dtrow_ref[...] = dt_t
    wrow_ref[...] = jnp.exp(a_end - arow) * dt_t
    dec_ref[...] = jnp.exp(a_end)

    lane_lo = lax.broadcasted_iota(jnp.int32, (1, 128), 1) < SSD_HEADDIM
    zeros_blk = jnp.zeros((SSD_STATE, q), BF16)

    for g in range(SSD_GROUPS):
        bmat = b_ref[:, g * SSD_STATE:(g + 1) * SSD_STATE]
        cmat = c_ref[:, g * SSD_STATE:(g + 1) * SSD_STATE]
        cb = lax.dot_general(cmat.astype(BF16), bmat.astype(BF16), (((1,), (1,)), ((), ())),
                             preferred_element_type=F32)
        b_t = bmat.T
        for pair in range(r_heads // 2):
            col = g * gw + pair * 128
            x_pair = x_ref[:, col:col + 128].astype(BF16)
            st_old = state_ref[:, col:col + 128]
            rhs = jnp.concatenate([x_pair, st_old.astype(BF16)], axis=0)
            res = []
            for sub in range(2):
                hd = lane0 + g * r_heads + 2 * pair + sub
                a_i = jnp.broadcast_to(acol_ref[:, hd:hd + 1], (q, q))
                seg = a_i - arow_ref[hd:hd + 1, :]
                decay = jnp.exp(jnp.where(causal, seg, -jnp.inf))
                m_intra = (cb * decay * dtrow_ref[hd:hd + 1, :]).astype(BF16)
                c_scaled = (cmat * jnp.exp(a_i)).astype(BF16)
                bw_t = (b_t * wrow_ref[hd:hd + 1, :]).astype(BF16)
                lhs = jnp.concatenate(
                    [jnp.concatenate([m_intra, c_scaled], axis=1),
                     jnp.concatenate([bw_t, zeros_blk], axis=1)], axis=0)
                res.append(jnp.dot(lhs, rhs, preferred_element_type=F32))
            hd0 = lane0 + g * r_heads + 2 * pair
            y_ref[:, col:col + 128] = jnp.where(lane_lo, res[0][:q], res[1][:q])
            dec_pair = jnp.where(lane_lo, dec_ref[hd0:hd0 + 1, :], dec_ref[hd0 + 1:hd0 + 2, :])
            upd = jnp.where(lane_lo, res[0][q:], res[1][q:])
            state_ref[:, col:col + 128] = dec_pair * st_old + upd

    if want_final:
        @pl.when(c == nc - 1)
        def _():
            for blk in range(SSD_INNER // 128):
                hf_ref[blk * 128:(blk + 1) * 128, :] = state_ref[:, blk * 128:(blk + 1) * 128].T


def _ssd_scan(st, xbc, dt, dt_bias, a_log, row0, nb, seq, rev, h0, layer_j, want_final, prev):
    nc = seq // SSD_CHUNK
    c0 = row0 // SSD_CHUNK

    def chunk(b, c):
        return c0 + b * nc + (nc - 1 - c if rev else c)

    in_specs = [
        pl.BlockSpec((SSD_CHUNK, SSD_INNER), lambda b, c: (chunk(b, c), 0)),
        pl.BlockSpec((SSD_CHUNK, SSD_GN), lambda b, c: (chunk(b, c), SSD_INNER // SSD_GN)),
        pl.BlockSpec((SSD_CHUNK, SSD_GN), lambda b, c: (chunk(b, c), SSD_INNER // SSD_GN + 1)),
        pl.BlockSpec((SSD_CHUNK, 128), lambda b, c: (chunk(b, c), 0)),
        pl.BlockSpec((1, 128), lambda b, c: (0, 0)),
        pl.BlockSpec((1, 128), lambda b, c: (0, 0)),
    ]
    args = [xbc, xbc, xbc, dt, dt_bias.reshape(1, 128), a_log.reshape(1, 128)]
    if h0 is not None:
        h0r = h0.reshape(h0.shape[0], h0.shape[1], SSD_INNER, SSD_STATE)
        in_specs.append(pl.BlockSpec((None, None, SSD_INNER, SSD_STATE),
                                     lambda b, c: (b, layer_j, 0, 0)))
        args.append(h0r)
    aliases = {}
    if prev is not None:
        in_specs.append(pl.BlockSpec(memory_space=pl.ANY))
        args.append(prev)
        aliases = {len(args) - 1: 0}
    out_shape = [jax.ShapeDtypeStruct((st.n, SSD_INNER), F32)]
    out_specs = [pl.BlockSpec((SSD_CHUNK, SSD_INNER), lambda b, c: (chunk(b, c), 0))]
    if want_final:
        out_shape.append(jax.ShapeDtypeStruct((nb, SSD_INNER, SSD_STATE), F32))
        out_specs.append(pl.BlockSpec((None, SSD_INNER, SSD_STATE), lambda b, c: (b, 0, 0)))
    res = pl.pallas_call(
        functools.partial(_ssd_scan_kernel, rev=rev, nc=nc, has_init=h0 is not None,
                          want_final=want_final, aliased=prev is not None),
        grid=(nb, nc), in_specs=in_specs, out_specs=out_specs, out_shape=out_shape,
        scratch_shapes=[pltpu.VMEM((SSD_STATE, SSD_INNER), F32),
                        pltpu.VMEM((SSD_CHUNK, 128), F32)] + [pltpu.VMEM((128, SSD_CHUNK), F32)] * 4,
        input_output_aliases=aliases,
        compiler_params=_cparams(("parallel", "arbitrary")), name="ssd_scan",
    )(*args)
    return res


def _ssd_gate_kernel(yf_ref, yb_ref, x_ref, z_ref, d_ref, g_ref, o_ref):
    gw = SSD_INNER // SSD_GROUPS
    for grp in range(SSD_GROUPS):
        sl = slice(grp * gw, (grp + 1) * gw)
        y = yf_ref[:, sl] + yb_ref[:, sl] + d_ref[:, sl] * x_ref[:, sl]
        gated = y * _silu(z_ref[:, sl])
        normed = gated * lax.rsqrt(jnp.mean(gated * gated, axis=-1, keepdims=True) + EPS)
        o_ref[:, sl] = (normed * g_ref[:, sl]).astype(o_ref.dtype)


def _ssd_gate(yf, yb, xbc, z, d_sum, g_norm):
    n = yf.shape[0]
    tm = 256
    row = pl.BlockSpec((tm, SSD_INNER), lambda i: (i, 0))
    vec = pl.BlockSpec((1, SSD_INNER), lambda i: (0, 0))
    return pl.pallas_call(
        _ssd_gate_kernel, grid=(n // tm,),
        in_specs=[row, row, row, row, vec, vec], out_specs=row,
        out_shape=jax.ShapeDtypeStruct((n, SSD_INNER), BF16),
        compiler_params=_cparams(("parallel",)), name="ssd_gate",
    )(yf, yb, xbc, z, d_sum, g_norm.reshape(1, SSD_INNER))


def _moe_up_kernel(be_ref, nu_ref, a_ref, wg_ref, wu_ref, o_ref):
    m = pl.program_id(1)

    @pl.when(m < nu_ref[0])
    def _():
        a = a_ref[...]
        gate = jnp.dot(a, wg_ref[...].astype(BF16), preferred_element_type=F32)
        up = jnp.dot(a, wu_ref[...].astype(BF16), preferred_element_type=F32)
        o_ref[...] = (_silu(gate) * up).astype(o_ref.dtype)

    @pl.when(m >= nu_ref[0])
    def _():
        o_ref[...] = jnp.zeros(o_ref.shape, o_ref.dtype)


def _moe_down_kernel(be_ref, nu_ref, a_ref, w_ref, o_ref):
    m = pl.program_id(0)

    @pl.when(m < nu_ref[0])
    def _():
        o_ref[...] = jnp.dot(a_ref[...], w_ref[...].astype(BF16),
                             preferred_element_type=F32).astype(o_ref.dtype)

    @pl.when(m >= nu_ref[0])
    def _():
        o_ref[...] = jnp.zeros(o_ref.shape, o_ref.dtype)


def _moe(st, h, logits, w_gu, w_down, f):
    n = st.n
    tb = MOE_BLOCK
    n_assign = n * TOP_K
    n_blocks = n_assign // tb + N_EXPERTS
    slots = n_blocks * tb
    top_logit, top_e = lax.top_k(logits, TOP_K)
    gates = jax.nn.softmax(top_logit, axis=-1)
    flat_e = top_e.reshape(-1).astype(jnp.int32)
    onehot = (flat_e[:, None] == jnp.arange(N_EXPERTS, dtype=jnp.int32)[None, :]).astype(jnp.int32)
    rank = jnp.sum((jnp.cumsum(onehot, axis=0) - onehot) * onehot, axis=1)
    counts = jnp.sum(onehot, axis=0)
    padded = (counts + tb - 1) // tb * tb
    pad_end = jnp.cumsum(padded)
    pad_start = pad_end - padded
    dest = pad_start[flat_e] + rank
    slot_tok = jnp.zeros((slots,), jnp.int32).at[dest].set(
        jnp.arange(n_assign, dtype=jnp.int32) // TOP_K, unique_indices=True, mode="promise_in_bounds")
    block_start = jnp.arange(n_blocks, dtype=jnp.int32) * tb
    block_e = jnp.minimum(jnp.searchsorted(pad_end, block_start, side="right"),
                          N_EXPERTS - 1).astype(jnp.int32)
    n_used = (pad_end[-1] // tb).astype(jnp.int32).reshape(1)

    xs = h.at[slot_tok].get(mode="promise_in_bounds")

    tn = 512
    nf = EXPERT_DIM // tn

    def live_block(m, nu):
        return jnp.minimum(m, nu[0] - 1)

    act = pl.pallas_call(
        _moe_up_kernel,
        grid_spec=pltpu.PrefetchScalarGridSpec(
            num_scalar_prefetch=2, grid=(nf, n_blocks),
            in_specs=[
                pl.BlockSpec((tb, D_MODEL), lambda j, m, be, nu: (live_block(m, nu), 0)),
                pl.BlockSpec((None, None, D_MODEL, tn),
                             lambda j, m, be, nu: (f, be[live_block(m, nu)], 0, j)),
                pl.BlockSpec((None, None, D_MODEL, tn),
                             lambda j, m, be, nu: (f, be[live_block(m, nu)], 0, j + nf)),
            ],
            out_specs=pl.BlockSpec((tb, tn), lambda j, m, be, nu: (m, j)),
        ),
        out_shape=jax.ShapeDtypeStruct((slots, EXPERT_DIM), BF16),
        compiler_params=_cparams(("parallel", "arbitrary")), name="moe_up",
    )(block_e, n_used, xs, w_gu, w_gu)

    tn2 = 256
    yb = pl.pallas_call(
        _moe_down_kernel,
        grid_spec=pltpu.PrefetchScalarGridSpec(
            num_scalar_prefetch=2, grid=(n_blocks, D_MODEL // tn2),
            in_specs=[
                pl.BlockSpec((tb, EXPERT_DIM), lambda m, j, be, nu: (live_block(m, nu), 0)),
                pl.BlockSpec((None, None, EXPERT_DIM, tn2),
                             lambda m, j, be, nu: (f, be[live_block(m, nu)], 0, j)),
            ],
            out_specs=pl.BlockSpec((tb, tn2), lambda m, j, be, nu: (m, j)),
        ),
        out_shape=jax.ShapeDtypeStruct((slots, D_MODEL), BF16),
        compiler_params=_cparams(("arbitrary", "arbitrary")), name="moe_down",
    )(block_e, n_used, act, w_down)

    dest2 = dest.reshape(n, TOP_K)
    y0 = yb.at[dest2[:, 0]].get(mode="promise_in_bounds")
    y1 = yb.at[dest2[:, 1]].get(mode="promise_in_bounds")
    return y0, y1, gates


def _combine_kernel(x_ref, y0_ref, y1_ref, g_ref, gate_ref, o_ref):
    y = g_ref[:, 0:1] * y0_ref[...].astype(F32) + g_ref[:, 1:2] * y1_ref[...].astype(F32)
    o_ref[...] = x_ref[...] + gate_ref[...] * y


def _combine(st, x, y0, y1, gates, mods, layer):
    tm = st.tile(512)
    row = pl.BlockSpec((tm, D_MODEL), lambda i: (i, 0))
    return pl.pallas_call(
        _combine_kernel, grid=(st.n // tm,),
        in_specs=[row, row, row, pl.BlockSpec((tm, TOP_K), lambda i: (i, 0)),
                  pl.BlockSpec((None, 1, D_MODEL),
                               lambda i: (_mod_index(layer, 5, st.mod_row(i, tm)), 0, 0))],
        out_specs=row, out_shape=jax.ShapeDtypeStruct((st.n, D_MODEL), F32),
        compiler_params=_cparams(("parallel",)), name="moe_combine",
    )(x, y0, y1, gates, mods)


def kernel(x_prompt, x_sample, cache_attn_k, cache_attn_v, cache_diff_k, cache_diff_v, state_ssd_fwd, state_ssd_bwd, c, c_ctx, w_ada, b_ada, g_mix, g_ffn, g_final, attn_w_in, attn_g_q, attn_g_k, attn_w_out, diff_w_in, diff_lambda, diff_g_sub, diff_w_out, ssd_w_in, ssd_conv_w, ssd_conv_b, ssd_dt_bias, ssd_a_log, ssd_d, ssd_g_norm, ssd_w_out, ffn_w_gu, ffn_w_down, moe_w_router, moe_w_gu, moe_w_down):
    pb, pl_len, _ = x_prompt.shape
    sb, sl_len, _ = x_sample.shape
    st = _Stream(pb, pl_len, sb, sl_len)
    np_, ns = st.n_prompt, st.n_sample
    tm = st.tile(1024)

    x = jnp.concatenate([x_prompt.reshape(np_, D_MODEL), x_sample.reshape(ns, D_MODEL)], axis=0)

    cond = jnp.zeros((MOD_ROWS, D_MODEL), F32).at[:sb].set(c).at[CTX_ROW].set(c_ctx)
    cond_act = jax.nn.silu(cond).astype(BF16)
    mods = []
    for i in range(DEPTH):
        m = _matmul(cond_act, w_ada, (i,), n_off=0, n_out=6 * D_MODEL, tm=MOD_ROWS, tn=1024,
                    tk=D_MODEL, out_dtype=F32, name="ada")
        mods.append(m + b_ada[i][None, :])
    mods = jnp.stack(mods).reshape(DEPTH * MOD_ROWS * 6, 1, D_MODEL)

    tables = _rope_tables(sl_len, PREP_ROWS)

    tm2 = st.tile(2048)

    def gate_index(layer, which, tile):
        return lambda i: _mod_index(layer, which, st.mod_row(i, tile))

    new_attn_k, new_attn_v, new_diff_k, new_diff_v, new_ssd_f, new_ssd_b = [], [], [], [], [], []
    for i in range(DEPTH):
        kind, j = i % 3, i // 3
        h = _norm_mod(st, x, g_mix[i], mods, i, 0)
        if kind == 0:
            qkv = _matmul(h, attn_w_in, (j,), n_off=0, n_out=attn_w_in.shape[-1], tm=tm, tn=1024,
                          tk=D_MODEL, out_dtype=F32, name="attn_in")
            q, k = _gqa_prep(st, qkv, attn_g_q[j], attn_g_k[j], tables)
            o = _gqa_attn(st, q, k, qkv, 0, pb, pl_len, None, None, j, None)
            o = _gqa_attn(st, q, k, qkv, np_, sb, sl_len, cache_attn_k, cache_attn_v, j, o)
            new_attn_k.append(k[:np_].reshape(pb, pl_len, ATTN_KV_HEADS, HD))
            new_attn_v.append(qkv[:np_, (ATTN_HEADS + ATTN_KV_HEADS) * HD:]
                              .reshape(pb, pl_len, ATTN_KV_HEADS, HD))
            x = _matmul(o, attn_w_out, (j,), n_off=0, n_out=D_MODEL, tm=tm2, tn=512, tk=2048,
                        out_dtype=F32, epilogue="resid", resid=x, mods=mods,
                        gate_index=gate_index(i, 2, tm2), name="attn_out")
        elif kind == 1:
            lam_init = 0.8 - 0.6 * math.exp(-0.3 * i)
            lf = diff_lambda[j]
            lam = (jnp.exp(jnp.sum(lf[0] * lf[1])) - jnp.exp(jnp.sum(lf[2] * lf[3])) + lam_init).reshape(1)
            qkv = _matmul(h, diff_w_in, (j,), n_off=0, n_out=diff_w_in.shape[-1], tm=tm, tn=1024,
                          tk=D_MODEL, out_dtype=F32, name="diff_in")
            w = 2 * DIFF_HEADS * HD
            q, k = _diff_prep(st, qkv, tables)
            o = _diff_attn(st, lam, q, k, qkv, diff_g_sub[j], lam_init, 0, pb, pl_len, None, None, j, None)
            o = _diff_attn(st, lam, q, k, qkv, diff_g_sub[j], lam_init, np_, sb, sl_len,
                           cache_diff_k, cache_diff_v, j, o)
            new_diff_k.append(qkv[:np_, w:2 * w].reshape(pb, pl_len, 2 * DIFF_HEADS, HD))
            new_diff_v.append(qkv[:np_, 2 * w:].reshape(pb, pl_len, DIFF_HEADS, 2 * HD))
            x = _matmul(o, diff_w_out, (j,), n_off=0, n_out=D_MODEL, tm=tm2, tn=512, tk=2048,
                        out_dtype=F32, epilogue="resid", resid=x, mods=mods,
                        gate_index=gate_index(i, 2, tm2), name="diff_out")
        else:
            z = _matmul(h, ssd_w_in, (j,), n_off=0, n_out=SSD_INNER, tm=tm, tn=1024, tk=D_MODEL,
                        out_dtype=F32, name="ssd_in_z")
            xbc = _matmul(h, ssd_w_in, (j,), n_off=SSD_INNER, n_out=SSD_CONV_DIM, tm=tm, tn=1024,
                          tk=D_MODEL, out_dtype=F32, name="ssd_in_xbc")
            dt = _matmul(h, ssd_w_in, (j,), n_off=SSD_INNER + SSD_CONV_DIM, n_out=2 * SSD_HEADS,
                         tm=tm, tn=128, tk=D_MODEL, out_dtype=F32, name="ssd_in_dt")
            xc = _ssd_conv(st, xbc, ssd_conv_w[j], ssd_conv_b[j], 0, pb, pl_len, None)
            xc = _ssd_conv(st, xbc, ssd_conv_w[j], ssd_conv_b[j], np_, sb, sl_len, xc)
            ys = []
            for rev, h0 in ((False, state_ssd_fwd), (True, state_ssd_bwd)):
                y, hfin = _ssd_scan(st, xc, dt, ssd_dt_bias[j], ssd_a_log[j], 0, pb, pl_len, rev,
                                    None, j, True, None)
                (y,) = _ssd_scan(st, xc, dt, ssd_dt_bias[j], ssd_a_log[j], np_, sb, sl_len, rev,
                                 h0, j, False, y)
                ys.append(y)
                (new_ssd_b if rev else new_ssd_f).append(
                    hfin.reshape(pb, SSD_HEADS, SSD_HEADDIM, SSD_STATE))
            d_sum = jnp.repeat(ssd_d[j][0] + ssd_d[j][1], SSD_HEADDIM).reshape(1, SSD_INNER)
            o = _ssd_gate(ys[0], ys[1], xc, z, d_sum, ssd_g_norm[j])
            x = _matmul(o, ssd_w_out, (j,), n_off=0, n_out=D_MODEL, tm=tm, tn=512, tk=SSD_INNER,
                        out_dtype=F32, epilogue="resid", resid=x, mods=mods,
                        gate_index=gate_index(i, 2, tm), name="ssd_out")

        f = i // 2
        if i % 2 == 0:
            h = _norm_mod(st, x, g_ffn[i], mods, i, 3)
            act = _matmul(h, ffn_w_gu, (f,), n_off=0, n_out=FFN_DIM, tm=tm, tn=512, tk=D_MODEL,
                          out_dtype=BF16, epilogue="swiglu", up_off=FFN_DIM, name="ffn_up")
            x = _matmul(act, ffn_w_down, (f,), n_off=0, n_out=D_MODEL, tm=tm, tn=256, tk=FFN_DIM,
                        out_dtype=F32, epilogue="resid", resid=x, mods=mods,
                        gate_index=gate_index(i, 5, tm), name="ffn_down")
        else:
            h, logits = _norm_mod(st, x, g_ffn[i], mods, i, 3, w_router=moe_w_router[f])
            y0, y1, gates = _moe(st, h, logits[:, :N_EXPERTS], moe_w_gu, moe_w_down, f)
            x = _combine(st, x, y0, y1, gates, mods, i)

    y_prompt = _final_norm(x, g_final, 0, np_).reshape(pb, pl_len, D_MODEL)
    y_sample = _final_norm(x, g_final, np_, ns).reshape(sb, sl_len, D_MODEL)
    return (y_prompt, y_sample,
            jnp.stack(new_attn_k, axis=1), jnp.stack(new_attn_v, axis=1),
            jnp.stack(new_diff_k, axis=1), jnp.stack(new_diff_v, axis=1),
            jnp.stack(new_ssd_f, axis=1), jnp.stack(new_ssd_b, axis=1))
```

```python
import functools
import math

import jax
import jax.numpy as jnp
from jax import lax
from jax.experimental import pallas as pl
from jax.experimental.pallas import tpu as pltpu

F32 = jnp.float32
BF16 = jnp.bfloat16

D_MODEL = 2048
DEPTH = 4
GRID_W = 64
ROPE_THETA = 10000.0
EPS = 1e-6
HD = 128
ATTN_HEADS = 16
ATTN_KV_HEADS = 4
DIFF_HEADS = 8
SSD_INNER = 2 * D_MODEL
SSD_HEADDIM = 64
SSD_HEADS = SSD_INNER // SSD_HEADDIM
SSD_GROUPS = 8
SSD_STATE = 128
SSD_CONV = 5
SSD_CHUNK = 128
SSD_GN = SSD_GROUPS * SSD_STATE
SSD_CONV_DIM = SSD_INNER + 2 * SSD_GN
FFN_DIM = 5632
N_EXPERTS = 8
TOP_K = 2
EXPERT_DIM = 7168

MOD_ROWS = 16
CTX_ROW = 8
MOE_BLOCK = 1024
PREP_ROWS = 256
VMEM_LIMIT = 56 * 1024 * 1024


def _cparams(sem):
    return pltpu.CompilerParams(dimension_semantics=sem, vmem_limit_bytes=VMEM_LIMIT)


def _silu(x):
    return x / (1.0 + jnp.exp(-x))


def _split3(x):
    hi = x.astype(BF16)
    r = x - hi.astype(F32)
    mid = r.astype(BF16)
    lo = (r - mid.astype(F32)).astype(BF16)
    return hi, mid, lo


def _dot3(a3, b):
    out = jnp.dot(a3[0], b, preferred_element_type=F32)
    out = out + jnp.dot(a3[1], b, preferred_element_type=F32)
    return out + jnp.dot(a3[2], b, preferred_element_type=F32)


class _Stream:
    def __init__(self, n_prompt_batch, prompt_len, n_sample_batch, sample_len):
        self.pb, self.pl_, self.sb, self.sl = n_prompt_batch, prompt_len, n_sample_batch, sample_len
        self.n_prompt = n_prompt_batch * prompt_len
        self.n_sample = n_sample_batch * sample_len
        self.n = self.n_prompt + self.n_sample

    def mod_row(self, i, tm):
        pt = self.n_prompt // tm
        per = self.sl // tm
        return jnp.where(i < pt, CTX_ROW, (i - pt) // per)

    def tile(self, want):
        t = want
        while self.n_prompt % t or self.sl % t:
            t //= 2
        return t


def _mod_index(layer, which, row):
    return (layer * MOD_ROWS + row) * 6 + which


def _norm_mod_store(x, g_ref, shift_ref, scale_ref, o_ref, wr_ref=None, lg_ref=None):
    y = x * lax.rsqrt(jnp.mean(x * x, axis=-1, keepdims=True) + EPS) * g_ref[...]
    h = y * (1.0 + scale_ref[...]) + shift_ref[...]
    if wr_ref is not None:
        w = wr_ref[...]
        whi = w.astype(BF16)
        wlo = (w - whi.astype(F32)).astype(BF16)
        h3 = _split3(h)
        lg = _dot3(h3, whi)
        lg = lg + jnp.dot(h3[0], wlo, preferred_element_type=F32)
        lg = lg + jnp.dot(h3[1], wlo, preferred_element_type=F32)
        lg_ref[...] = lg
    o_ref[...] = h.astype(o_ref.dtype)


def _norm_mod_kernel(x_ref, g_ref, shift_ref, scale_ref, *rest, router):
    if router:
        wr_ref, o_ref, lg_ref = rest
        _norm_mod_store(x_ref[...], g_ref, shift_ref, scale_ref, o_ref, wr_ref, lg_ref)
    else:
        (o_ref,) = rest
        _norm_mod_store(x_ref[...], g_ref, shift_ref, scale_ref, o_ref)


def _router_pad(w_router):
    return jnp.zeros((D_MODEL, 128), F32).at[:, :N_EXPERTS].set(w_router)


def _out_norm_kernel(a_ref, w_ref, x_ref, gate_ref, g_ref, shift_ref, scale_ref, *rest, router):
    xn = x_ref[...] + gate_ref[...] * jnp.dot(a_ref[...], w_ref[...], preferred_element_type=F32)
    if router:
        wr_ref, xo_ref, ho_ref, lg_ref = rest
        _norm_mod_store(xn, g_ref, shift_ref, scale_ref, ho_ref, wr_ref, lg_ref)
    else:
        xo_ref, ho_ref = rest
        _norm_mod_store(xn, g_ref, shift_ref, scale_ref, ho_ref)
    xo_ref[...] = xn


def _out_proj_norm(st, a, w, x, mods, layer, g_next, w_router=None):
    tm = st.tile(512)
    kdim = a.shape[1]

    def mod(which):
        return pl.BlockSpec((None, 1, D_MODEL),
                            lambda i: (_mod_index(layer, which, st.mod_row(i, tm)), 0, 0))

    row = pl.BlockSpec((tm, D_MODEL), lambda i: (i, 0))
    in_specs = [pl.BlockSpec((tm, kdim), lambda i: (i, 0)),
                pl.BlockSpec((kdim, D_MODEL), lambda i: (0, 0), pipeline_mode=pl.Buffered(1)),
                row, mod(2), pl.BlockSpec((1, D_MODEL), lambda i: (0, 0)), mod(3), mod(4)]
    args = [a, w, x, mods, g_next.reshape(1, D_MODEL), mods, mods]
    out_shape = [jax.ShapeDtypeStruct((st.n, D_MODEL), F32), jax.ShapeDtypeStruct((st.n, D_MODEL), BF16)]
    out_specs = [row, row]
    if w_router is not None:
        in_specs.append(pl.BlockSpec((D_MODEL, 128), lambda i: (0, 0)))
        args.append(_router_pad(w_router))
        out_shape.append(jax.ShapeDtypeStruct((st.n, 128), F32))
        out_specs.append(pl.BlockSpec((tm, 128), lambda i: (i, 0)))
    return pl.pallas_call(
        functools.partial(_out_norm_kernel, router=w_router is not None),
        grid=(st.n // tm,), in_specs=in_specs, out_specs=out_specs, out_shape=out_shape,
        compiler_params=_cparams(("parallel",)), name="out_proj_norm",
    )(*args)


def _norm_mod(st, x, g, mods, layer, which_shift, w_router=None):
    tm = st.tile(512)
    nt = st.n // tm
    g2 = g.reshape(1, D_MODEL)
    in_specs = [
        pl.BlockSpec((tm, D_MODEL), lambda i: (i, 0)),
        pl.BlockSpec((1, D_MODEL), lambda i: (0, 0)),
        pl.BlockSpec((None, 1, D_MODEL),
                     lambda i: (_mod_index(layer, which_shift, st.mod_row(i, tm)), 0, 0)),
        pl.BlockSpec((None, 1, D_MODEL),
                     lambda i: (_mod_index(layer, which_shift + 1, st.mod_row(i, tm)), 0, 0)),
    ]
    args = [x, g2, mods, mods]
    out_shape = [jax.ShapeDtypeStruct((st.n, D_MODEL), BF16)]
    out_specs = [pl.BlockSpec((tm, D_MODEL), lambda i: (i, 0))]
    if w_router is not None:
        in_specs.append(pl.BlockSpec((D_MODEL, 128), lambda i: (0, 0)))
        args.append(_router_pad(w_router))
        out_shape.append(jax.ShapeDtypeStruct((st.n, 128), F32))
        out_specs.append(pl.BlockSpec((tm, 128), lambda i: (i, 0)))
    res = pl.pallas_call(
        functools.partial(_norm_mod_kernel, router=w_router is not None),
        grid=(nt,), in_specs=in_specs, out_specs=out_specs, out_shape=out_shape,
        compiler_params=_cparams(("parallel",)), name="norm_mod",
    )(*args)
    return res if w_router is not None else res[0]


def _final_norm_kernel(x_ref, g_ref, o_ref):
    x = x_ref[...]
    o_ref[...] = x * lax.rsqrt(jnp.mean(x * x, axis=-1, keepdims=True) + EPS) * g_ref[...]


def _final_norm(x, g, row0, nrows):
    tm = 512
    r0 = row0 // tm
    return pl.pallas_call(
        _final_norm_kernel, grid=(nrows // tm,),
        in_specs=[pl.BlockSpec((tm, D_MODEL), lambda i: (i + r0, 0)),
                  pl.BlockSpec((1, D_MODEL), lambda i: (0, 0))],
        out_specs=pl.BlockSpec((tm, D_MODEL), lambda i: (i, 0)),
        out_shape=jax.ShapeDtypeStruct((nrows, D_MODEL), F32),
        compiler_params=_cparams(("parallel",)), name="final_norm",
    )(x, g.reshape(1, D_MODEL))


def _mm_kernel(*refs, nk, epilogue):
    a_ref, w_ref = refs[0], refs[1]
    pos = 2
    w2_ref = x_ref = gate_ref = None
    if epilogue == "swiglu":
        w2_ref = refs[pos]; pos += 1
    if epilogue == "resid":
        x_ref, gate_ref = refs[pos], refs[pos + 1]; pos += 2
    o_ref = refs[pos]; pos += 1
    acc_refs = refs[pos:]
    k = pl.program_id(2)
    a = a_ref[...]
    parts = [jnp.dot(a, w_ref[...].astype(BF16), preferred_element_type=F32)]
    if w2_ref is not None:
        parts.append(jnp.dot(a, w2_ref[...].astype(BF16), preferred_element_type=F32))

    def finish(vals):
        if epilogue == "swiglu":
            o_ref[...] = (_silu(vals[0]) * vals[1]).astype(o_ref.dtype)
        elif epilogue == "resid":
            o_ref[...] = (x_ref[...] + gate_ref[...] * vals[0]).astype(o_ref.dtype)
        else:
            o_ref[...] = vals[0].astype(o_ref.dtype)

    if nk == 1:
        finish(parts)
        return

    @pl.when(k == 0)
    def _():
        for acc, p in zip(acc_refs, parts):
            acc[...] = p

    @pl.when(jnp.logical_and(k > 0, k < nk - 1))
    def _():
        for acc, p in zip(acc_refs, parts):
            acc[...] += p

    @pl.when(k == nk - 1)
    def _():
        finish([acc[...] + p for acc, p in zip(acc_refs, parts)])


def _matmul(a, w, widx, *, n_off, n_out, tm, tn, tk, out_dtype, epilogue="plain",
            up_off=None, resid=None, mods=None, gate_index=None, name="mm"):
    m, kdim = a.shape
    nk = kdim // tk
    assert m % tm == 0 and kdim % tk == 0 and n_out % tn == 0 and n_off % tn == 0
    nb = n_off // tn
    lead = (None,) * len(widx)
    in_specs = [
        pl.BlockSpec((tm, tk), lambda i, j, k: (i, k)),
        pl.BlockSpec(lead + (tk, tn), lambda i, j, k: widx + (k, j + nb)),
    ]
    args = [a, w]
    if epilogue == "swiglu":
        ub = up_off // tn
        in_specs.append(pl.BlockSpec(lead + (tk, tn), lambda i, j, k: widx + (k, j + ub)))
        args.append(w)
    if epilogue == "resid":
        in_specs.append(pl.BlockSpec((tm, tn), lambda i, j, k: (i, j)))
        in_specs.append(pl.BlockSpec((None, 1, tn), lambda i, j, k: (gate_index(i), 0, j)))
        args += [resid, mods]
    n_acc = 0 if nk == 1 else (2 if epilogue == "swiglu" else 1)
    return pl.pallas_call(
        functools.partial(_mm_kernel, nk=nk, epilogue=epilogue),
        grid=(m // tm, n_out // tn, nk),
        in_specs=in_specs,
        out_specs=pl.BlockSpec((tm, tn), lambda i, j, k: (i, j)),
        out_shape=jax.ShapeDtypeStruct((m, n_out), out_dtype),
        scratch_shapes=[pltpu.VMEM((tm, tn), F32)] * n_acc,
        compiler_params=_cparams(("parallel", "parallel", "arbitrary")), name=name,
    )(*args)


def _rope_tables(length, ident_rows):
    nf = HD // 4
    inv = ROPE_THETA ** (-jnp.arange(nf, dtype=F32) / nf)
    rows = length // GRID_W
    row = jnp.repeat(jnp.arange(rows, dtype=F32), GRID_W)
    col = jnp.tile(jnp.arange(GRID_W, dtype=F32), rows)
    ar = row[:, None] * inv
    ac = col[:, None] * inv
    ang = jnp.concatenate([ar, ar, ac, ac], axis=-1)
    cos, sin = jnp.cos(ang), jnp.sin(ang)
    lane = jnp.arange(HD) % (HD // 2)
    sin_signed = jnp.where(lane >= HD // 4, sin, -sin)
    ones = jnp.ones((ident_rows, HD), F32)
    zeros = jnp.zeros((ident_rows, HD), F32)
    return jnp.concatenate([cos, ones]), jnp.concatenate([sin_signed, zeros])


def _table_block(st, i):
    pt = st.n_prompt // PREP_ROWS
    per = st.sl // PREP_ROWS
    return jnp.where(i < pt, per, (i - pt) % per)


def _rope(y, cos, sin_signed):
    partner = lax.broadcasted_iota(jnp.int32, y.shape, 1) ^ (HD // 4)
    return y * cos + jnp.take_along_axis(y, partner, axis=1) * sin_signed


def _gqa_prep_kernel(q_ref, k_ref, gq_ref, gk_ref, cos_ref, sin_ref, qo_ref, ko_ref):
    cos, sin = cos_ref[...], sin_ref[...]
    scale = 1.0 / math.sqrt(HD)

    def norm(x, g):
        return x * lax.rsqrt(jnp.mean(x * x, axis=-1, keepdims=True) + EPS) * g

    for h in range(ATTN_HEADS):
        y = _rope(norm(q_ref[:, h * HD:(h + 1) * HD], gq_ref[...]), cos, sin)
        qo_ref[:, h * HD:(h + 1) * HD] = (y * scale).astype(qo_ref.dtype)
    for h in range(ATTN_KV_HEADS):
        ko_ref[:, h * HD:(h + 1) * HD] = _rope(norm(k_ref[:, h * HD:(h + 1) * HD], gk_ref[...]),
                                               cos, sin)


def _gqa_prep(st, qkv, g_q, g_k, tables):
    tm = PREP_ROWS
    qw = ATTN_HEADS * HD
    kw = ATTN_KV_HEADS * HD
    tab = pl.BlockSpec((tm, HD), lambda i: (_table_block(st, i), 0))
    return pl.pallas_call(
        _gqa_prep_kernel, grid=(st.n // tm,),
        in_specs=[pl.BlockSpec((tm, qw), lambda i: (i, 0)),
                  pl.BlockSpec((tm, kw), lambda i: (i, qw // kw)),
                  pl.BlockSpec((1, HD), lambda i: (0, 0)),
                  pl.BlockSpec((1, HD), lambda i: (0, 0)), tab, tab],
        out_specs=[pl.BlockSpec((tm, qw), lambda i: (i, 0)),
                   pl.BlockSpec((tm, kw), lambda i: (i, 0))],
        out_shape=[jax.ShapeDtypeStruct((st.n, qw), BF16),
                   jax.ShapeDtypeStruct((st.n, kw), F32)],
        compiler_params=_cparams(("parallel",)), name="gqa_prep",
    )(qkv, qkv, g_q.reshape(1, HD), g_k.reshape(1, HD), *tables)


def _softmax_parts(q, ks):
    ss = [lax.dot_general(q, k, (((1,), (1,)), ((), ())), preferred_element_type=F32) for k in ks]
    m = ss[0].max(axis=-1, keepdims=True)
    for s in ss[1:]:
        m = jnp.maximum(m, s.max(axis=-1, keepdims=True))
    ps = [jnp.exp(s - m) for s in ss]
    l = ps[0].sum(axis=-1, keepdims=True)
    for p in ps[1:]:
        l = l + p.sum(axis=-1, keepdims=True)
    return ps, l


def _gqa_attn_kernel(q_ref, k_ref, v_ref, *rest, cache, aliased):
    if aliased:
        rest = rest[:-2] + rest[-1:]
    if cache:
        ck_ref, cv_ref, o_ref = rest
        ks = [ck_ref[...].astype(BF16), k_ref[...].astype(BF16)]
        vs = [cv_ref[...].astype(BF16), v_ref[...].astype(BF16)]
    else:
        (o_ref,) = rest
        ks = [k_ref[...].astype(BF16)]
        vs = [v_ref[...].astype(BF16)]
    grp = ATTN_HEADS // ATTN_KV_HEADS
    for h in range(grp):
        q = q_ref[:, h * HD:(h + 1) * HD]
        ps, l = _softmax_parts(q, ks)
        o = jnp.dot(ps[0].astype(BF16), vs[0], preferred_element_type=F32)
        for p, v in zip(ps[1:], vs[1:]):
            o = o + jnp.dot(p.astype(BF16), v, preferred_element_type=F32)
        o_ref[:, h * HD:(h + 1) * HD] = (o / l).astype(o_ref.dtype)


def _gqa_attn(st, q, k, qkv, row0, nb, seq, cache_k, cache_v, layer_j, prev):
    tq = min(512, seq)
    nq = seq // tq
    grp_w = (ATTN_HEADS // ATTN_KV_HEADS) * HD
    v_col0 = (ATTN_HEADS + ATTN_KV_HEADS)
    b0 = row0 // seq
    q0 = row0 // tq
    in_specs = [
        pl.BlockSpec((tq, grp_w), lambda b, g, i: (q0 + b * nq + i, g)),
        pl.BlockSpec((seq, HD), lambda b, g, i: (b + b0, g)),
        pl.BlockSpec((seq, HD), lambda b, g, i: (b + b0, v_col0 + g)),
    ]
    args = [q, k, qkv]
    cache = cache_k is not None
    if cache:
        past = cache_k.shape[2]
        ck = cache_k.reshape(cache_k.shape[0], cache_k.shape[1], past, ATTN_KV_HEADS * HD)
        cv = cache_v.reshape(ck.shape)
        in_specs += [pl.BlockSpec((None, None, past, HD), lambda b, g, i: (b, layer_j, 0, g))] * 2
        args += [ck, cv]
    aliases = {}
    if prev is not None:
        in_specs.append(pl.BlockSpec(memory_space=pl.ANY))
        args.append(prev)
        aliases = {len(args) - 1: 0}
    return pl.pallas_call(
        functools.partial(_gqa_attn_kernel, cache=cache, aliased=prev is not None),
        grid=(nb, ATTN_KV_HEADS, nq), in_specs=in_specs,
        out_specs=pl.BlockSpec((tq, grp_w), lambda b, g, i: (q0 + b * nq + i, g)),
        out_shape=jax.ShapeDtypeStruct((st.n, ATTN_HEADS * HD), BF16),
        input_output_aliases=aliases,
        compiler_params=_cparams(("parallel", "parallel", "parallel")), name="gqa_attn",
    )(*args)


def _diff_prep_kernel(q_ref, k_ref, cos_ref, sin_ref, qo_ref, ko_ref):
    cos, sin = cos_ref[...], sin_ref[...]
    scale = 1.0 / math.sqrt(HD)
    for h in range(2 * DIFF_HEADS):
        y = _rope(q_ref[:, h * HD:(h + 1) * HD], cos, sin)
        qo_ref[:, h * HD:(h + 1) * HD] = (y * scale).astype(qo_ref.dtype)
    for h in range(2 * DIFF_HEADS):
        y = _rope(k_ref[:, h * HD:(h + 1) * HD], cos, sin)
        ko_ref[:, h * HD:(h + 1) * HD] = y.astype(ko_ref.dtype)


def _diff_prep(st, qkv, tables):
    tm = PREP_ROWS
    w = 2 * DIFF_HEADS * HD
    tab = pl.BlockSpec((tm, HD), lambda i: (_table_block(st, i), 0))
    return pl.pallas_call(
        _diff_prep_kernel, grid=(st.n // tm,),
        in_specs=[pl.BlockSpec((tm, w), lambda i: (i, 0)),
                  pl.BlockSpec((tm, w), lambda i: (i, 1)), tab, tab],
        out_specs=[pl.BlockSpec((tm, w), lambda i: (i, 0))] * 2,
        out_shape=[jax.ShapeDtypeStruct((st.n, w), BF16)] * 2,
        compiler_params=_cparams(("parallel",)), name="diff_prep",
    )(qkv, qkv, *tables)


def _diff_attn_kernel(lam_ref, q_ref, k_ref, v_ref, g_ref, *rest, cache, out_scale, aliased):
    if aliased:
        rest = rest[:-2] + rest[-1:]
    if cache:
        ck_ref, cv_ref, o_ref = rest
        vs = [cv_ref[...].astype(BF16), v_ref[...].astype(BF16)]
    else:
        (o_ref,) = rest
        vs = [v_ref[...].astype(BF16)]
    lam = lam_ref[0]
    outs = []
    for m in range(2):
        q = q_ref[:, m * HD:(m + 1) * HD]
        ks = [k_ref[:, m * HD:(m + 1) * HD]]
        if cache:
            ks = [ck_ref[:, m * HD:(m + 1) * HD].astype(BF16)] + ks
        ps, l = _softmax_parts(q, ks)
        o = jnp.dot(ps[0].astype(BF16), vs[0], preferred_element_type=F32)
        for p, v in zip(ps[1:], vs[1:]):
            o = o + jnp.dot(p.astype(BF16), v, preferred_element_type=F32)
        outs.append(o / l)
    o = outs[0] - lam * outs[1]
    o = o * lax.rsqrt(jnp.mean(o * o, axis=-1, keepdims=True) + EPS) * g_ref[...]
    o_ref[...] = (o * out_scale).astype(o_ref.dtype)


def _diff_attn(st, lam, q, k, qkv, g_sub, lam_init, row0, nb, seq, cache_k, cache_v, layer_j, prev):
    tq = min(512, seq)
    nq = seq // tq
    hw = 2 * HD
    b0 = row0 // seq
    q0 = row0 // tq
    v_col0 = 2 * (2 * DIFF_HEADS * HD) // hw
    in_specs = [
        pl.BlockSpec(memory_space=pltpu.SMEM),
        pl.BlockSpec((tq, hw), lambda b, h, i: (q0 + b * nq + i, h)),
        pl.BlockSpec((seq, hw), lambda b, h, i: (b + b0, h)),
        pl.BlockSpec((seq, hw), lambda b, h, i: (b + b0, v_col0 + h)),
        pl.BlockSpec((1, hw), lambda b, h, i: (0, 0)),
    ]
    args = [lam, q, k, qkv, g_sub.reshape(1, hw)]
    cache = cache_k is not None
    if cache:
        past = cache_k.shape[2]
        ck = cache_k.reshape(cache_k.shape[0], cache_k.shape[1], past, 2 * DIFF_HEADS * HD)
        cv = cache_v.reshape(ck.shape)
        in_specs += [pl.BlockSpec((None, None, past, hw), lambda b, h, i: (b, layer_j, 0, h))] * 2
        args += [ck, cv]
    aliases = {}
    if prev is not None:
        in_specs.append(pl.BlockSpec(memory_space=pl.ANY))
        args.append(prev)
        aliases = {len(args) - 1: 0}
    return pl.pallas_call(
        functools.partial(_diff_attn_kernel, cache=cache, out_scale=1.0 - lam_init,
                          aliased=prev is not None),
        grid=(nb, DIFF_HEADS, nq), in_specs=in_specs,
        out_specs=pl.BlockSpec((tq, hw), lambda b, h, i: (q0 + b * nq + i, h)),
        out_shape=jax.ShapeDtypeStruct((st.n, DIFF_HEADS * hw), BF16),
        input_output_aliases=aliases,
        compiler_params=_cparams(("parallel", "parallel", "parallel")), name="diff_attn",
    )(*args)


def _conv_kernel(x_ref, w_ref, b_ref, *rest):
    o_ref, pad_ref = rest[-2:]
    seq = x_ref.shape[0]
    half = SSD_CONV // 2
    zeros = jnp.zeros((8, x_ref.shape[1]), F32)
    pad_ref[0:8, :] = zeros
    pad_ref[seq + 8:seq + 16, :] = zeros
    pad_ref[8:seq + 8, :] = x_ref[...]
    xp = pad_ref[...]
    rows = seq + 16
    acc = jnp.broadcast_to(b_ref[...], x_ref.shape)
    for k in range(SSD_CONV):
        shifted = xp if k == half else pltpu.roll(xp, (half - k) % rows, 0)
        acc = acc + w_ref[k:k + 1, :] * shifted[8:8 + seq, :]
    o_ref[...] = _silu(acc)


def _ssd_conv(st, xbc, conv_w, conv_b, row0, nb, seq, prev):
    tc = 512
    b0 = row0 // seq
    in_specs = [pl.BlockSpec((seq, tc), lambda b, c: (b + b0, c)),
                pl.BlockSpec((SSD_CONV, tc), lambda b, c: (0, c)),
                pl.BlockSpec((1, tc), lambda b, c: (0, c))]
    args = [xbc, conv_w, conv_b.reshape(1, SSD_CONV_DIM)]
    aliases = {}
    if prev is not None:
        in_specs.append(pl.BlockSpec(memory_space=pl.ANY))
        args.append(prev)
        aliases = {len(args) - 1: 0}
    return pl.pallas_call(
        _conv_kernel, grid=(nb, SSD_CONV_DIM // tc),
        in_specs=in_specs,
        out_specs=pl.BlockSpec((seq, tc), lambda b, c: (b + b0, c)),
        out_shape=jax.ShapeDtypeStruct((st.n, SSD_CONV_DIM), F32),
        scratch_shapes=[pltpu.VMEM((seq + 16, tc), F32)],
        input_output_aliases=aliases,
        compiler_params=_cparams(("parallel", "parallel")), name="ssd_conv",
    )(*args)


def _ssd_scan_kernel(x_ref, b_ref, c_ref, dt_ref, bias_ref, alog_ref, *rest,
                     rev, nc, has_init, want_final, aliased):
    pos = 0
    h0_ref = None
    if has_init:
        h0_ref = rest[pos]; pos += 1
    if aliased:
        pos += 1
    y_ref = rest[pos]; pos += 1
    hf_ref = None
    if want_final:
        hf_ref = rest[pos]; pos += 1
    state_ref, acol_ref, arow_ref, dtrow_ref, wrow_ref, dec_ref = rest[pos:]
    c = pl.program_id(1)
    q = SSD_CHUNK
    lane0 = SSD_HEADS if rev else 0
    r_heads = SSD_HEADS // SSD_GROUPS
    gw = r_heads * SSD_HEADDIM

    ii = lax.broadcasted_iota(jnp.int32, (q, q), 0)
    jj = lax.broadcasted_iota(jnp.int32, (q, q), 1)
    causal = (jj >= ii) if rev else (jj <= ii)

    @pl.when(c == 0)
    def _():
        if has_init:
            for blk in range(SSD_INNER // 128):
                state_ref[:, blk * 128:(blk + 1) * 128] = h0_ref[blk * 128:(blk + 1) * 128, :].T
        else:
            state_ref[...] = jnp.zeros(state_ref.shape, F32)

    xdt = dt_ref[...] + bias_ref[...]
    dt = jnp.maximum(xdt, 0.0) + jnp.log(1.0 + jnp.exp(-jnp.abs(xdt)))
    la = dt * (-jnp.exp(alog_ref[...]))
    tri = causal.astype(BF16)
    tri_t = ((ii >= jj) if rev else (ii <= jj)).astype(BF16)
    la3 = _split3(la)
    acol_ref[...] = (jnp.dot(tri, la3[0], preferred_element_type=F32)
                     + jnp.dot(tri, la3[1], preferred_element_type=F32)
                     + jnp.dot(tri, la3[2], preferred_element_type=F32))
    arow = _dot3(_split3(la.T), tri_t)
    end = 0 if rev else q - 1
    a_end = jnp.broadcast_to(arow[:, end:end + 1], arow.shape)
    dt_t = dt.T
    arow_ref[...] = arow
    dtrow_ref[...] = dt_t
    wrow_ref[...] = jnp.exp(a_end - arow) * dt_t
    dec_ref[...] = jnp.exp(a_end)

    lane_lo = lax.broadcasted_iota(jnp.int32, (1, 128), 1) < SSD_HEADDIM
    zeros_blk = jnp.zeros((SSD_STATE, q), BF16)

    for g in range(SSD_GROUPS):
        bmat = b_ref[:, g * SSD_STATE:(g + 1) * SSD_STATE]
        cmat = c_ref[:, g * SSD_STATE:(g + 1) * SSD_STATE]
        cb = lax.dot_general(cmat.astype(BF16), bmat.astype(BF16), (((1,), (1,)), ((), ())),
                             preferred_element_type=F32)
        b_t = bmat.T
        for pair in range(r_heads // 2):
            col = g * gw + pair * 128
            x_pair = x_ref[:, col:col + 128].astype(BF16)
            st_old = state_ref[:, col:col + 128]
            rhs = jnp.concatenate([x_pair, st_old.astype(BF16)], axis=0)
            res = []
            for sub in range(2):
                hd = lane0 + g * r_heads + 2 * pair + sub
                a_i = jnp.broadcast_to(acol_ref[:, hd:hd + 1], (q, q))
                seg = a_i - arow_ref[hd:hd + 1, :]
                decay = jnp.exp(jnp.where(causal, seg, -jnp.inf))
                m_intra = (cb * decay * dtrow_ref[hd:hd + 1, :]).astype(BF16)
                c_scaled = (cmat * jnp.exp(a_i)).astype(BF16)
                bw_t = (b_t * wrow_ref[hd:hd + 1, :]).astype(BF16)
                lhs = jnp.concatenate(
                    [jnp.concatenate([m_intra, c_scaled], axis=1),
                     jnp.concatenate([bw_t, zeros_blk], axis=1)], axis=0)
                res.append(jnp.dot(lhs, rhs, preferred_element_type=F32))
            hd0 = lane0 + g * r_heads + 2 * pair
            y_ref[:, col:col + 128] = jnp.where(lane_lo, res[0][:q], res[1][:q])
            dec_pair = jnp.where(lane_lo, dec_ref[hd0:hd0 + 1, :], dec_ref[hd0 + 1:hd0 + 2, :])
            upd = jnp.where(lane_lo, res[0][q:], res[1][q:])
            state_ref[:, col:col + 128] = dec_pair * st_old + upd

    if want_final:
        @pl.when(c == nc - 1)
        def _():
            for blk in range(SSD_INNER // 128):
                hf_ref[blk * 128:(blk + 1) * 128, :] = state_ref[:, blk * 128:(blk + 1) * 128].T


def _ssd_scan(st, xbc, dt, dt_bias, a_log, row0, nb, seq, rev, h0, layer_j, want_final, prev):
    nc = seq // SSD_CHUNK
    c0 = row0 // SSD_CHUNK

    def chunk(b, c):
        return c0 + b * nc + (nc - 1 - c if rev else c)

    in_specs = [
        pl.BlockSpec((SSD_CHUNK, SSD_INNER), lambda b, c: (chunk(b, c), 0)),
        pl.BlockSpec((SSD_CHUNK, SSD_GN), lambda b, c: (chunk(b, c), SSD_INNER // SSD_GN)),
        pl.BlockSpec((SSD_CHUNK, SSD_GN), lambda b, c: (chunk(b, c), SSD_INNER // SSD_GN + 1)),
        pl.BlockSpec((SSD_CHUNK, 128), lambda b, c: (chunk(b, c), 0)),
        pl.BlockSpec((1, 128), lambda b, c: (0, 0)),
        pl.BlockSpec((1, 128), lambda b, c: (0, 0)),
    ]
    args = [xbc, xbc, xbc, dt, dt_bias.reshape(1, 128), a_log.reshape(1, 128)]
    if h0 is not None:
        h0r = h0.reshape(h0.shape[0], h0.shape[1], SSD_INNER, SSD_STATE)
        in_specs.append(pl.BlockSpec((None, None, SSD_INNER, SSD_STATE),
                                     lambda b, c: (b, layer_j, 0, 0)))
        args.append(h0r)
    aliases = {}
    if prev is not None:
        in_specs.append(pl.BlockSpec(memory_space=pl.ANY))
        args.append(prev)
        aliases = {len(args) - 1: 0}
    out_shape = [jax.ShapeDtypeStruct((st.n, SSD_INNER), F32)]
    out_specs = [pl.BlockSpec((SSD_CHUNK, SSD_INNER), lambda b, c: (chunk(b, c), 0))]
    if want_final:
        out_shape.append(jax.ShapeDtypeStruct((nb, SSD_INNER, SSD_STATE), F32))
        out_specs.append(pl.BlockSpec((None, SSD_INNER, SSD_STATE), lambda b, c: (b, 0, 0)))
    res = pl.pallas_call(
        functools.partial(_ssd_scan_kernel, rev=rev, nc=nc, has_init=h0 is not None,
                          want_final=want_final, aliased=prev is not None),
        grid=(nb, nc), in_specs=in_specs, out_specs=out_specs, out_shape=out_shape,
        scratch_shapes=[pltpu.VMEM((SSD_STATE, SSD_INNER), F32),
                        pltpu.VMEM((SSD_CHUNK, 128), F32)] + [pltpu.VMEM((128, SSD_CHUNK), F32)] * 4,
        input_output_aliases=aliases,
        compiler_params=_cparams(("parallel", "arbitrary")), name="ssd_scan",
    )(*args)
    return res


def _ssd_gate_kernel(yf_ref, yb_ref, x_ref, z_ref, d_ref, g_ref, o_ref):
    gw = SSD_INNER // SSD_GROUPS
    for grp in range(SSD_GROUPS):
        sl = slice(grp * gw, (grp + 1) * gw)
        y = yf_ref[:, sl] + yb_ref[:, sl] + d_ref[:, sl] * x_ref[:, sl]
        gated = y * _silu(z_ref[:, sl])
        normed = gated * lax.rsqrt(jnp.mean(gated * gated, axis=-1, keepdims=True) + EPS)
        o_ref[:, sl] = (normed * g_ref[:, sl]).astype(o_ref.dtype)


def _ssd_gate(yf, yb, xbc, z, d_sum, g_norm):
    n = yf.shape[0]
    tm = 256
    row = pl.BlockSpec((tm, SSD_INNER), lambda i: (i, 0))
    vec = pl.BlockSpec((1, SSD_INNER), lambda i: (0, 0))
    return pl.pallas_call(
        _ssd_gate_kernel, grid=(n // tm,),
        in_specs=[row, row, row, row, vec, vec], out_specs=row,
        out_shape=jax.ShapeDtypeStruct((n, SSD_INNER), BF16),
        compiler_params=_cparams(("parallel",)), name="ssd_gate",
    )(yf, yb, xbc, z, d_sum, g_norm.reshape(1, SSD_INNER))


def _moe_up_kernel(be_ref, nu_ref, a_ref, wg_ref, wu_ref, *rest, blk0):
    o_ref = rest[-1]
    m = pl.program_id(1) + blk0

    @pl.when(m < nu_ref[0])
    def _():
        a = a_ref[...]
        gate = jnp.dot(a, wg_ref[...].astype(BF16), preferred_element_type=F32)
        up = jnp.dot(a, wu_ref[...].astype(BF16), preferred_element_type=F32)
        o_ref[...] = (_silu(gate) * up).astype(o_ref.dtype)

    @pl.when(m >= nu_ref[0])
    def _():
        o_ref[...] = jnp.zeros(o_ref.shape, o_ref.dtype)


def _moe_down_kernel(be_ref, nu_ref, a_ref, w_ref, o_ref):
    m = pl.program_id(0)

    @pl.when(m < nu_ref[0])
    def _():
        o_ref[...] = jnp.dot(a_ref[...], w_ref[...].astype(BF16),
                             preferred_element_type=F32).astype(o_ref.dtype)

    @pl.when(m >= nu_ref[0])
    def _():
        o_ref[...] = jnp.zeros(o_ref.shape, o_ref.dtype)


def _moe(st, h, logits, w_gu, w_down, f):
    n = st.n
    tb = MOE_BLOCK
    n_assign = n * TOP_K
    n_blocks = n_assign // tb + N_EXPERTS
    slots = n_blocks * tb
    top_logit, top_e = lax.top_k(logits, TOP_K)
    gates = jax.nn.softmax(top_logit, axis=-1)
    flat_e = top_e.reshape(-1).astype(jnp.int32)
    onehot = (flat_e[:, None] == jnp.arange(N_EXPERTS, dtype=jnp.int32)[None, :]).astype(jnp.int32)
    rank = jnp.sum((jnp.cumsum(onehot, axis=0) - onehot) * onehot, axis=1)
    counts = jnp.sum(onehot, axis=0)
    padded = (counts + tb - 1) // tb * tb
    pad_end = jnp.cumsum(padded)
    pad_start = pad_end - padded
    dest = pad_start[flat_e] + rank
    slot_tok = jnp.zeros((slots,), jnp.int32).at[dest].set(
        jnp.arange(n_assign, dtype=jnp.int32) // TOP_K, unique_indices=True, mode="promise_in_bounds")
    block_start = jnp.arange(n_blocks, dtype=jnp.int32) * tb
    block_e = jnp.minimum(jnp.searchsorted(pad_end, block_start, side="right"),
                          N_EXPERTS - 1).astype(jnp.int32)
    n_used = (pad_end[-1] // tb).astype(jnp.int32).reshape(1)

    tn = 512
    nf = EXPERT_DIM // tn

    def live_block(m, nu):
        return jnp.minimum(m, nu[0] - 1)

    half = n_blocks // 2
    act = None
    for blk0, nblk in ((0, half), (half, n_blocks - half)):
        xs = h.at[slot_tok[blk0 * tb:(blk0 + nblk) * tb]].get(mode="promise_in_bounds")

        def local(m, nu, blk0=blk0, nblk=nblk):
            return jnp.clip(live_block(m + blk0, nu) - blk0, 0, nblk - 1)

        def expert(m, be, nu, blk0=blk0):
            return be[live_block(m + blk0, nu)]

        in_specs = [
            pl.BlockSpec((tb, D_MODEL), lambda j, m, be, nu, local=local: (local(m, nu), 0)),
            pl.BlockSpec((None, None, D_MODEL, tn),
                         lambda j, m, be, nu, expert=expert: (f, expert(m, be, nu), 0, j)),
            pl.BlockSpec((None, None, D_MODEL, tn),
                         lambda j, m, be, nu, expert=expert: (f, expert(m, be, nu), 0, j + nf)),
        ]
        args = [block_e, n_used, xs, w_gu, w_gu]
        aliases = {}
        if act is not None:
            in_specs.append(pl.BlockSpec(memory_space=pl.ANY))
            args.append(act)
            aliases = {len(args) - 1: 0}
        act = pl.pallas_call(
            functools.partial(_moe_up_kernel, blk0=blk0),
            grid_spec=pltpu.PrefetchScalarGridSpec(
                num_scalar_prefetch=2, grid=(nf, nblk), in_specs=in_specs,
                out_specs=pl.BlockSpec((tb, tn), lambda j, m, be, nu, blk0=blk0: (m + blk0, j)),
            ),
            out_shape=jax.ShapeDtypeStruct((slots, EXPERT_DIM), BF16),
            input_output_aliases=aliases,
            compiler_params=_cparams(("parallel", "arbitrary")), name="moe_up",
        )(*args)

    tn2 = 256
    yb = pl.pallas_call(
        _moe_down_kernel,
        grid_spec=pltpu.PrefetchScalarGridSpec(
            num_scalar_prefetch=2, grid=(n_blocks, D_MODEL // tn2),
            in_specs=[
                pl.BlockSpec((tb, EXPERT_DIM), lambda m, j, be, nu: (live_block(m, nu), 0)),
                pl.BlockSpec((None, None, EXPERT_DIM, tn2),
                             lambda m, j, be, nu: (f, be[live_block(m, nu)], 0, j)),
            ],
            out_specs=pl.BlockSpec((tb, tn2), lambda m, j, be, nu: (m, j)),
        ),
        out_shape=jax.ShapeDtypeStruct((slots, D_MODEL), BF16),
        compiler_params=_cparams(("arbitrary", "arbitrary")), name="moe_down",
    )(block_e, n_used, act, w_down)

    dest2 = dest.reshape(n, TOP_K)
    y0 = yb.at[dest2[:, 0]].get(mode="promise_in_bounds")
    y1 = yb.at[dest2[:, 1]].get(mode="promise_in_bounds")
    return y0, y1, gates


def _combine_kernel(x_ref, y0_ref, y1_ref, g_ref, gate_ref, *rest, norm):
    y = g_ref[:, 0:1] * y0_ref[...].astype(F32) + g_ref[:, 1:2] * y1_ref[...].astype(F32)
    xn = x_ref[...] + gate_ref[...] * y
    if norm:
        gn_ref, shift_ref, scale_ref, o_ref, h_ref = rest
        _norm_mod_store(xn, gn_ref, shift_ref, scale_ref, h_ref)
    else:
        (o_ref,) = rest
    o_ref[...] = xn


def _combine(st, x, y0, y1, gates, mods, layer, g_next=None):
    tm = st.tile(512)
    row = pl.BlockSpec((tm, D_MODEL), lambda i: (i, 0))

    def mod(lyr, which):
        return pl.BlockSpec((None, 1, D_MODEL),
                            lambda i: (_mod_index(lyr, which, st.mod_row(i, tm)), 0, 0))

    in_specs = [row, row, row, pl.BlockSpec((tm, TOP_K), lambda i: (i, 0)), mod(layer, 5)]
    args = [x, y0, y1, gates, mods]
    out_shape = [jax.ShapeDtypeStruct((st.n, D_MODEL), F32)]
    out_specs = [row]
    if g_next is not None:
        in_specs += [pl.BlockSpec((1, D_MODEL), lambda i: (0, 0)), mod(layer + 1, 0), mod(layer + 1, 1)]
        args += [g_next.reshape(1, D_MODEL), mods, mods]
        out_shape.append(jax.ShapeDtypeStruct((st.n, D_MODEL), BF16))
        out_specs.append(row)
    res = pl.pallas_call(
        functools.partial(_combine_kernel, norm=g_next is not None), grid=(st.n // tm,),
        in_specs=in_specs, out_specs=out_specs, out_shape=out_shape,
        compiler_params=_cparams(("parallel",)), name="moe_combine",
    )(*args)
    return res if g_next is not None else (res[0], None)


def kernel(x_prompt, x_sample, cache_attn_k, cache_attn_v, cache_diff_k, cache_diff_v, state_ssd_fwd, state_ssd_bwd, c, c_ctx, w_ada, b_ada, g_mix, g_ffn, g_final, attn_w_in, attn_g_q, attn_g_k, attn_w_out, diff_w_in, diff_lambda, diff_g_sub, diff_w_out, ssd_w_in, ssd_conv_w, ssd_conv_b, ssd_dt_bias, ssd_a_log, ssd_d, ssd_g_norm, ssd_w_out, ffn_w_gu, ffn_w_down, moe_w_router, moe_w_gu, moe_w_down):
    pb, pl_len, _ = x_prompt.shape
    sb, sl_len, _ = x_sample.shape
    st = _Stream(pb, pl_len, sb, sl_len)
    np_, ns = st.n_prompt, st.n_sample
    tm = st.tile(1024)

    x = jnp.concatenate([x_prompt.reshape(np_, D_MODEL), x_sample.reshape(ns, D_MODEL)], axis=0)

    cond = jnp.zeros((MOD_ROWS, D_MODEL), F32).at[:sb].set(c).at[CTX_ROW].set(c_ctx)
    cond_act = jax.nn.silu(cond).astype(BF16)
    mods = []
    for i in range(DEPTH):
        m = _matmul(cond_act, w_ada, (i,), n_off=0, n_out=6 * D_MODEL, tm=MOD_ROWS, tn=1024,
                    tk=D_MODEL, out_dtype=F32, name="ada")
        mods.append(m + b_ada[i][None, :])
    mods = jnp.stack(mods).reshape(DEPTH * MOD_ROWS * 6, 1, D_MODEL)

    tables = _rope_tables(sl_len, PREP_ROWS)

    tm2 = st.tile(2048)

    def gate_index(layer, which, tile):
        return lambda i: _mod_index(layer, which, st.mod_row(i, tile))

    new_attn_k, new_attn_v, new_diff_k, new_diff_v, new_ssd_f, new_ssd_b = [], [], [], [], [], []
    def out_proj(o, w16, x, layer, fuse_norm):
        if fuse_norm:
            return _out_proj_norm(st, o, w16, x, mods, layer, g_ffn[layer])
        xo = _matmul(o, w16, (), n_off=0, n_out=D_MODEL, tm=tm2, tn=512, tk=D_MODEL,
                     out_dtype=F32, epilogue="resid", resid=x, mods=mods,
                     gate_index=gate_index(layer, 2, tm2), name="attn_out")
        return xo, None

    h_pre = None
    for i in range(DEPTH):
        kind, j = i % 3, i // 3
        f = i // 2
        w_router = moe_w_router[f] if i % 2 == 1 else None
        h = h_pre if h_pre is not None else _norm_mod(st, x, g_mix[i], mods, i, 0)
        h_pre = hf = logits = None
        if kind == 0:
            qkv = _matmul(h, attn_w_in, (j,), n_off=0, n_out=attn_w_in.shape[-1], tm=tm, tn=1024,
                          tk=D_MODEL, out_dtype=F32, name="attn_in")
            q, k = _gqa_prep(st, qkv, attn_g_q[j], attn_g_k[j], tables)
            o = _gqa_attn(st, q, k, qkv, 0, pb, pl_len, None, None, j, None)
            o = _gqa_attn(st, q, k, qkv, np_, sb, sl_len, cache_attn_k, cache_attn_v, j, o)
            new_attn_k.append(k[:np_].reshape(pb, pl_len, ATTN_KV_HEADS, HD))
            new_attn_v.append(qkv[:np_, (ATTN_HEADS + ATTN_KV_HEADS) * HD:]
                              .reshape(pb, pl_len, ATTN_KV_HEADS, HD))
            x, hf = out_proj(o, attn_w_out[j].astype(BF16), x, i, w_router is None)
        elif kind == 1:
            lam_init = 0.8 - 0.6 * math.exp(-0.3 * i)
            lf = diff_lambda[j]
            lam = (jnp.exp(jnp.sum(lf[0] * lf[1])) - jnp.exp(jnp.sum(lf[2] * lf[3])) + lam_init).reshape(1)
            qkv = _matmul(h, diff_w_in, (j,), n_off=0, n_out=diff_w_in.shape[-1], tm=tm, tn=1024,
                          tk=D_MODEL, out_dtype=F32, name="diff_in")
            w = 2 * DIFF_HEADS * HD
            q, k = _diff_prep(st, qkv, tables)
            o = _diff_attn(st, lam, q, k, qkv, diff_g_sub[j], lam_init, 0, pb, pl_len, None, None, j, None)
            o = _diff_attn(st, lam, q, k, qkv, diff_g_sub[j], lam_init, np_, sb, sl_len,
                           cache_diff_k, cache_diff_v, j, o)
            new_diff_k.append(qkv[:np_, w:2 * w].reshape(pb, pl_len, 2 * DIFF_HEADS, HD))
            new_diff_v.append(qkv[:np_, 2 * w:].reshape(pb, pl_len, DIFF_HEADS, 2 * HD))
            x, hf = out_proj(o, diff_w_out[j].astype(BF16), x, i, w_router is None)
        else:
            z = _matmul(h, ssd_w_in, (j,), n_off=0, n_out=SSD_INNER, tm=tm, tn=1024, tk=D_MODEL,
                        out_dtype=F32, name="ssd_in_z")
            xbc = _matmul(h, ssd_w_in, (j,), n_off=SSD_INNER, n_out=SSD_CONV_DIM, tm=tm, tn=1024,
                          tk=D_MODEL, out_dtype=F32, name="ssd_in_xbc")
            dt = _matmul(h, ssd_w_in, (j,), n_off=SSD_INNER + SSD_CONV_DIM, n_out=2 * SSD_HEADS,
                         tm=tm, tn=128, tk=D_MODEL, out_dtype=F32, name="ssd_in_dt")
            xc = _ssd_conv(st, xbc, ssd_conv_w[j], ssd_conv_b[j], 0, pb, pl_len, None)
            xc = _ssd_conv(st, xbc, ssd_conv_w[j], ssd_conv_b[j], np_, sb, sl_len, xc)
            ys = []
            for rev, h0 in ((False, state_ssd_fwd), (True, state_ssd_bwd)):
                y, hfin = _ssd_scan(st, xc, dt, ssd_dt_bias[j], ssd_a_log[j], 0, pb, pl_len, rev,
                                    None, j, True, None)
                (y,) = _ssd_scan(st, xc, dt, ssd_dt_bias[j], ssd_a_log[j], np_, sb, sl_len, rev,
                                 h0, j, False, y)
                ys.append(y)
                (new_ssd_b if rev else new_ssd_f).append(
                    hfin.reshape(pb, SSD_HEADS, SSD_HEADDIM, SSD_STATE))
            d_sum = jnp.repeat(ssd_d[j][0] + ssd_d[j][1], SSD_HEADDIM).reshape(1, SSD_INNER)
            o = _ssd_gate(ys[0], ys[1], xc, z, d_sum, ssd_g_norm[j])
            x = _matmul(o, ssd_w_out[j].astype(BF16), (), n_off=0, n_out=D_MODEL, tm=tm, tn=512, tk=SSD_INNER,
                        out_dtype=F32, epilogue="resid", resid=x, mods=mods,
                        gate_index=gate_index(i, 2, tm), name="ssd_out")

        if hf is None:
            if w_router is not None:
                hf, logits = _norm_mod(st, x, g_ffn[i], mods, i, 3, w_router=w_router)
            else:
                hf = _norm_mod(st, x, g_ffn[i], mods, i, 3)
        if i % 2 == 0:
            act = _matmul(hf, ffn_w_gu, (f,), n_off=0, n_out=FFN_DIM, tm=tm, tn=512, tk=D_MODEL,
                          out_dtype=BF16, epilogue="swiglu", up_off=FFN_DIM, name="ffn_up")
            x = _matmul(act, ffn_w_down[f].astype(BF16), (), n_off=0, n_out=D_MODEL, tm=tm, tn=512, tk=FFN_DIM,
                        out_dtype=F32, epilogue="resid", resid=x, mods=mods,
                        gate_index=gate_index(i, 5, tm), name="ffn_down")
        else:
            y0, y1, gates = _moe(st, hf, logits[:, :N_EXPERTS], moe_w_gu, moe_w_down, f)
            x, h_pre = _combine(st, x, y0, y1, gates, mods, i,
                                g_next=g_mix[i + 1] if i + 1 < DEPTH else None)

    y_prompt = _final_norm(x, g_final, 0, np_).reshape(pb, pl_len, D_MODEL)
    y_sample = _final_norm(x, g_final, np_, ns).reshape(sb, sl_len, D_MODEL)
    return (y_prompt, y_sample,
            jnp.stack(new_attn_k, axis=1), jnp.stack(new_attn_v, axis=1),
            jnp.stack(new_diff_k, axis=1), jnp.stack(new_diff_v, axis=1),
            jnp.stack(new_ssd_f, axis=1), jnp.stack(new_ssd_b, axis=1))
```

```python
import functools
import math

import jax
import jax.numpy as jnp
from jax import lax
from jax.experimental import pallas as pl
from jax.experimental.pallas import tpu as pltpu

F32 = jnp.float32
BF16 = jnp.bfloat16

D_MODEL = 2048
DEPTH = 4
GRID_W = 64
ROPE_THETA = 10000.0
EPS = 1e-6
HD = 128
Q_SCALE = math.log2(math.e) / math.sqrt(HD)
ATTN_HEADS = 16
ATTN_KV_HEADS = 4
DIFF_HEADS = 8
SSD_INNER = 2 * D_MODEL
SSD_HEADDIM = 64
SSD_HEADS = SSD_INNER // SSD_HEADDIM
SSD_GROUPS = 8
SSD_STATE = 128
SSD_CONV = 5
SSD_CHUNK = 128
SSD_GN = SSD_GROUPS * SSD_STATE
SSD_CONV_DIM = SSD_INNER + 2 * SSD_GN
FFN_DIM = 5632
N_EXPERTS = 8
TOP_K = 2
EXPERT_DIM = 7168

MOD_ROWS = 16
CTX_ROW = 8
MOE_BLOCK = 1024
VMEM_LIMIT = 56 * 1024 * 1024


def _cparams(sem):
    return pltpu.CompilerParams(dimension_semantics=sem, vmem_limit_bytes=VMEM_LIMIT)


def _silu(x):
    return x / (1.0 + jnp.exp(-x))


def _split3(x):
    hi = x.astype(BF16)
    r = x - hi.astype(F32)
    mid = r.astype(BF16)
    lo = (r - mid.astype(F32)).astype(BF16)
    return hi, mid, lo


def _dot3(a3, b):
    out = jnp.dot(a3[0], b, preferred_element_type=F32)
    out = out + jnp.dot(a3[1], b, preferred_element_type=F32)
    return out + jnp.dot(a3[2], b, preferred_element_type=F32)


class _Stream:
    def __init__(self, n_prompt_batch, prompt_len, n_sample_batch, sample_len):
        self.pb, self.pl_, self.sb, self.sl = n_prompt_batch, prompt_len, n_sample_batch, sample_len
        self.n_prompt = n_prompt_batch * prompt_len
        self.n_sample = n_sample_batch * sample_len
        self.n = self.n_prompt + self.n_sample

    def mod_row(self, i, tm):
        pt = self.n_prompt // tm
        per = self.sl // tm
        return jnp.where(i < pt, CTX_ROW, (i - pt) // per)

    def tile(self, want):
        t = want
        while self.n_prompt % t or self.sl % t:
            t //= 2
        return t


def _mod_index(layer, which, row):
    return (layer * MOD_ROWS + row) * 6 + which


def _norm_mod_store(x, g_ref, shift_ref, scale_ref, o_ref, wr_ref=None, lg_ref=None):
    y = x * lax.rsqrt(jnp.mean(x * x, axis=-1, keepdims=True) + EPS) * g_ref[...]
    h = y * (1.0 + scale_ref[...]) + shift_ref[...]
    if wr_ref is not None:
        w = wr_ref[...]
        whi = w.astype(BF16)
        wlo = (w - whi.astype(F32)).astype(BF16)
        h3 = _split3(h)
        lg = _dot3(h3, whi)
        lg = lg + jnp.dot(h3[0], wlo, preferred_element_type=F32)
        lg = lg + jnp.dot(h3[1], wlo, preferred_element_type=F32)
        lg_ref[...] = lg
    o_ref[...] = h.astype(o_ref.dtype)


def _norm_mod_kernel(x_ref, g_ref, shift_ref, scale_ref, *rest, router):
    if router:
        wr_ref, o_ref, lg_ref = rest
        _norm_mod_store(x_ref[...], g_ref, shift_ref, scale_ref, o_ref, wr_ref, lg_ref)
    else:
        (o_ref,) = rest
        _norm_mod_store(x_ref[...], g_ref, shift_ref, scale_ref, o_ref)


def _router_pad(w_router):
    return jnp.zeros((D_MODEL, 128), F32).at[:, :N_EXPERTS].set(w_router)


def _out_norm_kernel(a_ref, w_ref, x_ref, gate_ref, g_ref, shift_ref, scale_ref, *rest, router):
    xn = x_ref[...] + gate_ref[...] * jnp.dot(a_ref[...], w_ref[...], preferred_element_type=F32)
    if router:
        wr_ref, xo_ref, ho_ref, lg_ref = rest
        _norm_mod_store(xn, g_ref, shift_ref, scale_ref, ho_ref, wr_ref, lg_ref)
    else:
        xo_ref, ho_ref = rest
        _norm_mod_store(xn, g_ref, shift_ref, scale_ref, ho_ref)
    xo_ref[...] = xn


def _out_proj_norm(st, a, w, x, mods, layer, g_next, w_router=None):
    tm = st.tile(512)
    kdim = a.shape[1]

    def mod(which):
        return pl.BlockSpec((None, 1, D_MODEL),
                            lambda i: (_mod_index(layer, which, st.mod_row(i, tm)), 0, 0))

    row = pl.BlockSpec((tm, D_MODEL), lambda i: (i, 0))
    in_specs = [pl.BlockSpec((tm, kdim), lambda i: (i, 0)),
                pl.BlockSpec((kdim, D_MODEL), lambda i: (0, 0), pipeline_mode=pl.Buffered(1)),
                row, mod(2), pl.BlockSpec((1, D_MODEL), lambda i: (0, 0)), mod(3), mod(4)]
    args = [a, w, x, mods, g_next.reshape(1, D_MODEL), mods, mods]
    out_shape = [jax.ShapeDtypeStruct((st.n, D_MODEL), F32), jax.ShapeDtypeStruct((st.n, D_MODEL), BF16)]
    out_specs = [row, row]
    if w_router is not None:
        in_specs.append(pl.BlockSpec((D_MODEL, 128), lambda i: (0, 0)))
        args.append(_router_pad(w_router))
        out_shape.append(jax.ShapeDtypeStruct((st.n, 128), F32))
        out_specs.append(pl.BlockSpec((tm, 128), lambda i: (i, 0)))
    return pl.pallas_call(
        functools.partial(_out_norm_kernel, router=w_router is not None),
        grid=(st.n // tm,), in_specs=in_specs, out_specs=out_specs, out_shape=out_shape,
        compiler_params=_cparams(("parallel",)), name="out_proj_norm",
    )(*args)


def _norm_mod(st, x, g, mods, layer, which_shift, w_router=None):
    tm = st.tile(512)
    nt = st.n // tm
    g2 = g.reshape(1, D_MODEL)
    in_specs = [
        pl.BlockSpec((tm, D_MODEL), lambda i: (i, 0)),
        pl.BlockSpec((1, D_MODEL), lambda i: (0, 0)),
        pl.BlockSpec((None, 1, D_MODEL),
                     lambda i: (_mod_index(layer, which_shift, st.mod_row(i, tm)), 0, 0)),
        pl.BlockSpec((None, 1, D_MODEL),
                     lambda i: (_mod_index(layer, which_shift + 1, st.mod_row(i, tm)), 0, 0)),
    ]
    args = [x, g2, mods, mods]
    out_shape = [jax.ShapeDtypeStruct((st.n, D_MODEL), BF16)]
    out_specs = [pl.BlockSpec((tm, D_MODEL), lambda i: (i, 0))]
    if w_router is not None:
        in_specs.append(pl.BlockSpec((D_MODEL, 128), lambda i: (0, 0)))
        args.append(_router_pad(w_router))
        out_shape.append(jax.ShapeDtypeStruct((st.n, 128), F32))
        out_specs.append(pl.BlockSpec((tm, 128), lambda i: (i, 0)))
    res = pl.pallas_call(
        functools.partial(_norm_mod_kernel, router=w_router is not None),
        grid=(nt,), in_specs=in_specs, out_specs=out_specs, out_shape=out_shape,
        compiler_params=_cparams(("parallel",)), name="norm_mod",
    )(*args)
    return res if w_router is not None else res[0]


def _final_norm_kernel(x_ref, g_ref, o_ref):
    x = x_ref[...]
    o_ref[...] = x * lax.rsqrt(jnp.mean(x * x, axis=-1, keepdims=True) + EPS) * g_ref[...]


def _final_norm(x, g, row0, nrows):
    tm = 512
    r0 = row0 // tm
    return pl.pallas_call(
        _final_norm_kernel, grid=(nrows // tm,),
        in_specs=[pl.BlockSpec((tm, D_MODEL), lambda i: (i + r0, 0)),
                  pl.BlockSpec((1, D_MODEL), lambda i: (0, 0))],
        out_specs=pl.BlockSpec((tm, D_MODEL), lambda i: (i, 0)),
        out_shape=jax.ShapeDtypeStruct((nrows, D_MODEL), F32),
        compiler_params=_cparams(("parallel",)), name="final_norm",
    )(x, g.reshape(1, D_MODEL))


def _mm_kernel(*refs, epilogue, scale):
    a_ref, w_ref = refs[0], refs[1]
    o_ref = refs[-1]
    a = a_ref[...]
    y = jnp.dot(a, w_ref[...].astype(BF16), preferred_element_type=F32)
    if epilogue == "swiglu":
        up = jnp.dot(a, refs[2][...].astype(BF16), preferred_element_type=F32)
        o_ref[...] = (_silu(y) * up).astype(o_ref.dtype)
    elif epilogue == "resid":
        x_ref, gate_ref = refs[2], refs[3]
        o_ref[...] = (x_ref[...] + gate_ref[...] * y).astype(o_ref.dtype)
    elif epilogue == "rope":
        cos, sin = refs[2][...], refs[3][...]
        for h in range(y.shape[1] // HD):
            yh = _rope(y[:, h * HD:(h + 1) * HD], cos, sin)
            o_ref[:, h * HD:(h + 1) * HD] = (yh * scale).astype(o_ref.dtype)
    else:
        o_ref[...] = y.astype(o_ref.dtype)


def _matmul(a, w, widx, *, n_off, n_out, tm, tn, out_dtype, epilogue="plain", up_off=None,
            resid=None, mods=None, gate_index=None, scale=1.0, tables=None, table_index=None,
            name="mm"):
    m, kdim = a.shape
    assert m % tm == 0 and n_out % tn == 0 and n_off % tn == 0
    nb = n_off // tn
    lead = (None,) * len(widx)
    in_specs = [
        pl.BlockSpec((tm, kdim), lambda i, j: (i, 0)),
        pl.BlockSpec(lead + (kdim, tn), lambda i, j: widx + (0, j + nb)),
    ]
    args = [a, w]
    if epilogue == "swiglu":
        ub = up_off // tn
        in_specs.append(pl.BlockSpec(lead + (kdim, tn), lambda i, j: widx + (0, j + ub)))
        args.append(w)
    if epilogue == "resid":
        in_specs.append(pl.BlockSpec((tm, tn), lambda i, j: (i, j)))
        in_specs.append(pl.BlockSpec((None, 1, tn), lambda i, j: (gate_index(i), 0, j)))
        args += [resid, mods]
    if epilogue == "rope":
        in_specs += [pl.BlockSpec((tm, HD), lambda i, j: (table_index(i), 0))] * 2
        args += list(tables)
    return pl.pallas_call(
        functools.partial(_mm_kernel, epilogue=epilogue, scale=scale),
        grid=(m // tm, n_out // tn),
        in_specs=in_specs,
        out_specs=pl.BlockSpec((tm, tn), lambda i, j: (i, j)),
        out_shape=jax.ShapeDtypeStruct((m, n_out), out_dtype),
        compiler_params=_cparams(("parallel", "parallel")), name=name,
    )(*args)


def _rope_tables(length, ident_rows):
    nf = HD // 4
    inv = ROPE_THETA ** (-jnp.arange(nf, dtype=F32) / nf)
    rows = length // GRID_W
    row = jnp.repeat(jnp.arange(rows, dtype=F32), GRID_W)
    col = jnp.tile(jnp.arange(GRID_W, dtype=F32), rows)
    ar = row[:, None] * inv
    ac = col[:, None] * inv
    ang = jnp.concatenate([ar, ar, ac, ac], axis=-1)
    cos, sin = jnp.cos(ang), jnp.sin(ang)
    lane = jnp.arange(HD) % (HD // 2)
    sin_signed = jnp.where(lane >= HD // 4, sin, -sin)
    ones = jnp.ones((ident_rows, HD), F32)
    zeros = jnp.zeros((ident_rows, HD), F32)
    return jnp.concatenate([cos, ones]), jnp.concatenate([sin_signed, zeros])


def _table_block(st, i, tm):
    pt = st.n_prompt // tm
    per = st.sl // tm
    return jnp.where(i < pt, per, (i - pt) % per)


def _rope(y, cos, sin_signed):
    partner = lax.broadcasted_iota(jnp.int32, y.shape, 1) ^ (HD // 4)
    return y * cos + jnp.take_along_axis(y, partner, axis=1) * sin_signed


GQA_PREP_ROWS = 256


def _gqa_prep_kernel(q_ref, k_ref, gq_ref, gk_ref, cos_ref, sin_ref, qo_ref, ko_ref):
    cos, sin = cos_ref[...], sin_ref[...]
    scale = Q_SCALE

    def norm(x, g):
        return x * lax.rsqrt(jnp.mean(x * x, axis=-1, keepdims=True) + EPS) * g

    for h in range(ATTN_HEADS):
        y = _rope(norm(q_ref[:, h * HD:(h + 1) * HD], gq_ref[...]), cos, sin)
        qo_ref[:, h * HD:(h + 1) * HD] = (y * scale).astype(qo_ref.dtype)
    for h in range(ATTN_KV_HEADS):
        ko_ref[:, h * HD:(h + 1) * HD] = _rope(norm(k_ref[:, h * HD:(h + 1) * HD], gk_ref[...]),
                                               cos, sin)


def _gqa_prep(st, qkv, g_q, g_k, tables):
    tm = GQA_PREP_ROWS
    qw = ATTN_HEADS * HD
    kw = ATTN_KV_HEADS * HD
    tab = pl.BlockSpec((tm, HD), lambda i: (_table_block(st, i, tm), 0))
    return pl.pallas_call(
        _gqa_prep_kernel, grid=(st.n // tm,),
        in_specs=[pl.BlockSpec((tm, qw), lambda i: (i, 0)),
                  pl.BlockSpec((tm, kw), lambda i: (i, qw // kw)),
                  pl.BlockSpec((1, HD), lambda i: (0, 0)),
                  pl.BlockSpec((1, HD), lambda i: (0, 0)), tab, tab],
        out_specs=[pl.BlockSpec((tm, qw), lambda i: (i, 0)),
                   pl.BlockSpec((tm, kw), lambda i: (i, 0))],
        out_shape=[jax.ShapeDtypeStruct((st.n, qw), BF16),
                   jax.ShapeDtypeStruct((st.n, kw), F32)],
        compiler_params=_cparams(("parallel",)), name="gqa_prep",
    )(qkv, qkv, g_q.reshape(1, HD), g_k.reshape(1, HD), *tables)


def _softmax_parts(q, ks):
    ss = [lax.dot_general(q, k, (((1,), (1,)), ((), ())), preferred_element_type=F32) for k in ks]
    m = ss[0].max(axis=-1, keepdims=True)
    for s in ss[1:]:
        m = jnp.maximum(m, s.max(axis=-1, keepdims=True))
    ps = [jnp.exp2(s - m) for s in ss]
    l = ps[0].sum(axis=-1, keepdims=True)
    for p in ps[1:]:
        l = l + p.sum(axis=-1, keepdims=True)
    return ps, l


def _gqa_attn_kernel(q_ref, k_ref, v_ref, *rest, cache, aliased):
    if aliased:
        rest = rest[:-2] + rest[-1:]
    if cache:
        ck_ref, cv_ref, o_ref = rest
        ks = [ck_ref[...].astype(BF16), k_ref[...].astype(BF16)]
        vs = [cv_ref[...].astype(BF16), v_ref[...].astype(BF16)]
    else:
        (o_ref,) = rest
        ks = [k_ref[...].astype(BF16)]
        vs = [v_ref[...].astype(BF16)]
    grp = ATTN_HEADS // ATTN_KV_HEADS
    for h in range(grp):
        q = q_ref[:, h * HD:(h + 1) * HD]
        ps, l = _softmax_parts(q, ks)
        o = jnp.dot(ps[0].astype(BF16), vs[0], preferred_element_type=F32)
        for p, v in zip(ps[1:], vs[1:]):
            o = o + jnp.dot(p.astype(BF16), v, preferred_element_type=F32)
        o_ref[:, h * HD:(h + 1) * HD] = (o / l).astype(o_ref.dtype)


def _gqa_attn(st, q, k, qkv, row0, nb, seq, cache_k, cache_v, layer_j, prev):
    tq = min(512, seq)
    nq = seq // tq
    grp_w = (ATTN_HEADS // ATTN_KV_HEADS) * HD
    v_col0 = ATTN_HEADS + ATTN_KV_HEADS
    b0 = row0 // seq
    q0 = row0 // tq
    in_specs = [
        pl.BlockSpec((tq, grp_w), lambda b, g, i: (q0 + b * nq + i, g)),
        pl.BlockSpec((seq, HD), lambda b, g, i: (b + b0, g)),
        pl.BlockSpec((seq, HD), lambda b, g, i: (b + b0, v_col0 + g)),
    ]
    args = [q, k, qkv]
    cache = cache_k is not None
    if cache:
        past = cache_k.shape[2]
        ck = cache_k.reshape(cache_k.shape[0], cache_k.shape[1], past, ATTN_KV_HEADS * HD)
        cv = cache_v.reshape(ck.shape)
        in_specs += [pl.BlockSpec((None, None, past, HD), lambda b, g, i: (b, layer_j, 0, g))] * 2
        args += [ck, cv]
    aliases = {}
    if prev is not None:
        in_specs.append(pl.BlockSpec(memory_space=pl.ANY))
        args.append(prev)
        aliases = {len(args) - 1: 0}
    return pl.pallas_call(
        functools.partial(_gqa_attn_kernel, cache=cache, aliased=prev is not None),
        grid=(nb, ATTN_KV_HEADS, nq), in_specs=in_specs,
        out_specs=pl.BlockSpec((tq, grp_w), lambda b, g, i: (q0 + b * nq + i, g)),
        out_shape=jax.ShapeDtypeStruct((st.n, ATTN_HEADS * HD), BF16),
        input_output_aliases=aliases,
        compiler_params=_cparams(("parallel", "parallel", "parallel")), name="gqa_attn",
    )(*args)


def _diff_attn_kernel(lam_ref, q_ref, k_ref, v_ref, g_ref, *rest, cache, out_scale, aliased):
    if aliased:
        rest = rest[:-2] + rest[-1:]
    if cache:
        ck_ref, cv_ref, o_ref = rest
        vs = [cv_ref[...].astype(BF16), v_ref[...].astype(BF16)]
    else:
        (o_ref,) = rest
        vs = [v_ref[...].astype(BF16)]
    lam = lam_ref[0]
    outs = []
    for m in range(2):
        q = q_ref[:, m * HD:(m + 1) * HD]
        ks = [k_ref[:, m * HD:(m + 1) * HD]]
        if cache:
            ks = [ck_ref[:, m * HD:(m + 1) * HD].astype(BF16)] + ks
        ps, l = _softmax_parts(q, ks)
        o = jnp.dot(ps[0].astype(BF16), vs[0], preferred_element_type=F32)
        for p, v in zip(ps[1:], vs[1:]):
            o = o + jnp.dot(p.astype(BF16), v, preferred_element_type=F32)
        outs.append(o / l)
    o = outs[0] - lam * outs[1]
    o = o * lax.rsqrt(jnp.mean(o * o, axis=-1, keepdims=True) + EPS) * g_ref[...]
    o_ref[...] = (o * out_scale).astype(o_ref.dtype)


def _diff_attn(st, lam, q, k, v, g_sub, lam_init, row0, nb, seq, cache_k, cache_v, layer_j, prev):
    tq = min(512, seq)
    nq = seq // tq
    hw = 2 * HD
    b0 = row0 // seq
    q0 = row0 // tq
    in_specs = [
        pl.BlockSpec(memory_space=pltpu.SMEM),
        pl.BlockSpec((tq, hw), lambda b, h, i: (q0 + b * nq + i, h)),
        pl.BlockSpec((seq, hw), lambda b, h, i: (b + b0, h)),
        pl.BlockSpec((seq, hw), lambda b, h, i: (b + b0, h)),
        pl.BlockSpec((1, hw), lambda b, h, i: (0, 0)),
    ]
    args = [lam, q, k, v, g_sub.reshape(1, hw)]
    cache = cache_k is not None
    if cache:
        past = cache_k.shape[2]
        ck = cache_k.reshape(cache_k.shape[0], cache_k.shape[1], past, 2 * DIFF_HEADS * HD)
        cv = cache_v.reshape(ck.shape)
        in_specs += [pl.BlockSpec((None, None, past, hw), lambda b, h, i: (b, layer_j, 0, h))] * 2
        args += [ck, cv]
    aliases = {}
    if prev is not None:
        in_specs.append(pl.BlockSpec(memory_space=pl.ANY))
        args.append(prev)
        aliases = {len(args) - 1: 0}
    return pl.pallas_call(
        functools.partial(_diff_attn_kernel, cache=cache, out_scale=1.0 - lam_init,
                          aliased=prev is not None),
        grid=(nb, DIFF_HEADS, nq), in_specs=in_specs,
        out_specs=pl.BlockSpec((tq, hw), lambda b, h, i: (q0 + b * nq + i, h)),
        out_shape=jax.ShapeDtypeStruct((st.n, DIFF_HEADS * hw), BF16),
        input_output_aliases=aliases,
        compiler_params=_cparams(("parallel", "parallel", "parallel")), name="diff_attn",
    )(*args)


def _conv_kernel(x_ref, w_ref, b_ref, *rest):
    o_ref, pad_ref = rest[-2:]
    seq = x_ref.shape[0]
    half = SSD_CONV // 2
    zeros = jnp.zeros((8, x_ref.shape[1]), F32)
    pad_ref[0:8, :] = zeros
    pad_ref[seq + 8:seq + 16, :] = zeros
    pad_ref[8:seq + 8, :] = x_ref[...]
    xp = pad_ref[...]
    rows = seq + 16
    acc = jnp.broadcast_to(b_ref[...], x_ref.shape)
    for k in range(SSD_CONV):
        shifted = xp if k == half else pltpu.roll(xp, (half - k) % rows, 0)
        acc = acc + w_ref[k:k + 1, :] * shifted[8:8 + seq, :]
    o_ref[...] = _silu(acc)


def _ssd_conv(st, xbc, conv_w, conv_b, row0, nb, seq, prev):
    tc = 512
    b0 = row0 // seq
    in_specs = [pl.BlockSpec((seq, tc), lambda b, c: (b + b0, c)),
                pl.BlockSpec((SSD_CONV, tc), lambda b, c: (0, c)),
                pl.BlockSpec((1, tc), lambda b, c: (0, c))]
    args = [xbc, conv_w, conv_b.reshape(1, SSD_CONV_DIM)]
    aliases = {}
    if prev is not None:
        in_specs.append(pl.BlockSpec(memory_space=pl.ANY))
        args.append(prev)
        aliases = {len(args) - 1: 0}
    return pl.pallas_call(
        _conv_kernel, grid=(nb, SSD_CONV_DIM // tc),
        in_specs=in_specs,
        out_specs=pl.BlockSpec((seq, tc), lambda b, c: (b + b0, c)),
        out_shape=jax.ShapeDtypeStruct((st.n, SSD_CONV_DIM), F32),
        scratch_shapes=[pltpu.VMEM((seq + 16, tc), F32)],
        input_output_aliases=aliases,
        compiler_params=_cparams(("parallel", "parallel")), name="ssd_conv",
    )(*args)


def _ssd_scan_kernel(x_ref, b_ref, c_ref, dt_ref, bias_ref, alog_ref, *rest,
                     rev, nc, has_init, want_final, aliased):
    pos = 0
    h0_ref = None
    if has_init:
        h0_ref = rest[pos]; pos += 1
    if aliased:
        pos += 1
    y_ref = rest[pos]; pos += 1
    hf_ref = None
    if want_final:
        hf_ref = rest[pos]; pos += 1
    state_ref, acol_ref, arow_ref, dtrow_ref, wrow_ref, dec_ref = rest[pos:]
    c = pl.program_id(1)
    q = SSD_CHUNK
    lane0 = SSD_HEADS if rev else 0
    r_heads = SSD_HEADS // SSD_GROUPS
    gw = r_heads * SSD_HEADDIM

    ii = lax.broadcasted_iota(jnp.int32, (q, q), 0)
    jj = lax.broadcasted_iota(jnp.int32, (q, q), 1)
    causal = (jj >= ii) if rev else (jj <= ii)

    @pl.when(c == 0)
    def _():
        if has_init:
            for blk in range(SSD_INNER // 128):
                state_ref[:, blk * 128:(blk + 1) * 128] = h0_ref[blk * 128:(blk + 1) * 128, :].T
        else:
            state_ref[...] = jnp.zeros(state_ref.shape, F32)

    xdt = dt_ref[...] + bias_ref[...]
    dt = jnp.maximum(xdt, 0.0) + jnp.log(1.0 + jnp.exp(-jnp.abs(xdt)))
    la = dt * (-jnp.exp(alog_ref[...]))
    tri = causal.astype(BF16)
    tri_t = ((ii >= jj) if rev else (ii <= jj)).astype(BF16)
    la3 = _split3(la)
    acol_ref[...] = (jnp.dot(tri, la3[0], preferred_element_type=F32)
                     + jnp.dot(tri, la3[1], preferred_element_type=F32)
                     + jnp.dot(tri, la3[2], preferred_element_type=F32))
    arow = _dot3(_split3(la.T), tri_t)
    end = 0 if rev else q - 1
    a_end = jnp.broadcast_to(arow[:, end:end + 1], arow.shape)
    dt_t = dt.T
    arow_ref[...] = arow
    dtrow_ref[...] = dt_t
    wrow_ref[...] = jnp.exp(a_end - arow) * dt_t
    dec_ref[...] = jnp.exp(a_end)

    lane_lo = lax.broadcasted_iota(jnp.int32, (1, 128), 1) < SSD_HEADDIM
    zeros_blk = jnp.zeros((SSD_STATE, q), BF16)

    for g in range(SSD_GROUPS):
        bmat = b_ref[:, g * SSD_STATE:(g + 1) * SSD_STATE]
        cmat = c_ref[:, g * SSD_STATE:(g + 1) * SSD_STATE]
        cb = lax.dot_general(cmat.astype(BF16), bmat.astype(BF16), (((1,), (1,)), ((), ())),
                             preferred_element_type=F32)
        b_t = bmat.T
        for pair in range(r_heads // 2):
            col = g * gw + pair * 128
            x_pair = x_ref[:, col:col + 128].astype(BF16)
            st_old = state_ref[:, col:col + 128]
            rhs = jnp.concatenate([x_pair, st_old.astype(BF16)], axis=0)
            res = []
            for sub in range(2):
                hd = lane0 + g * r_heads + 2 * pair + sub
                a_i = jnp.broadcast_to(acol_ref[:, hd:hd + 1], (q, q))
                seg = a_i - arow_ref[hd:hd + 1, :]
                decay = jnp.exp(jnp.where(causal, seg, -jnp.inf))
                m_intra = (cb * decay * dtrow_ref[hd:hd + 1, :]).astype(BF16)
                c_scaled = (cmat * jnp.exp(a_i)).astype(BF16)
                bw_t = (b_t * wrow_ref[hd:hd + 1, :]).astype(BF16)
                lhs = jnp.concatenate(
                    [jnp.concatenate([m_intra, c_scaled], axis=1),
                     jnp.concatenate([bw_t, zeros_blk], axis=1)], axis=0)
                res.append(jnp.dot(lhs, rhs, preferred_element_type=F32))
            hd0 = lane0 + g * r_heads + 2 * pair
            y_ref[:, col:col + 128] = jnp.where(lane_lo, res[0][:q], res[1][:q])
            dec_pair = jnp.where(lane_lo, dec_ref[hd0:hd0 + 1, :], dec_ref[hd0 + 1:hd0 + 2, :])
            upd = jnp.where(lane_lo, res[0][q:], res[1][q:])
            state_ref[:, col:col + 128] = dec_pair * st_old + upd

    if want_final:
        @pl.when(c == nc - 1)
        def _():
            for blk in range(SSD_INNER // 128):
                hf_ref[blk * 128:(blk + 1) * 128, :] = state_ref[:, blk * 128:(blk + 1) * 128].T


def _ssd_scan(st, xbc, dt, dt_bias, a_log, row0, nb, seq, rev, h0, layer_j, want_final, prev):
    nc = seq // SSD_CHUNK
    c0 = row0 // SSD_CHUNK

    def chunk(b, c):
        return c0 + b * nc + (nc - 1 - c if rev else c)

    in_specs = [
        pl.BlockSpec((SSD_CHUNK, SSD_INNER), lambda b, c: (chunk(b, c), 0)),
        pl.BlockSpec((SSD_CHUNK, SSD_GN), lambda b, c: (chunk(b, c), SSD_INNER // SSD_GN)),
        pl.BlockSpec((SSD_CHUNK, SSD_GN), lambda b, c: (chunk(b, c), SSD_INNER // SSD_GN + 1)),
        pl.BlockSpec((SSD_CHUNK, 128), lambda b, c: (chunk(b, c), 0)),
        pl.BlockSpec((1, 128), lambda b, c: (0, 0)),
        pl.BlockSpec((1, 128), lambda b, c: (0, 0)),
    ]
    args = [xbc, xbc, xbc, dt, dt_bias.reshape(1, 128), a_log.reshape(1, 128)]
    if h0 is not None:
        h0r = h0.reshape(h0.shape[0], h0.shape[1], SSD_INNER, SSD_STATE)
        in_specs.append(pl.BlockSpec((None, None, SSD_INNER, SSD_STATE),
                                     lambda b, c: (b, layer_j, 0, 0)))
        args.append(h0r)
    aliases = {}
    if prev is not None:
        in_specs.append(pl.BlockSpec(memory_space=pl.ANY))
        args.append(prev)
        aliases = {len(args) - 1: 0}
    out_shape = [jax.ShapeDtypeStruct((st.n, SSD_INNER), F32)]
    out_specs = [pl.BlockSpec((SSD_CHUNK, SSD_INNER), lambda b, c: (chunk(b, c), 0))]
    if want_final:
        out_shape.append(jax.ShapeDtypeStruct((nb, SSD_INNER, SSD_STATE), F32))
        out_specs.append(pl.BlockSpec((None, SSD_INNER, SSD_STATE), lambda b, c: (b, 0, 0)))
    res = pl.pallas_call(
        functools.partial(_ssd_scan_kernel, rev=rev, nc=nc, has_init=h0 is not None,
                          want_final=want_final, aliased=prev is not None),
        grid=(nb, nc), in_specs=in_specs, out_specs=out_specs, out_shape=out_shape,
        scratch_shapes=[pltpu.VMEM((SSD_STATE, SSD_INNER), F32),
                        pltpu.VMEM((SSD_CHUNK, 128), F32)] + [pltpu.VMEM((128, SSD_CHUNK), F32)] * 4,
        input_output_aliases=aliases,
        compiler_params=_cparams(("parallel", "arbitrary")), name="ssd_scan",
    )(*args)
    return res


def _ssd_gate_kernel(yf_ref, yb_ref, x_ref, z_ref, d_ref, g_ref, o_ref):
    gw = SSD_INNER // SSD_GROUPS
    for grp in range(SSD_GROUPS):
        sl = slice(grp * gw, (grp + 1) * gw)
        y = yf_ref[:, sl] + yb_ref[:, sl] + d_ref[:, sl] * x_ref[:, sl]
        gated = y * _silu(z_ref[:, sl])
        normed = gated * lax.rsqrt(jnp.mean(gated * gated, axis=-1, keepdims=True) + EPS)
        o_ref[:, sl] = (normed * g_ref[:, sl]).astype(o_ref.dtype)


def _ssd_gate(yf, yb, xbc, z, d_sum, g_norm):
    n = yf.shape[0]
    tm = 256
    row = pl.BlockSpec((tm, SSD_INNER), lambda i: (i, 0))
    vec = pl.BlockSpec((1, SSD_INNER), lambda i: (0, 0))
    return pl.pallas_call(
        _ssd_gate_kernel, grid=(n // tm,),
        in_specs=[row, row, row, row, vec, vec], out_specs=row,
        out_shape=jax.ShapeDtypeStruct((n, SSD_INNER), BF16),
        compiler_params=_cparams(("parallel",)), name="ssd_gate",
    )(yf, yb, xbc, z, d_sum, g_norm.reshape(1, SSD_INNER))


def _moe_up_kernel(be_ref, nu_ref, a_ref, wg_ref, wu_ref, *rest, blk0):
    o_ref = rest[-1]
    m = pl.program_id(1) + blk0

    @pl.when(m < nu_ref[0])
    def _():
        a = a_ref[...]
        gate = jnp.dot(a, wg_ref[...].astype(BF16), preferred_element_type=F32)
        up = jnp.dot(a, wu_ref[...].astype(BF16), preferred_element_type=F32)
        o_ref[...] = (_silu(gate) * up).astype(o_ref.dtype)

    @pl.when(m >= nu_ref[0])
    def _():
        o_ref[...] = jnp.zeros(o_ref.shape, o_ref.dtype)


def _moe_down_kernel(be_ref, nu_ref, a_ref, w_ref, o_ref):
    m = pl.program_id(0)

    @pl.when(m < nu_ref[0])
    def _():
        o_ref[...] = jnp.dot(a_ref[...], w_ref[...].astype(BF16),
                             preferred_element_type=F32).astype(o_ref.dtype)

    @pl.when(m >= nu_ref[0])
    def _():
        o_ref[...] = jnp.zeros(o_ref.shape, o_ref.dtype)


def _moe(st, h, logits, w_gu, w_down, f):
    n = st.n
    tb = MOE_BLOCK
    n_assign = n * TOP_K
    n_blocks = n_assign // tb + N_EXPERTS
    slots = n_blocks * tb
    top_logit, top_e = lax.top_k(logits, TOP_K)
    gates = jax.nn.softmax(top_logit, axis=-1)
    flat_e = top_e.reshape(-1).astype(jnp.int32)
    onehot = (flat_e[:, None] == jnp.arange(N_EXPERTS, dtype=jnp.int32)[None, :]).astype(jnp.int32)
    rank = jnp.sum((jnp.cumsum(onehot, axis=0) - onehot) * onehot, axis=1)
    counts = jnp.sum(onehot, axis=0)
    padded = (counts + tb - 1) // tb * tb
    pad_end = jnp.cumsum(padded)
    pad_start = pad_end - padded
    dest = pad_start[flat_e] + rank
    slot_tok = jnp.zeros((slots,), jnp.int32).at[dest].set(
        jnp.arange(n_assign, dtype=jnp.int32) // TOP_K, unique_indices=True, mode="promise_in_bounds")
    block_start = jnp.arange(n_blocks, dtype=jnp.int32) * tb
    block_e = jnp.minimum(jnp.searchsorted(pad_end, block_start, side="right"),
                          N_EXPERTS - 1).astype(jnp.int32)
    n_used = (pad_end[-1] // tb).astype(jnp.int32).reshape(1)

    tn = 512
    nf = EXPERT_DIM // tn

    def live_block(m, nu):
        return jnp.minimum(m, nu[0] - 1)

    half = n_blocks // 2
    act = None
    for blk0, nblk in ((0, half), (half, n_blocks - half)):
        xs = h.at[slot_tok[blk0 * tb:(blk0 + nblk) * tb]].get(mode="promise_in_bounds")

        def local(m, nu, blk0=blk0, nblk=nblk):
            return jnp.clip(live_block(m + blk0, nu) - blk0, 0, nblk - 1)

        def expert(m, be, nu, blk0=blk0):
            return be[live_block(m + blk0, nu)]

        in_specs = [
            pl.BlockSpec((tb, D_MODEL), lambda j, m, be, nu, local=local: (local(m, nu), 0)),
            pl.BlockSpec((None, None, D_MODEL, tn),
                         lambda j, m, be, nu, expert=expert: (f, expert(m, be, nu), 0, j)),
            pl.BlockSpec((None, None, D_MODEL, tn),
                         lambda j, m, be, nu, expert=expert: (f, expert(m, be, nu), 0, j + nf)),
        ]
        args = [block_e, n_used, xs, w_gu, w_gu]
        aliases = {}
        if act is not None:
            in_specs.append(pl.BlockSpec(memory_space=pl.ANY))
            args.append(act)
            aliases = {len(args) - 1: 0}
        act = pl.pallas_call(
            functools.partial(_moe_up_kernel, blk0=blk0),
            grid_spec=pltpu.PrefetchScalarGridSpec(
                num_scalar_prefetch=2, grid=(nf, nblk), in_specs=in_specs,
                out_specs=pl.BlockSpec((tb, tn), lambda j, m, be, nu, blk0=blk0: (m + blk0, j)),
            ),
            out_shape=jax.ShapeDtypeStruct((slots, EXPERT_DIM), BF16),
            input_output_aliases=aliases,
            compiler_params=_cparams(("parallel", "arbitrary")), name="moe_up",
        )(*args)

    tn2 = 256
    yb = pl.pallas_call(
        _moe_down_kernel,
        grid_spec=pltpu.PrefetchScalarGridSpec(
            num_scalar_prefetch=2, grid=(n_blocks, D_MODEL // tn2),
            in_specs=[
                pl.BlockSpec((tb, EXPERT_DIM), lambda m, j, be, nu: (live_block(m, nu), 0)),
                pl.BlockSpec((None, None, EXPERT_DIM, tn2),
                             lambda m, j, be, nu: (f, be[live_block(m, nu)], 0, j)),
            ],
            out_specs=pl.BlockSpec((tb, tn2), lambda m, j, be, nu: (m, j)),
        ),
        out_shape=jax.ShapeDtypeStruct((slots, D_MODEL), BF16),
        compiler_params=_cparams(("arbitrary", "arbitrary")), name="moe_down",
    )(block_e, n_used, act, w_down)

    dest2 = dest.reshape(n, TOP_K)
    y0 = yb.at[dest2[:, 0]].get(mode="promise_in_bounds")
    y1 = yb.at[dest2[:, 1]].get(mode="promise_in_bounds")
    return y0, y1, gates


def _combine_kernel(x_ref, y0_ref, y1_ref, g_ref, gate_ref, *rest, norm):
    y = g_ref[:, 0:1] * y0_ref[...].astype(F32) + g_ref[:, 1:2] * y1_ref[...].astype(F32)
    xn = x_ref[...] + gate_ref[...] * y
    if norm:
        gn_ref, shift_ref, scale_ref, o_ref, h_ref = rest
        _norm_mod_store(xn, gn_ref, shift_ref, scale_ref, h_ref)
    else:
        (o_ref,) = rest
    o_ref[...] = xn


def _combine(st, x, y0, y1, gates, mods, layer, g_next=None):
    tm = st.tile(512)
    row = pl.BlockSpec((tm, D_MODEL), lambda i: (i, 0))

    def mod(lyr, which):
        return pl.BlockSpec((None, 1, D_MODEL),
                            lambda i: (_mod_index(lyr, which, st.mod_row(i, tm)), 0, 0))

    in_specs = [row, row, row, pl.BlockSpec((tm, TOP_K), lambda i: (i, 0)), mod(layer, 5)]
    args = [x, y0, y1, gates, mods]
    out_shape = [jax.ShapeDtypeStruct((st.n, D_MODEL), F32)]
    out_specs = [row]
    if g_next is not None:
        in_specs += [pl.BlockSpec((1, D_MODEL), lambda i: (0, 0)), mod(layer + 1, 0), mod(layer + 1, 1)]
        args += [g_next.reshape(1, D_MODEL), mods, mods]
        out_shape.append(jax.ShapeDtypeStruct((st.n, D_MODEL), BF16))
        out_specs.append(row)
    res = pl.pallas_call(
        functools.partial(_combine_kernel, norm=g_next is not None), grid=(st.n // tm,),
        in_specs=in_specs, out_specs=out_specs, out_shape=out_shape,
        compiler_params=_cparams(("parallel",)), name="moe_combine",
    )(*args)
    return res if g_next is not None else (res[0], None)


def kernel(x_prompt, x_sample, cache_attn_k, cache_attn_v, cache_diff_k, cache_diff_v, state_ssd_fwd, state_ssd_bwd, c, c_ctx, w_ada, b_ada, g_mix, g_ffn, g_final, attn_w_in, attn_g_q, attn_g_k, attn_w_out, diff_w_in, diff_lambda, diff_g_sub, diff_w_out, ssd_w_in, ssd_conv_w, ssd_conv_b, ssd_dt_bias, ssd_a_log, ssd_d, ssd_g_norm, ssd_w_out, ffn_w_gu, ffn_w_down, moe_w_router, moe_w_gu, moe_w_down):
    pb, pl_len, _ = x_prompt.shape
    sb, sl_len, _ = x_sample.shape
    st = _Stream(pb, pl_len, sb, sl_len)
    np_, ns = st.n_prompt, st.n_sample
    tm = st.tile(1024)

    x = jnp.concatenate([x_prompt.reshape(np_, D_MODEL), x_sample.reshape(ns, D_MODEL)], axis=0)

    cond = jnp.zeros((MOD_ROWS, D_MODEL), F32).at[:sb].set(c).at[CTX_ROW].set(c_ctx)
    cond_act = jax.nn.silu(cond).astype(BF16)
    mods = []
    for i in range(DEPTH):
        m = _matmul(cond_act, w_ada, (i,), n_off=0, n_out=6 * D_MODEL, tm=MOD_ROWS, tn=1024,
                    out_dtype=F32, name="ada")
        mods.append(m + b_ada[i][None, :])
    mods = jnp.stack(mods).reshape(DEPTH * MOD_ROWS * 6, 1, D_MODEL)

    tables = _rope_tables(sl_len, tm)
    prep_tables = _rope_tables(sl_len, GQA_PREP_ROWS)

    def table_index(i):
        return _table_block(st, i, tm)

    tm2 = st.tile(2048)

    def gate_index(layer, which, tile):
        return lambda i: _mod_index(layer, which, st.mod_row(i, tile))

    new_attn_k, new_attn_v, new_diff_k, new_diff_v, new_ssd_f, new_ssd_b = [], [], [], [], [], []
    def out_proj(o, w16, x, layer, fuse_norm):
        if fuse_norm:
            return _out_proj_norm(st, o, w16, x, mods, layer, g_ffn[layer])
        xo = _matmul(o, w16, (), n_off=0, n_out=D_MODEL, tm=tm2, tn=512,
                     out_dtype=F32, epilogue="resid", resid=x, mods=mods,
                     gate_index=gate_index(layer, 2, tm2), name="attn_out")
        return xo, None

    h_pre = None
    for i in range(DEPTH):
        kind, j = i % 3, i // 3
        f = i // 2
        w_router = moe_w_router[f] if i % 2 == 1 else None
        h = h_pre if h_pre is not None else _norm_mod(st, x, g_mix[i], mods, i, 0)
        h_pre = hf = logits = None
        if kind == 0:
            qkv = _matmul(h, attn_w_in, (j,), n_off=0, n_out=attn_w_in.shape[-1], tm=tm, tn=1024,
                          out_dtype=F32, name="attn_in")
            q, k = _gqa_prep(st, qkv, attn_g_q[j], attn_g_k[j], prep_tables)
            o = _gqa_attn(st, q, k, qkv, 0, pb, pl_len, None, None, j, None)
            o = _gqa_attn(st, q, k, qkv, np_, sb, sl_len, cache_attn_k, cache_attn_v, j, o)
            new_attn_k.append(k[:np_].reshape(pb, pl_len, ATTN_KV_HEADS, HD))
            new_attn_v.append(qkv[:np_, (ATTN_HEADS + ATTN_KV_HEADS) * HD:]
                              .reshape(pb, pl_len, ATTN_KV_HEADS, HD))
            x, hf = out_proj(o, attn_w_out[j].astype(BF16), x, i, w_router is None)
        elif kind == 1:
            lam_init = 0.8 - 0.6 * math.exp(-0.3 * i)
            lf = diff_lambda[j]
            lam = (jnp.exp(jnp.sum(lf[0] * lf[1])) - jnp.exp(jnp.sum(lf[2] * lf[3])) + lam_init).reshape(1)
            w = 2 * DIFF_HEADS * HD
            q = _matmul(h, diff_w_in, (j,), n_off=0, n_out=w, tm=tm, tn=1024, out_dtype=BF16,
                        epilogue="rope", scale=Q_SCALE,
                        tables=tables, table_index=table_index, name="diff_in_q")
            k = _matmul(h, diff_w_in, (j,), n_off=w, n_out=w, tm=tm, tn=1024, out_dtype=BF16,
                        epilogue="rope", tables=tables, table_index=table_index, name="diff_in_k")
            v = _matmul(h, diff_w_in, (j,), n_off=2 * w, n_out=w, tm=tm, tn=1024, out_dtype=F32,
                        name="diff_in_v")
            k_ctx = _matmul(h[:np_], diff_w_in, (j,), n_off=w, n_out=w, tm=tm, tn=1024,
                            out_dtype=F32, name="diff_in_kctx")
            o = _diff_attn(st, lam, q, k, v, diff_g_sub[j], lam_init, 0, pb, pl_len, None, None, j, None)
            o = _diff_attn(st, lam, q, k, v, diff_g_sub[j], lam_init, np_, sb, sl_len,
                           cache_diff_k, cache_diff_v, j, o)
            new_diff_k.append(k_ctx.reshape(pb, pl_len, 2 * DIFF_HEADS, HD))
            new_diff_v.append(v[:np_].reshape(pb, pl_len, DIFF_HEADS, 2 * HD))
            x, hf = out_proj(o, diff_w_out[j].astype(BF16), x, i, w_router is None)
        else:
            z = _matmul(h, ssd_w_in, (j,), n_off=0, n_out=SSD_INNER, tm=tm, tn=1024,
                        out_dtype=F32, name="ssd_in_z")
            xbc = _matmul(h, ssd_w_in, (j,), n_off=SSD_INNER, n_out=SSD_CONV_DIM, tm=tm, tn=1024,
                          out_dtype=F32, name="ssd_in_xbc")
            dt = _matmul(h, ssd_w_in, (j,), n_off=SSD_INNER + SSD_CONV_DIM, n_out=2 * SSD_HEADS,
                         tm=tm, tn=128, out_dtype=F32, name="ssd_in_dt")
            xc = _ssd_conv(st, xbc, ssd_conv_w[j], ssd_conv_b[j], 0, pb, pl_len, None)
            xc = _ssd_conv(st, xbc, ssd_conv_w[j], ssd_conv_b[j], np_, sb, sl_len, xc)
            ys = []
            for rev, h0 in ((False, state_ssd_fwd), (True, state_ssd_bwd)):
                y, hfin = _ssd_scan(st, xc, dt, ssd_dt_bias[j], ssd_a_log[j], 0, pb, pl_len, rev,
                                    None, j, True, None)
                (y,) = _ssd_scan(st, xc, dt, ssd_dt_bias[j], ssd_a_log[j], np_, sb, sl_len, rev,
                                 h0, j, False, y)
                ys.append(y)
                (new_ssd_b if rev else new_ssd_f).append(
                    hfin.reshape(pb, SSD_HEADS, SSD_HEADDIM, SSD_STATE))
            d_sum = jnp.repeat(ssd_d[j][0] + ssd_d[j][1], SSD_HEADDIM).reshape(1, SSD_INNER)
            o = _ssd_gate(ys[0], ys[1], xc, z, d_sum, ssd_g_norm[j])
            x = _matmul(o, ssd_w_out[j].astype(BF16), (), n_off=0, n_out=D_MODEL, tm=tm, tn=512,
                        out_dtype=F32, epilogue="resid", resid=x, mods=mods,
                        gate_index=gate_index(i, 2, tm), name="ssd_out")

        if hf is None:
            if w_router is not None:
                hf, logits = _norm_mod(st, x, g_ffn[i], mods, i, 3, w_router=w_router)
            else:
                hf = _norm_mod(st, x, g_ffn[i], mods, i, 3)
        if i % 2 == 0:
            act = _matmul(hf, ffn_w_gu, (f,), n_off=0, n_out=FFN_DIM, tm=tm, tn=512,
                          out_dtype=BF16, epilogue="swiglu", up_off=FFN_DIM, name="ffn_up")
            x = _matmul(act, ffn_w_down[f].astype(BF16), (), n_off=0, n_out=D_MODEL, tm=tm, tn=512,
                        out_dtype=F32, epilogue="resid", resid=x, mods=mods,
                        gate_index=gate_index(i, 5, tm), name="ffn_down")
        else:
            y0, y1, gates = _moe(st, hf, logits[:, :N_EXPERTS], moe_w_gu, moe_w_down, f)
            x, h_pre = _combine(st, x, y0, y1, gates, mods, i,
                                g_next=g_mix[i + 1] if i + 1 < DEPTH else None)

    y_prompt = _final_norm(x, g_final, 0, np_).reshape(pb, pl_len, D_MODEL)
    y_sample = _final_norm(x, g_final, np_, ns).reshape(sb, sl_len, D_MODEL)
    return (y_prompt, y_sample,
            jnp.stack(new_attn_k, axis=1), jnp.stack(new_attn_v, axis=1),
            jnp.stack(new_diff_k, axis=1), jnp.stack(new_diff_v, axis=1),
            jnp.stack(new_ssd_f, axis=1), jnp.stack(new_ssd_b, axis=1))
```

```python
import functools
import math

import jax
import jax.numpy as jnp
from jax import lax
from jax.experimental import pallas as pl
from jax.experimental.pallas import tpu as pltpu

F32 = jnp.float32
BF16 = jnp.bfloat16

D_MODEL = 2048
DEPTH = 4
GRID_W = 64
ROPE_THETA = 10000.0
EPS = 1e-6
HD = 128
Q_SCALE = math.log2(math.e) / math.sqrt(HD)
ATTN_HEADS = 16
ATTN_KV_HEADS = 4
DIFF_HEADS = 8
SSD_INNER = 2 * D_MODEL
SSD_HEADDIM = 64
SSD_HEADS = SSD_INNER // SSD_HEADDIM
SSD_GROUPS = 8
SSD_STATE = 128
SSD_CONV = 5
SSD_CHUNK = 128
SSD_GN = SSD_GROUPS * SSD_STATE
SSD_CONV_DIM = SSD_INNER + 2 * SSD_GN
FFN_DIM = 5632
N_EXPERTS = 8
TOP_K = 2
EXPERT_DIM = 7168

MOD_ROWS = 16
CTX_ROW = 8
MOE_BLOCK = 1024
MOE_HEAD_BLOCKS = 8
VMEM_LIMIT = 56 * 1024 * 1024


def _cparams(sem):
    return pltpu.CompilerParams(dimension_semantics=sem, vmem_limit_bytes=VMEM_LIMIT)


def _silu(x):
    return x / (1.0 + jnp.exp(-x))


def _split3(x):
    hi = x.astype(BF16)
    r = x - hi.astype(F32)
    mid = r.astype(BF16)
    lo = (r - mid.astype(F32)).astype(BF16)
    return hi, mid, lo


def _dot3(a3, b):
    out = jnp.dot(a3[0], b, preferred_element_type=F32)
    out = out + jnp.dot(a3[1], b, preferred_element_type=F32)
    return out + jnp.dot(a3[2], b, preferred_element_type=F32)


class _Stream:
    def __init__(self, n_prompt_batch, prompt_len, n_sample_batch, sample_len):
        self.pb, self.pl_, self.sb, self.sl = n_prompt_batch, prompt_len, n_sample_batch, sample_len
        self.n_prompt = n_prompt_batch * prompt_len
        self.n_sample = n_sample_batch * sample_len
        self.n = self.n_prompt + self.n_sample

    def mod_row(self, i, tm):
        pt = self.n_prompt // tm
        per = self.sl // tm
        return jnp.where(i < pt, CTX_ROW, (i - pt) // per)

    def tile(self, want):
        t = want
        while self.n_prompt % t or self.sl % t:
            t //= 2
        return t


def _mod_index(layer, which, row):
    return (layer * MOD_ROWS + row) * 6 + which


def _norm_mod_store(x, g_ref, shift_ref, scale_ref, o_ref, wr_ref=None, lg_ref=None):
    y = x * lax.rsqrt(jnp.mean(x * x, axis=-1, keepdims=True) + EPS) * g_ref[...]
    h = y * (1.0 + scale_ref[...]) + shift_ref[...]
    if wr_ref is not None:
        w = wr_ref[...]
        whi = w.astype(BF16)
        wlo = (w - whi.astype(F32)).astype(BF16)
        h3 = _split3(h)
        lg = _dot3(h3, whi)
        lg = lg + jnp.dot(h3[0], wlo, preferred_element_type=F32)
        lg = lg + jnp.dot(h3[1], wlo, preferred_element_type=F32)
        lg_ref[...] = lg
    o_ref[...] = h.astype(o_ref.dtype)


def _norm_mod_kernel(x_ref, g_ref, shift_ref, scale_ref, *rest, router):
    if router:
        wr_ref, o_ref, lg_ref = rest
        _norm_mod_store(x_ref[...], g_ref, shift_ref, scale_ref, o_ref, wr_ref, lg_ref)
    else:
        (o_ref,) = rest
        _norm_mod_store(x_ref[...], g_ref, shift_ref, scale_ref, o_ref)


def _router_pad(w_router):
    return jnp.zeros((D_MODEL, 128), F32).at[:, :N_EXPERTS].set(w_router)


def _out_norm_kernel(a_ref, w_ref, x_ref, gate_ref, g_ref, shift_ref, scale_ref, xo_ref, ho_ref):
    xn = x_ref[...] + gate_ref[...] * jnp.dot(a_ref[...], w_ref[...], preferred_element_type=F32)
    _norm_mod_store(xn, g_ref, shift_ref, scale_ref, ho_ref)
    xo_ref[...] = xn


def _out_proj_norm(st, a, w, x, mods, layer, g_next):
    tm = st.tile(512)
    kdim = a.shape[1]

    def mod(which):
        return pl.BlockSpec((None, 1, D_MODEL),
                            lambda i: (_mod_index(layer, which, st.mod_row(i, tm)), 0, 0))

    row = pl.BlockSpec((tm, D_MODEL), lambda i: (i, 0))
    in_specs = [pl.BlockSpec((tm, kdim), lambda i: (i, 0)),
                pl.BlockSpec((kdim, D_MODEL), lambda i: (0, 0), pipeline_mode=pl.Buffered(1)),
                row, mod(2), pl.BlockSpec((1, D_MODEL), lambda i: (0, 0)), mod(3), mod(4)]
    args = [a, w, x, mods, g_next.reshape(1, D_MODEL), mods, mods]
    out_shape = [jax.ShapeDtypeStruct((st.n, D_MODEL), F32), jax.ShapeDtypeStruct((st.n, D_MODEL), BF16)]
    return pl.pallas_call(
        _out_norm_kernel,
        grid=(st.n // tm,), in_specs=in_specs, out_specs=[row, row], out_shape=out_shape,
        compiler_params=_cparams(("parallel",)), name="out_proj_norm",
    )(*args)


def _norm_mod(st, x, g, mods, layer, which_shift, w_router=None):
    tm = st.tile(512)
    nt = st.n // tm
    g2 = g.reshape(1, D_MODEL)
    in_specs = [
        pl.BlockSpec((tm, D_MODEL), lambda i: (i, 0)),
        pl.BlockSpec((1, D_MODEL), lambda i: (0, 0)),
        pl.BlockSpec((None, 1, D_MODEL),
                     lambda i: (_mod_index(layer, which_shift, st.mod_row(i, tm)), 0, 0)),
        pl.BlockSpec((None, 1, D_MODEL),
                     lambda i: (_mod_index(layer, which_shift + 1, st.mod_row(i, tm)), 0, 0)),
    ]
    args = [x, g2, mods, mods]
    out_shape = [jax.ShapeDtypeStruct((st.n, D_MODEL), BF16)]
    out_specs = [pl.BlockSpec((tm, D_MODEL), lambda i: (i, 0))]
    if w_router is not None:
        in_specs.append(pl.BlockSpec((D_MODEL, 128), lambda i: (0, 0)))
        args.append(_router_pad(w_router))
        out_shape.append(jax.ShapeDtypeStruct((st.n, 128), F32))
        out_specs.append(pl.BlockSpec((tm, 128), lambda i: (i, 0)))
    res = pl.pallas_call(
        functools.partial(_norm_mod_kernel, router=w_router is not None),
        grid=(nt,), in_specs=in_specs, out_specs=out_specs, out_shape=out_shape,
        compiler_params=_cparams(("parallel",)), name="norm_mod",
    )(*args)
    return res if w_router is not None else res[0]


def _final_norm_kernel(x_ref, g_ref, o_ref):
    x = x_ref[...]
    o_ref[...] = x * lax.rsqrt(jnp.mean(x * x, axis=-1, keepdims=True) + EPS) * g_ref[...]


def _final_norm(x, g, row0, nrows):
    tm = 512
    r0 = row0 // tm
    return pl.pallas_call(
        _final_norm_kernel, grid=(nrows // tm,),
        in_specs=[pl.BlockSpec((tm, D_MODEL), lambda i: (i + r0, 0)),
                  pl.BlockSpec((1, D_MODEL), lambda i: (0, 0))],
        out_specs=pl.BlockSpec((tm, D_MODEL), lambda i: (i, 0)),
        out_shape=jax.ShapeDtypeStruct((nrows, D_MODEL), F32),
        compiler_params=_cparams(("parallel",)), name="final_norm",
    )(x, g.reshape(1, D_MODEL))


def _mm_kernel(*refs, epilogue, scale):
    a_ref, w_ref = refs[0], refs[1]
    o_ref = refs[-1]
    a = a_ref[...]
    y = jnp.dot(a, w_ref[...].astype(BF16), preferred_element_type=F32)
    if epilogue == "swiglu":
        up = jnp.dot(a, refs[2][...].astype(BF16), preferred_element_type=F32)
        o_ref[...] = (_silu(y) * up).astype(o_ref.dtype)
    elif epilogue == "resid":
        x_ref, gate_ref = refs[2], refs[3]
        o_ref[...] = (x_ref[...] + gate_ref[...] * y).astype(o_ref.dtype)
    elif epilogue == "rope":
        cos, sin = refs[2][...], refs[3][...]
        for h in range(y.shape[1] // HD):
            yh = _rope(y[:, h * HD:(h + 1) * HD], cos, sin)
            o_ref[:, h * HD:(h + 1) * HD] = (yh * scale).astype(o_ref.dtype)
    else:
        o_ref[...] = y.astype(o_ref.dtype)


def _matmul(a, w, widx, *, n_off, n_out, tm, tn, out_dtype, epilogue="plain", up_off=None,
            resid=None, mods=None, gate_index=None, scale=1.0, tables=None, table_index=None,
            name="mm"):
    m, kdim = a.shape
    assert m % tm == 0 and n_out % tn == 0 and n_off % tn == 0
    nb = n_off // tn
    lead = (None,) * len(widx)
    in_specs = [
        pl.BlockSpec((tm, kdim), lambda i, j: (i, 0)),
        pl.BlockSpec(lead + (kdim, tn), lambda i, j: widx + (0, j + nb)),
    ]
    args = [a, w]
    if epilogue == "swiglu":
        ub = up_off // tn
        in_specs.append(pl.BlockSpec(lead + (kdim, tn), lambda i, j: widx + (0, j + ub)))
        args.append(w)
    if epilogue == "resid":
        in_specs.append(pl.BlockSpec((tm, tn), lambda i, j: (i, j)))
        in_specs.append(pl.BlockSpec((None, 1, tn), lambda i, j: (gate_index(i), 0, j)))
        args += [resid, mods]
    if epilogue == "rope":
        in_specs += [pl.BlockSpec((tm, HD), lambda i, j: (table_index(i), 0))] * 2
        args += list(tables)
    return pl.pallas_call(
        functools.partial(_mm_kernel, epilogue=epilogue, scale=scale),
        grid=(m // tm, n_out // tn),
        in_specs=in_specs,
        out_specs=pl.BlockSpec((tm, tn), lambda i, j: (i, j)),
        out_shape=jax.ShapeDtypeStruct((m, n_out), out_dtype),
        compiler_params=_cparams(("parallel", "parallel")), name=name,
    )(*args)


def _rope_tables(length, ident_rows):
    nf = HD // 4
    inv = ROPE_THETA ** (-jnp.arange(nf, dtype=F32) / nf)
    rows = length // GRID_W
    row = jnp.repeat(jnp.arange(rows, dtype=F32), GRID_W)
    col = jnp.tile(jnp.arange(GRID_W, dtype=F32), rows)
    ar = row[:, None] * inv
    ac = col[:, None] * inv
    ang = jnp.concatenate([ar, ar, ac, ac], axis=-1)
    cos, sin = jnp.cos(ang), jnp.sin(ang)
    lane = jnp.arange(HD) % (HD // 2)
    sin_signed = jnp.where(lane >= HD // 4, sin, -sin)
    ones = jnp.ones((ident_rows, HD), F32)
    zeros = jnp.zeros((ident_rows, HD), F32)
    return jnp.concatenate([cos, ones]), jnp.concatenate([sin_signed, zeros])


def _table_block(st, i, tm):
    pt = st.n_prompt // tm
    per = st.sl // tm
    return jnp.where(i < pt, per, (i - pt) % per)


def _rope(y, cos, sin_signed):
    partner = lax.broadcasted_iota(jnp.int32, y.shape, 1) ^ (HD // 4)
    return y * cos + jnp.take_along_axis(y, partner, axis=1) * sin_signed


GQA_PREP_ROWS = 256


def _head_norm(x, g):
    return x * lax.rsqrt(jnp.mean(x * x, axis=-1, keepdims=True) + EPS) * g


def _gqa_kprep_kernel(k_ref, gk_ref, cos_ref, sin_ref, ko_ref):
    cos, sin = cos_ref[...], sin_ref[...]
    for h in range(ATTN_KV_HEADS):
        ko_ref[:, h * HD:(h + 1) * HD] = _rope(_head_norm(k_ref[:, h * HD:(h + 1) * HD], gk_ref[...]),
                                               cos, sin)


def _gqa_kprep(st, qkv, g_k, tables):
    tm = GQA_PREP_ROWS
    qw = ATTN_HEADS * HD
    kw = ATTN_KV_HEADS * HD
    tab = pl.BlockSpec((tm, HD), lambda i: (_table_block(st, i, tm), 0))
    return pl.pallas_call(
        _gqa_kprep_kernel, grid=(st.n // tm,),
        in_specs=[pl.BlockSpec((tm, kw), lambda i: (i, qw // kw)),
                  pl.BlockSpec((1, HD), lambda i: (0, 0)), tab, tab],
        out_specs=pl.BlockSpec((tm, kw), lambda i: (i, 0)),
        out_shape=jax.ShapeDtypeStruct((st.n, kw), F32),
        compiler_params=_cparams(("parallel",)), name="gqa_kprep",
    )(qkv, g_k.reshape(1, HD), *tables)


def _softmax_parts(q, ks):
    ss = [lax.dot_general(q, k, (((1,), (1,)), ((), ())), preferred_element_type=F32) for k in ks]
    m = ss[0].max(axis=-1, keepdims=True)
    for s in ss[1:]:
        m = jnp.maximum(m, s.max(axis=-1, keepdims=True))
    ps = [jnp.exp2(s - m) for s in ss]
    l = ps[0].sum(axis=-1, keepdims=True)
    for p in ps[1:]:
        l = l + p.sum(axis=-1, keepdims=True)
    return ps, l


def _gqa_attn_kernel(q_ref, gq_ref, k_ref, v_ref, *rest, cache, aliased, rope):
    rest = list(rest)
    o_ref = rest.pop()
    if aliased:
        rest.pop()
    if rope:
        cos, sin = rest[0][...], rest[1][...]
        rest = rest[2:]
    if cache:
        ck_ref, cv_ref = rest
        ks = [ck_ref[...].astype(BF16), k_ref[...].astype(BF16)]
        vs = [cv_ref[...].astype(BF16), v_ref[...].astype(BF16)]
    else:
        ks = [k_ref[...].astype(BF16)]
        vs = [v_ref[...].astype(BF16)]
    grp = ATTN_HEADS // ATTN_KV_HEADS
    for h in range(grp):
        y = _head_norm(q_ref[:, h * HD:(h + 1) * HD], gq_ref[...])
        if rope:
            y = _rope(y, cos, sin)
        q = (y * Q_SCALE).astype(BF16)
        ps, l = _softmax_parts(q, ks)
        o = jnp.dot(ps[0].astype(BF16), vs[0], preferred_element_type=F32)
        for p, v in zip(ps[1:], vs[1:]):
            o = o + jnp.dot(p.astype(BF16), v, preferred_element_type=F32)
        o_ref[:, h * HD:(h + 1) * HD] = (o / l).astype(o_ref.dtype)


def _gqa_attn(st, qkv, g_q, k, row0, nb, seq, tables, cache_k, cache_v, layer_j, prev):
    tq = min(1024, seq)
    nq = seq // tq
    grp_w = (ATTN_HEADS // ATTN_KV_HEADS) * HD
    v_col0 = ATTN_HEADS + ATTN_KV_HEADS
    assert row0 % seq == 0, "a segment must start on a multiple of its sequence length"
    b0 = row0 // seq
    q0 = row0 // tq
    in_specs = [
        pl.BlockSpec((tq, grp_w), lambda b, g, i: (q0 + b * nq + i, g)),
        pl.BlockSpec((1, HD), lambda b, g, i: (0, 0)),
        pl.BlockSpec((seq, HD), lambda b, g, i: (b + b0, g)),
        pl.BlockSpec((seq, HD), lambda b, g, i: (b + b0, v_col0 + g)),
    ]
    args = [qkv, g_q.reshape(1, HD), k, qkv]
    if tables is not None:
        in_specs += [pl.BlockSpec((tq, HD), lambda b, g, i: (i, 0))] * 2
        args += list(tables)
    cache = cache_k is not None
    if cache:
        past = cache_k.shape[2]
        ck = cache_k.reshape(cache_k.shape[0], cache_k.shape[1], past, ATTN_KV_HEADS * HD)
        cv = cache_v.reshape(ck.shape)
        in_specs += [pl.BlockSpec((None, None, past, HD), lambda b, g, i: (b, layer_j, 0, g))] * 2
        args += [ck, cv]
    aliases = {}
    if prev is not None:
        in_specs.append(pl.BlockSpec(memory_space=pl.ANY))
        args.append(prev)
        aliases = {len(args) - 1: 0}
    return pl.pallas_call(
        functools.partial(_gqa_attn_kernel, cache=cache, aliased=prev is not None,
                          rope=tables is not None),
        grid=(nb, ATTN_KV_HEADS, nq), in_specs=in_specs,
        out_specs=pl.BlockSpec((tq, grp_w), lambda b, g, i: (q0 + b * nq + i, g)),
        out_shape=jax.ShapeDtypeStruct((st.n, ATTN_HEADS * HD), BF16),
        input_output_aliases=aliases,
        compiler_params=_cparams(("parallel", "parallel", "parallel")), name="gqa_attn",
    )(*args)


def _diff_attn_kernel(lam_ref, q_ref, k_ref, v_ref, g_ref, *rest, cache, out_scale, aliased):
    if aliased:
        rest = rest[:-2] + rest[-1:]
    if cache:
        ck_ref, cv_ref, o_ref = rest
        vs = [cv_ref[...].astype(BF16), v_ref[...].astype(BF16)]
    else:
        (o_ref,) = rest
        vs = [v_ref[...].astype(BF16)]
    lam = lam_ref[0]
    outs = []
    for m in range(2):
        q = q_ref[:, m * HD:(m + 1) * HD]
        ks = [k_ref[:, m * HD:(m + 1) * HD]]
        if cache:
            ks = [ck_ref[:, m * HD:(m + 1) * HD].astype(BF16)] + ks
        ps, l = _softmax_parts(q, ks)
        o = jnp.dot(ps[0].astype(BF16), vs[0], preferred_element_type=F32)
        for p, v in zip(ps[1:], vs[1:]):
            o = o + jnp.dot(p.astype(BF16), v, preferred_element_type=F32)
        outs.append(o / l)
    o = outs[0] - lam * outs[1]
    o = o * lax.rsqrt(jnp.mean(o * o, axis=-1, keepdims=True) + EPS) * g_ref[...]
    o_ref[...] = (o * out_scale).astype(o_ref.dtype)


def _diff_attn(st, lam, q, k, v, g_sub, lam_init, row0, nb, seq, cache_k, cache_v, layer_j, prev):
    tq = min(1024, seq)
    nq = seq // tq
    hw = 2 * HD
    assert row0 % seq == 0, "a segment must start on a multiple of its sequence length"
    b0 = row0 // seq
    q0 = row0 // tq
    in_specs = [
        pl.BlockSpec(memory_space=pltpu.SMEM),
        pl.BlockSpec((tq, hw), lambda b, h, i: (q0 + b * nq + i, h)),
        pl.BlockSpec((seq, hw), lambda b, h, i: (b + b0, h)),
        pl.BlockSpec((seq, hw), lambda b, h, i: (b + b0, h)),
        pl.BlockSpec((1, hw), lambda b, h, i: (0, 0)),
    ]
    args = [lam, q, k, v, g_sub.reshape(1, hw)]
    cache = cache_k is not None
    if cache:
        past = cache_k.shape[2]
        ck = cache_k.reshape(cache_k.shape[0], cache_k.shape[1], past, 2 * DIFF_HEADS * HD)
        cv = cache_v.reshape(ck.shape)
        in_specs += [pl.BlockSpec((None, None, past, hw), lambda b, h, i: (b, layer_j, 0, h))] * 2
        args += [ck, cv]
    aliases = {}
    if prev is not None:
        in_specs.append(pl.BlockSpec(memory_space=pl.ANY))
        args.append(prev)
        aliases = {len(args) - 1: 0}
    return pl.pallas_call(
        functools.partial(_diff_attn_kernel, cache=cache, out_scale=1.0 - lam_init,
                          aliased=prev is not None),
        grid=(nb, DIFF_HEADS, nq), in_specs=in_specs,
        out_specs=pl.BlockSpec((tq, hw), lambda b, h, i: (q0 + b * nq + i, h)),
        out_shape=jax.ShapeDtypeStruct((st.n, DIFF_HEADS * hw), BF16),
        input_output_aliases=aliases,
        compiler_params=_cparams(("parallel", "parallel", "parallel")), name="diff_attn",
    )(*args)


def _conv_kernel(x_ref, w_ref, b_ref, *rest):
    o_ref, pad_ref = rest[-2:]
    seq = x_ref.shape[0]
    half = SSD_CONV // 2
    zeros = jnp.zeros((8, x_ref.shape[1]), F32)
    pad_ref[0:8, :] = zeros
    pad_ref[seq + 8:seq + 16, :] = zeros
    pad_ref[8:seq + 8, :] = x_ref[...]
    xp = pad_ref[...]
    rows = seq + 16
    acc = jnp.broadcast_to(b_ref[...], x_ref.shape)
    for k in range(SSD_CONV):
        shifted = xp if k == half else pltpu.roll(xp, (half - k) % rows, 0)
        acc = acc + w_ref[k:k + 1, :] * shifted[8:8 + seq, :]
    o_ref[...] = _silu(acc)


def _ssd_conv(st, xbc, conv_w, conv_b, row0, nb, seq, prev):
    tc = 512
    assert row0 % seq == 0, "a segment must start on a multiple of its sequence length"
    b0 = row0 // seq
    in_specs = [pl.BlockSpec((seq, tc), lambda b, c: (b + b0, c)),
                pl.BlockSpec((SSD_CONV, tc), lambda b, c: (0, c)),
                pl.BlockSpec((1, tc), lambda b, c: (0, c))]
    args = [xbc, conv_w, conv_b.reshape(1, SSD_CONV_DIM)]
    aliases = {}
    if prev is not None:
        in_specs.append(pl.BlockSpec(memory_space=pl.ANY))
        args.append(prev)
        aliases = {len(args) - 1: 0}
    return pl.pallas_call(
        _conv_kernel, grid=(nb, SSD_CONV_DIM // tc),
        in_specs=in_specs,
        out_specs=pl.BlockSpec((seq, tc), lambda b, c: (b + b0, c)),
        out_shape=jax.ShapeDtypeStruct((st.n, SSD_CONV_DIM), F32),
        scratch_shapes=[pltpu.VMEM((seq + 16, tc), F32)],
        input_output_aliases=aliases,
        compiler_params=_cparams(("parallel", "parallel")), name="ssd_conv",
    )(*args)


def _ssd_scan_kernel(x_ref, b_ref, c_ref, dt_ref, bias_ref, alog_ref, *rest,
                     rev, nc, has_init, want_final, aliased):
    pos = 0
    h0_ref = None
    if has_init:
        h0_ref = rest[pos]; pos += 1
    if aliased:
        pos += 1
    y_ref = rest[pos]; pos += 1
    hf_ref = None
    if want_final:
        hf_ref = rest[pos]; pos += 1
    state_ref, acol_ref, arow_ref, dtrow_ref, wrow_ref, dec_ref = rest[pos:]
    c = pl.program_id(1)
    q = SSD_CHUNK
    lane0 = SSD_HEADS if rev else 0
    r_heads = SSD_HEADS // SSD_GROUPS
    gw = r_heads * SSD_HEADDIM

    ii = lax.broadcasted_iota(jnp.int32, (q, q), 0)
    jj = lax.broadcasted_iota(jnp.int32, (q, q), 1)
    causal = (jj >= ii) if rev else (jj <= ii)

    @pl.when(c == 0)
    def _():
        if has_init:
            for blk in range(SSD_INNER // 128):
                state_ref[:, blk * 128:(blk + 1) * 128] = h0_ref[blk * 128:(blk + 1) * 128, :].T
        else:
            state_ref[...] = jnp.zeros(state_ref.shape, F32)

    xdt = dt_ref[...] + bias_ref[...]
    dt = jnp.maximum(xdt, 0.0) + jnp.log(1.0 + jnp.exp(-jnp.abs(xdt)))
    la = dt * (-jnp.exp(alog_ref[...]))
    tri = causal.astype(BF16)
    tri_t = ((ii >= jj) if rev else (ii <= jj)).astype(BF16)
    la3 = _split3(la)
    acol_ref[...] = (jnp.dot(tri, la3[0], preferred_element_type=F32)
                     + jnp.dot(tri, la3[1], preferred_element_type=F32)
                     + jnp.dot(tri, la3[2], preferred_element_type=F32))
    arow = _dot3(_split3(la.T), tri_t)
    end = 0 if rev else q - 1
    a_end = jnp.broadcast_to(arow[:, end:end + 1], arow.shape)
    dt_t = dt.T
    arow_ref[...] = arow
    dtrow_ref[...] = dt_t
    wrow_ref[...] = jnp.exp(a_end - arow) * dt_t
    dec_ref[...] = jnp.exp(a_end)

    lane_lo = lax.broadcasted_iota(jnp.int32, (1, 128), 1) < SSD_HEADDIM
    zeros_blk = jnp.zeros((SSD_STATE, q), BF16)

    for g in range(SSD_GROUPS):
        bmat = b_ref[:, g * SSD_STATE:(g + 1) * SSD_STATE]
        cmat = c_ref[:, g * SSD_STATE:(g + 1) * SSD_STATE]
        cb = lax.dot_general(cmat.astype(BF16), bmat.astype(BF16), (((1,), (1,)), ((), ())),
                             preferred_element_type=F32)
        b_t = bmat.T
        for pair in range(r_heads // 2):
            col = g * gw + pair * 128
            x_pair = x_ref[:, col:col + 128].astype(BF16)
            st_old = state_ref[:, col:col + 128]
            rhs = jnp.concatenate([x_pair, st_old.astype(BF16)], axis=0)
            res = []
            for sub in range(2):
                hd = lane0 + g * r_heads + 2 * pair + sub
                a_i = jnp.broadcast_to(acol_ref[:, hd:hd + 1], (q, q))
                seg = a_i - arow_ref[hd:hd + 1, :]
                decay = jnp.exp(jnp.where(causal, seg, -jnp.inf))
                m_intra = (cb * decay * dtrow_ref[hd:hd + 1, :]).astype(BF16)
                c_scaled = (cmat * jnp.exp(a_i)).astype(BF16)
                bw_t = (b_t * wrow_ref[hd:hd + 1, :]).astype(BF16)
                lhs = jnp.concatenate(
                    [jnp.concatenate([m_intra, c_scaled], axis=1),
                     jnp.concatenate([bw_t, zeros_blk], axis=1)], axis=0)
                res.append(jnp.dot(lhs, rhs, preferred_element_type=F32))
            hd0 = lane0 + g * r_heads + 2 * pair
            y_ref[:, col:col + 128] = jnp.where(lane_lo, res[0][:q], res[1][:q]).astype(y_ref.dtype)
            dec_pair = jnp.where(lane_lo, dec_ref[hd0:hd0 + 1, :], dec_ref[hd0 + 1:hd0 + 2, :])
            upd = jnp.where(lane_lo, res[0][q:], res[1][q:])
            state_ref[:, col:col + 128] = dec_pair * st_old + upd

    if want_final:
        @pl.when(c == nc - 1)
        def _():
            for blk in range(SSD_INNER // 128):
                hf_ref[blk * 128:(blk + 1) * 128, :] = state_ref[:, blk * 128:(blk + 1) * 128].T


def _ssd_scan(st, xbc, dt, dt_bias, a_log, row0, nb, seq, rev, h0, layer_j, want_final, prev):
    nc = seq // SSD_CHUNK
    c0 = row0 // SSD_CHUNK

    def chunk(b, c):
        return c0 + b * nc + (nc - 1 - c if rev else c)

    in_specs = [
        pl.BlockSpec((SSD_CHUNK, SSD_INNER), lambda b, c: (chunk(b, c), 0)),
        pl.BlockSpec((SSD_CHUNK, SSD_GN), lambda b, c: (chunk(b, c), SSD_INNER // SSD_GN)),
        pl.BlockSpec((SSD_CHUNK, SSD_GN), lambda b, c: (chunk(b, c), SSD_INNER // SSD_GN + 1)),
        pl.BlockSpec((SSD_CHUNK, 128), lambda b, c: (chunk(b, c), 0)),
        pl.BlockSpec((1, 128), lambda b, c: (0, 0)),
        pl.BlockSpec((1, 128), lambda b, c: (0, 0)),
    ]
    args = [xbc, xbc, xbc, dt, dt_bias.reshape(1, 128), a_log.reshape(1, 128)]
    if h0 is not None:
        h0r = h0.reshape(h0.shape[0], h0.shape[1], SSD_INNER, SSD_STATE)
        in_specs.append(pl.BlockSpec((None, None, SSD_INNER, SSD_STATE),
                                     lambda b, c: (b, layer_j, 0, 0)))
        args.append(h0r)
    aliases = {}
    if prev is not None:
        in_specs.append(pl.BlockSpec(memory_space=pl.ANY))
        args.append(prev)
        aliases = {len(args) - 1: 0}
    out_shape = [jax.ShapeDtypeStruct((st.n, SSD_INNER), BF16)]
    out_specs = [pl.BlockSpec((SSD_CHUNK, SSD_INNER), lambda b, c: (chunk(b, c), 0))]
    if want_final:
        out_shape.append(jax.ShapeDtypeStruct((nb, SSD_INNER, SSD_STATE), F32))
        out_specs.append(pl.BlockSpec((None, SSD_INNER, SSD_STATE), lambda b, c: (b, 0, 0)))
    res = pl.pallas_call(
        functools.partial(_ssd_scan_kernel, rev=rev, nc=nc, has_init=h0 is not None,
                          want_final=want_final, aliased=prev is not None),
        grid=(nb, nc), in_specs=in_specs, out_specs=out_specs, out_shape=out_shape,
        scratch_shapes=[pltpu.VMEM((SSD_STATE, SSD_INNER), F32),
                        pltpu.VMEM((SSD_CHUNK, 128), F32)] + [pltpu.VMEM((128, SSD_CHUNK), F32)] * 4,
        input_output_aliases=aliases,
        compiler_params=_cparams(("parallel", "arbitrary")), name="ssd_scan",
    )(*args)
    return res


def _ssd_gate_kernel(yf_ref, yb_ref, x_ref, z_ref, d_ref, g_ref, o_ref):
    gw = SSD_INNER // SSD_GROUPS
    for grp in range(SSD_GROUPS):
        sl = slice(grp * gw, (grp + 1) * gw)
        y = (yf_ref[:, sl].astype(F32) + yb_ref[:, sl].astype(F32)) + d_ref[:, sl] * x_ref[:, sl]
        gated = y * _silu(z_ref[:, sl].astype(F32))
        normed = gated * lax.rsqrt(jnp.mean(gated * gated, axis=-1, keepdims=True) + EPS)
        o_ref[:, sl] = (normed * g_ref[:, sl]).astype(o_ref.dtype)


def _ssd_gate(yf, yb, xbc, z, d_sum, g_norm):
    n = yf.shape[0]
    tm = 256
    row = pl.BlockSpec((tm, SSD_INNER), lambda i: (i, 0))
    vec = pl.BlockSpec((1, SSD_INNER), lambda i: (0, 0))
    return pl.pallas_call(
        _ssd_gate_kernel, grid=(n // tm,),
        in_specs=[row, row, row, row, vec, vec], out_specs=row,
        out_shape=jax.ShapeDtypeStruct((n, SSD_INNER), BF16),
        compiler_params=_cparams(("parallel",)), name="ssd_gate",
    )(yf, yb, xbc, z, d_sum, g_norm.reshape(1, SSD_INNER))


def _moe_up_kernel(be_ref, nu_ref, a_ref, wg_ref, wu_ref, *rest, blk0):
    o_ref = rest[-1]
    m = pl.program_id(1) + blk0

    @pl.when(m < nu_ref[0])
    def _():
        a = a_ref[...]
        gate = jnp.dot(a, wg_ref[...].astype(BF16), preferred_element_type=F32)
        up = jnp.dot(a, wu_ref[...].astype(BF16), preferred_element_type=F32)
        o_ref[...] = (_silu(gate) * up).astype(o_ref.dtype)

    @pl.when(m >= nu_ref[0])
    def _():
        o_ref[...] = jnp.zeros(o_ref.shape, o_ref.dtype)


def _moe_down_kernel(be_ref, nu_ref, a_ref, w_ref, o_ref):
    m = pl.program_id(0)

    @pl.when(m < nu_ref[0])
    def _():
        o_ref[...] = jnp.dot(a_ref[...], w_ref[...].astype(BF16),
                             preferred_element_type=F32).astype(o_ref.dtype)

    @pl.when(m >= nu_ref[0])
    def _():
        o_ref[...] = jnp.zeros(o_ref.shape, o_ref.dtype)


def _moe(st, h, logits, w_gu, w_down, f):
    n = st.n
    tb = MOE_BLOCK
    n_assign = n * TOP_K
    n_blocks = n_assign // tb + N_EXPERTS
    slots = n_blocks * tb
    top_logit, top_e = lax.top_k(logits, TOP_K)
    gates = jax.nn.softmax(top_logit, axis=-1)
    flat_e = top_e.reshape(-1).astype(jnp.int32)
    onehot = (flat_e[:, None] == jnp.arange(N_EXPERTS, dtype=jnp.int32)[None, :]).astype(jnp.int32)
    rank = jnp.sum((jnp.cumsum(onehot, axis=0) - onehot) * onehot, axis=1)
    counts = jnp.sum(onehot, axis=0)
    padded = (counts + tb - 1) // tb * tb
    pad_end = jnp.cumsum(padded)
    pad_start = pad_end - padded
    dest = pad_start[flat_e] + rank
    slot_tok = jnp.zeros((slots,), jnp.int32).at[dest].set(
        jnp.arange(n_assign, dtype=jnp.int32) // TOP_K, unique_indices=True, mode="promise_in_bounds")
    block_start = jnp.arange(n_blocks, dtype=jnp.int32) * tb
    block_e = jnp.minimum(jnp.searchsorted(pad_end, block_start, side="right"),
                          N_EXPERTS - 1).astype(jnp.int32)
    n_used = (pad_end[-1] // tb).astype(jnp.int32).reshape(1)

    tn = 512
    nf = EXPERT_DIM // tn

    def live_block(m, nu):
        return jnp.minimum(m, nu[0] - 1)

    head = min(MOE_HEAD_BLOCKS, n_blocks // 2)
    act = None
    for blk0, nblk in ((0, head), (head, n_blocks - head)):
        xs = h.at[slot_tok[blk0 * tb:(blk0 + nblk) * tb]].get(mode="promise_in_bounds")

        def local(m, nu, blk0=blk0, nblk=nblk):
            return jnp.clip(live_block(m + blk0, nu) - blk0, 0, nblk - 1)

        def expert(m, be, nu, blk0=blk0):
            return be[live_block(m + blk0, nu)]

        in_specs = [
            pl.BlockSpec((tb, D_MODEL), lambda j, m, be, nu, local=local: (local(m, nu), 0)),
            pl.BlockSpec((None, None, D_MODEL, tn),
                         lambda j, m, be, nu, expert=expert: (f, expert(m, be, nu), 0, j)),
            pl.BlockSpec((None, None, D_MODEL, tn),
                         lambda j, m, be, nu, expert=expert: (f, expert(m, be, nu), 0, j + nf)),
        ]
        args = [block_e, n_used, xs, w_gu, w_gu]
        aliases = {}
        if act is not None:
            in_specs.append(pl.BlockSpec(memory_space=pl.ANY))
            args.append(act)
            aliases = {len(args) - 1: 0}
        act = pl.pallas_call(
            functools.partial(_moe_up_kernel, blk0=blk0),
            grid_spec=pltpu.PrefetchScalarGridSpec(
                num_scalar_prefetch=2, grid=(nf, nblk), in_specs=in_specs,
                out_specs=pl.BlockSpec((tb, tn), lambda j, m, be, nu, blk0=blk0: (m + blk0, j)),
            ),
            out_shape=jax.ShapeDtypeStruct((slots, EXPERT_DIM), BF16),
            input_output_aliases=aliases,
            compiler_params=_cparams(("parallel", "arbitrary")), name="moe_up",
        )(*args)

    tn2 = 256
    yb = pl.pallas_call(
        _moe_down_kernel,
        grid_spec=pltpu.PrefetchScalarGridSpec(
            num_scalar_prefetch=2, grid=(n_blocks, D_MODEL // tn2),
            in_specs=[
                pl.BlockSpec((tb, EXPERT_DIM), lambda m, j, be, nu: (live_block(m, nu), 0)),
                pl.BlockSpec((None, None, EXPERT_DIM, tn2),
                             lambda m, j, be, nu: (f, be[live_block(m, nu)], 0, j)),
            ],
            out_specs=pl.BlockSpec((tb, tn2), lambda m, j, be, nu: (m, j)),
        ),
        out_shape=jax.ShapeDtypeStruct((slots, D_MODEL), BF16),
        compiler_params=_cparams(("arbitrary", "arbitrary")), name="moe_down",
    )(block_e, n_used, act, w_down)

    dest2 = dest.reshape(n, TOP_K)
    y0 = yb.at[dest2[:, 0]].get(mode="promise_in_bounds")
    y1 = yb.at[dest2[:, 1]].get(mode="promise_in_bounds")
    return y0, y1, gates


def _combine_kernel(x_ref, y0_ref, y1_ref, g_ref, gate_ref, *rest, norm):
    y = g_ref[:, 0:1] * y0_ref[...].astype(F32) + g_ref[:, 1:2] * y1_ref[...].astype(F32)
    xn = x_ref[...] + gate_ref[...] * y
    if norm:
        gn_ref, shift_ref, scale_ref, o_ref, h_ref = rest
        _norm_mod_store(xn, gn_ref, shift_ref, scale_ref, h_ref)
    else:
        (o_ref,) = rest
    o_ref[...] = xn


def _combine(st, x, y0, y1, gates, mods, layer, g_next=None):
    tm = st.tile(512)
    row = pl.BlockSpec((tm, D_MODEL), lambda i: (i, 0))

    def mod(lyr, which):
        return pl.BlockSpec((None, 1, D_MODEL),
                            lambda i: (_mod_index(lyr, which, st.mod_row(i, tm)), 0, 0))

    in_specs = [row, row, row, pl.BlockSpec((tm, TOP_K), lambda i: (i, 0)), mod(layer, 5)]
    args = [x, y0, y1, gates, mods]
    out_shape = [jax.ShapeDtypeStruct((st.n, D_MODEL), F32)]
    out_specs = [row]
    if g_next is not None:
        in_specs += [pl.BlockSpec((1, D_MODEL), lambda i: (0, 0)), mod(layer + 1, 0), mod(layer + 1, 1)]
        args += [g_next.reshape(1, D_MODEL), mods, mods]
        out_shape.append(jax.ShapeDtypeStruct((st.n, D_MODEL), BF16))
        out_specs.append(row)
    res = pl.pallas_call(
        functools.partial(_combine_kernel, norm=g_next is not None), grid=(st.n // tm,),
        in_specs=in_specs, out_specs=out_specs, out_shape=out_shape,
        compiler_params=_cparams(("parallel",)), name="moe_combine",
    )(*args)
    return res if g_next is not None else (res[0], None)


def kernel(x_prompt, x_sample, cache_attn_k, cache_attn_v, cache_diff_k, cache_diff_v, state_ssd_fwd, state_ssd_bwd, c, c_ctx, w_ada, b_ada, g_mix, g_ffn, g_final, attn_w_in, attn_g_q, attn_g_k, attn_w_out, diff_w_in, diff_lambda, diff_g_sub, diff_w_out, ssd_w_in, ssd_conv_w, ssd_conv_b, ssd_dt_bias, ssd_a_log, ssd_d, ssd_g_norm, ssd_w_out, ffn_w_gu, ffn_w_down, moe_w_router, moe_w_gu, moe_w_down):
    pb, pl_len, _ = x_prompt.shape
    sb, sl_len, _ = x_sample.shape
    st = _Stream(pb, pl_len, sb, sl_len)
    np_, ns = st.n_prompt, st.n_sample
    tm = st.tile(1024)

    x = jnp.concatenate([x_prompt.reshape(np_, D_MODEL), x_sample.reshape(ns, D_MODEL)], axis=0)

    cond = jnp.zeros((MOD_ROWS, D_MODEL), F32).at[:sb].set(c).at[CTX_ROW].set(c_ctx)
    cond_act = jax.nn.silu(cond).astype(BF16)
    mods = []
    for i in range(DEPTH):
        m = _matmul(cond_act, w_ada, (i,), n_off=0, n_out=6 * D_MODEL, tm=MOD_ROWS, tn=1024,
                    out_dtype=F32, name="ada")
        mods.append(m + b_ada[i][None, :])
    mods = jnp.stack(mods).reshape(DEPTH * MOD_ROWS * 6, 1, D_MODEL)

    tables = _rope_tables(sl_len, tm)
    prep_tables = _rope_tables(sl_len, GQA_PREP_ROWS)

    def table_index(i):
        return _table_block(st, i, tm)

    tm2 = st.tile(2048)

    def gate_index(layer, which, tile):
        return lambda i: _mod_index(layer, which, st.mod_row(i, tile))

    new_attn_k, new_attn_v, new_diff_k, new_diff_v, new_ssd_f, new_ssd_b = [], [], [], [], [], []
    def out_proj(o, w16, x, layer, fuse_norm):
        if fuse_norm:
            return _out_proj_norm(st, o, w16, x, mods, layer, g_ffn[layer])
        xo = _matmul(o, w16, (), n_off=0, n_out=D_MODEL, tm=tm2, tn=512,
                     out_dtype=F32, epilogue="resid", resid=x, mods=mods,
                     gate_index=gate_index(layer, 2, tm2), name="attn_out")
        return xo, None

    h_pre = None
    for i in range(DEPTH):
        kind, j = i % 3, i // 3
        f = i // 2
        w_router = moe_w_router[f] if i % 2 == 1 else None
        h = h_pre if h_pre is not None else _norm_mod(st, x, g_mix[i], mods, i, 0)
        h_pre = hf = logits = None
        if kind == 0:
            qkv = _matmul(h, attn_w_in, (j,), n_off=0, n_out=attn_w_in.shape[-1], tm=tm, tn=1024,
                          out_dtype=F32, name="attn_in")
            k = _gqa_kprep(st, qkv, attn_g_k[j], prep_tables)
            o = _gqa_attn(st, qkv, attn_g_q[j], k, 0, pb, pl_len, None, None, None, j, None)
            o = _gqa_attn(st, qkv, attn_g_q[j], k, np_, sb, sl_len, tables,
                          cache_attn_k, cache_attn_v, j, o)
            new_attn_k.append(k[:np_].reshape(pb, pl_len, ATTN_KV_HEADS, HD))
            new_attn_v.append(qkv[:np_, (ATTN_HEADS + ATTN_KV_HEADS) * HD:]
                              .reshape(pb, pl_len, ATTN_KV_HEADS, HD))
            x, hf = out_proj(o, attn_w_out[j].astype(BF16), x, i, w_router is None)
        elif kind == 1:
            lam_init = 0.8 - 0.6 * math.exp(-0.3 * i)
            lf = diff_lambda[j]
            lam = (jnp.exp(jnp.sum(lf[0] * lf[1])) - jnp.exp(jnp.sum(lf[2] * lf[3])) + lam_init).reshape(1)
            w = 2 * DIFF_HEADS * HD
            q = _matmul(h, diff_w_in, (j,), n_off=0, n_out=w, tm=tm, tn=1024, out_dtype=BF16,
                        epilogue="rope", scale=Q_SCALE,
                        tables=tables, table_index=table_index, name="diff_in_q")
            k = _matmul(h, diff_w_in, (j,), n_off=w, n_out=w, tm=tm, tn=1024, out_dtype=BF16,
                        epilogue="rope", tables=tables, table_index=table_index, name="diff_in_k")
            v = _matmul(h, diff_w_in, (j,), n_off=2 * w, n_out=w, tm=tm, tn=1024, out_dtype=F32,
                        name="diff_in_v")
            k_ctx = _matmul(h[:np_], diff_w_in, (j,), n_off=w, n_out=w, tm=tm, tn=1024,
                            out_dtype=F32, name="diff_in_kctx")
            o = _diff_attn(st, lam, q, k, v, diff_g_sub[j], lam_init, 0, pb, pl_len, None, None, j, None)
            o = _diff_attn(st, lam, q, k, v, diff_g_sub[j], lam_init, np_, sb, sl_len,
                           cache_diff_k, cache_diff_v, j, o)
            new_diff_k.append(k_ctx.reshape(pb, pl_len, 2 * DIFF_HEADS, HD))
            new_diff_v.append(v[:np_].reshape(pb, pl_len, DIFF_HEADS, 2 * HD))
            x, hf = out_proj(o, diff_w_out[j].astype(BF16), x, i, w_router is None)
        else:
            z = _matmul(h, ssd_w_in, (j,), n_off=0, n_out=SSD_INNER, tm=tm, tn=1024,
                        out_dtype=BF16, name="ssd_in_z")
            xbc = _matmul(h, ssd_w_in, (j,), n_off=SSD_INNER, n_out=SSD_CONV_DIM, tm=tm, tn=1024,
                          out_dtype=F32, name="ssd_in_xbc")
            dt = _matmul(h, ssd_w_in, (j,), n_off=SSD_INNER + SSD_CONV_DIM, n_out=2 * SSD_HEADS,
                         tm=tm, tn=128, out_dtype=F32, name="ssd_in_dt")
            xc = _ssd_conv(st, xbc, ssd_conv_w[j], ssd_conv_b[j], 0, pb, pl_len, None)
            xc = _ssd_conv(st, xbc, ssd_conv_w[j], ssd_conv_b[j], np_, sb, sl_len, xc)
            ys = []
            for rev, h0 in ((False, state_ssd_fwd), (True, state_ssd_bwd)):
                y, hfin = _ssd_scan(st, xc, dt, ssd_dt_bias[j], ssd_a_log[j], 0, pb, pl_len, rev,
                                    None, j, True, None)
                (y,) = _ssd_scan(st, xc, dt, ssd_dt_bias[j], ssd_a_log[j], np_, sb, sl_len, rev,
                                 h0, j, False, y)
                ys.append(y)
                (new_ssd_b if rev else new_ssd_f).append(
                    hfin.reshape(pb, SSD_HEADS, SSD_HEADDIM, SSD_STATE))
            d_sum = jnp.repeat(ssd_d[j][0] + ssd_d[j][1], SSD_HEADDIM).reshape(1, SSD_INNER)
            o = _ssd_gate(ys[0], ys[1], xc, z, d_sum, ssd_g_norm[j])
            x = _matmul(o, ssd_w_out[j].astype(BF16), (), n_off=0, n_out=D_MODEL, tm=tm, tn=512,
                        out_dtype=F32, epilogue="resid", resid=x, mods=mods,
                        gate_index=gate_index(i, 2, tm), name="ssd_out")

        if hf is None:
            if w_router is not None:
                hf, logits = _norm_mod(st, x, g_ffn[i], mods, i, 3, w_router=w_router)
            else:
                hf = _norm_mod(st, x, g_ffn[i], mods, i, 3)
        if i % 2 == 0:
            act = _matmul(hf, ffn_w_gu, (f,), n_off=0, n_out=FFN_DIM, tm=tm, tn=512,
                          out_dtype=BF16, epilogue="swiglu", up_off=FFN_DIM, name="ffn_up")
            x = _matmul(act, ffn_w_down[f].astype(BF16), (), n_off=0, n_out=D_MODEL, tm=tm, tn=512,
                        out_dtype=F32, epilogue="resid", resid=x, mods=mods,
                        gate_index=gate_index(i, 5, tm), name="ffn_down")
        else:
            y0, y1, gates = _moe(st, hf, logits[:, :N_EXPERTS], moe_w_gu, moe_w_down, f)
            x, h_pre = _combine(st, x, y0, y1, gates, mods, i,
                                g_next=g_mix[i + 1] if i + 1 < DEPTH else None)

    y_prompt = _final_norm(x, g_final, 0, np_).reshape(pb, pl_len, D_MODEL)
    y_sample = _final_norm(x, g_final, np_, ns).reshape(sb, sl_len, D_MODEL)
    return (y_prompt, y_sample,
            jnp.stack(new_attn_k, axis=1), jnp.stack(new_attn_v, axis=1),
            jnp.stack(new_diff_k, axis=1), jnp.stack(new_diff_v, axis=1),
            jnp.stack(new_ssd_f, axis=1), jnp.stack(new_ssd_b, axis=1))
```

```python
import functools
import math

import jax
import jax.numpy as jnp
from jax import lax
from jax.experimental import pallas as pl
from jax.experimental.pallas import tpu as pltpu

F32 = jnp.float32
BF16 = jnp.bfloat16

D_MODEL = 2048
DEPTH = 4
GRID_W = 64
ROPE_THETA = 10000.0
EPS = 1e-6
HD = 128
Q_SCALE = math.log2(math.e) / math.sqrt(HD)
ATTN_HEADS = 16
ATTN_KV_HEADS = 4
DIFF_HEADS = 8
SSD_INNER = 2 * D_MODEL
SSD_HEADDIM = 64
SSD_HEADS = SSD_INNER // SSD_HEADDIM
SSD_GROUPS = 8
SSD_STATE = 128
SSD_CONV = 5
SSD_CHUNK = 128
SSD_GN = SSD_GROUPS * SSD_STATE
SSD_CONV_DIM = SSD_INNER + 2 * SSD_GN
FFN_DIM = 5632
N_EXPERTS = 8
TOP_K = 2
EXPERT_DIM = 7168

MOD_ROWS = 16
CTX_ROW = 8
MOE_BLOCK = 1024
MOE_HEAD_BLOCKS = 8
VMEM_LIMIT = 56 * 1024 * 1024


def _cparams(sem):
    return pltpu.CompilerParams(dimension_semantics=sem, vmem_limit_bytes=VMEM_LIMIT)


def _silu(x):
    return x / (1.0 + jnp.exp(-x))


def _split3(x):
    hi = x.astype(BF16)
    r = x - hi.astype(F32)
    mid = r.astype(BF16)
    lo = (r - mid.astype(F32)).astype(BF16)
    return hi, mid, lo


def _dot3(a3, b):
    out = jnp.dot(a3[0], b, preferred_element_type=F32)
    out = out + jnp.dot(a3[1], b, preferred_element_type=F32)
    return out + jnp.dot(a3[2], b, preferred_element_type=F32)


class _Stream:
    def __init__(self, n_prompt_batch, prompt_len, n_sample_batch, sample_len):
        self.pb, self.pl_, self.sb, self.sl = n_prompt_batch, prompt_len, n_sample_batch, sample_len
        self.n_prompt = n_prompt_batch * prompt_len
        self.n_sample = n_sample_batch * sample_len
        self.n = self.n_prompt + self.n_sample

    def mod_row(self, i, tm):
        pt = self.n_prompt // tm
        per = self.sl // tm
        return jnp.where(i < pt, CTX_ROW, (i - pt) // per)

    def tile(self, want):
        t = want
        while self.n_prompt % t or self.sl % t:
            t //= 2
        return t


def _mod_index(layer, which, row):
    return (layer * MOD_ROWS + row) * 6 + which


def _norm_mod_store(x, g_ref, shift_ref, scale_ref, o_ref, wr_ref=None, lg_ref=None):
    y = x * lax.rsqrt(jnp.mean(x * x, axis=-1, keepdims=True) + EPS) * g_ref[...]
    h = y * (1.0 + scale_ref[...]) + shift_ref[...]
    if wr_ref is not None:
        w = wr_ref[...]
        whi = w.astype(BF16)
        wlo = (w - whi.astype(F32)).astype(BF16)
        hhi = h.astype(BF16)
        hlo = (h - hhi.astype(F32)).astype(BF16)
        lg = jnp.dot(hhi, whi, preferred_element_type=F32)
        lg = lg + jnp.dot(hlo, whi, preferred_element_type=F32)
        lg_ref[...] = lg + jnp.dot(hhi, wlo, preferred_element_type=F32)
    o_ref[...] = h.astype(o_ref.dtype)


def _norm_mod_kernel(x_ref, g_ref, shift_ref, scale_ref, *rest, router):
    if router:
        wr_ref, o_ref, lg_ref = rest
        _norm_mod_store(x_ref[...], g_ref, shift_ref, scale_ref, o_ref, wr_ref, lg_ref)
    else:
        (o_ref,) = rest
        _norm_mod_store(x_ref[...], g_ref, shift_ref, scale_ref, o_ref)


def _router_pad(w_router):
    return jnp.zeros((D_MODEL, 128), F32).at[:, :N_EXPERTS].set(w_router)


def _out_norm_kernel(a_ref, w_ref, x_ref, gate_ref, g_ref, shift_ref, scale_ref, xo_ref, ho_ref):
    xn = x_ref[...] + gate_ref[...] * jnp.dot(a_ref[...], w_ref[...], preferred_element_type=F32)
    _norm_mod_store(xn, g_ref, shift_ref, scale_ref, ho_ref)
    xo_ref[...] = xn


def _out_proj_norm(st, a, w, x, mods, layer, g_next):
    tm = st.tile(512)
    kdim = a.shape[1]

    def mod(which):
        return pl.BlockSpec((None, 1, D_MODEL),
                            lambda i: (_mod_index(layer, which, st.mod_row(i, tm)), 0, 0))

    row = pl.BlockSpec((tm, D_MODEL), lambda i: (i, 0))
    in_specs = [pl.BlockSpec((tm, kdim), lambda i: (i, 0)),
                pl.BlockSpec((kdim, D_MODEL), lambda i: (0, 0), pipeline_mode=pl.Buffered(1)),
                row, mod(2), pl.BlockSpec((1, D_MODEL), lambda i: (0, 0)), mod(3), mod(4)]
    args = [a, w, x, mods, g_next.reshape(1, D_MODEL), mods, mods]
    out_shape = [jax.ShapeDtypeStruct((st.n, D_MODEL), F32), jax.ShapeDtypeStruct((st.n, D_MODEL), BF16)]
    return pl.pallas_call(
        _out_norm_kernel,
        grid=(st.n // tm,), in_specs=in_specs, out_specs=[row, row], out_shape=out_shape,
        compiler_params=_cparams(("parallel",)), name="out_proj_norm",
    )(*args)


def _norm_mod(st, x, g, mods, layer, which_shift, w_router=None):
    tm = st.tile(512)
    nt = st.n // tm
    g2 = g.reshape(1, D_MODEL)
    in_specs = [
        pl.BlockSpec((tm, D_MODEL), lambda i: (i, 0)),
        pl.BlockSpec((1, D_MODEL), lambda i: (0, 0)),
        pl.BlockSpec((None, 1, D_MODEL),
                     lambda i: (_mod_index(layer, which_shift, st.mod_row(i, tm)), 0, 0)),
        pl.BlockSpec((None, 1, D_MODEL),
                     lambda i: (_mod_index(layer, which_shift + 1, st.mod_row(i, tm)), 0, 0)),
    ]
    args = [x, g2, mods, mods]
    out_shape = [jax.ShapeDtypeStruct((st.n, D_MODEL), BF16)]
    out_specs = [pl.BlockSpec((tm, D_MODEL), lambda i: (i, 0))]
    if w_router is not None:
        in_specs.append(pl.BlockSpec((D_MODEL, 128), lambda i: (0, 0)))
        args.append(_router_pad(w_router))
        out_shape.append(jax.ShapeDtypeStruct((st.n, 128), F32))
        out_specs.append(pl.BlockSpec((tm, 128), lambda i: (i, 0)))
    res = pl.pallas_call(
        functools.partial(_norm_mod_kernel, router=w_router is not None),
        grid=(nt,), in_specs=in_specs, out_specs=out_specs, out_shape=out_shape,
        compiler_params=_cparams(("parallel",)), name="norm_mod",
    )(*args)
    return res if w_router is not None else res[0]


def _final_norm_kernel(x_ref, g_ref, o_ref):
    x = x_ref[...]
    o_ref[...] = x * lax.rsqrt(jnp.mean(x * x, axis=-1, keepdims=True) + EPS) * g_ref[...]


def _final_norm(x, g, row0, nrows):
    tm = 512
    r0 = row0 // tm
    return pl.pallas_call(
        _final_norm_kernel, grid=(nrows // tm,),
        in_specs=[pl.BlockSpec((tm, D_MODEL), lambda i: (i + r0, 0)),
                  pl.BlockSpec((1, D_MODEL), lambda i: (0, 0))],
        out_specs=pl.BlockSpec((tm, D_MODEL), lambda i: (i, 0)),
        out_shape=jax.ShapeDtypeStruct((nrows, D_MODEL), F32),
        compiler_params=_cparams(("parallel",)), name="final_norm",
    )(x, g.reshape(1, D_MODEL))


def _mm_kernel(*refs, epilogue, scale):
    a_ref, w_ref = refs[0], refs[1]
    o_ref = refs[-1]
    a = a_ref[...]
    y = jnp.dot(a, w_ref[...].astype(BF16), preferred_element_type=F32)
    if epilogue == "swiglu":
        up = jnp.dot(a, refs[2][...].astype(BF16), preferred_element_type=F32)
        o_ref[...] = (_silu(y) * up).astype(o_ref.dtype)
    elif epilogue == "resid":
        x_ref, gate_ref = refs[2], refs[3]
        o_ref[...] = (x_ref[...] + gate_ref[...] * y).astype(o_ref.dtype)
    elif epilogue == "rope":
        cos, sin = refs[2][...], refs[3][...]
        for h in range(y.shape[1] // HD):
            yh = _rope(y[:, h * HD:(h + 1) * HD], cos, sin)
            o_ref[:, h * HD:(h + 1) * HD] = (yh * scale).astype(o_ref.dtype)
    else:
        o_ref[...] = y.astype(o_ref.dtype)


def _matmul(a, w, widx, *, n_off, n_out, tm, tn, out_dtype, epilogue="plain", up_off=None,
            resid=None, mods=None, gate_index=None, scale=1.0, tables=None, table_index=None,
            name="mm"):
    m, kdim = a.shape
    assert m % tm == 0 and n_out % tn == 0 and n_off % tn == 0
    nb = n_off // tn
    lead = (None,) * len(widx)
    in_specs = [
        pl.BlockSpec((tm, kdim), lambda i, j: (i, 0)),
        pl.BlockSpec(lead + (kdim, tn), lambda i, j: widx + (0, j + nb)),
    ]
    args = [a, w]
    if epilogue == "swiglu":
        ub = up_off // tn
        in_specs.append(pl.BlockSpec(lead + (kdim, tn), lambda i, j: widx + (0, j + ub)))
        args.append(w)
    if epilogue == "resid":
        in_specs.append(pl.BlockSpec((tm, tn), lambda i, j: (i, j)))
        in_specs.append(pl.BlockSpec((None, 1, tn), lambda i, j: (gate_index(i), 0, j)))
        args += [resid, mods]
    if epilogue == "rope":
        in_specs += [pl.BlockSpec((tm, HD), lambda i, j: (table_index(i), 0))] * 2
        args += list(tables)
    return pl.pallas_call(
        functools.partial(_mm_kernel, epilogue=epilogue, scale=scale),
        grid=(m // tm, n_out // tn),
        in_specs=in_specs,
        out_specs=pl.BlockSpec((tm, tn), lambda i, j: (i, j)),
        out_shape=jax.ShapeDtypeStruct((m, n_out), out_dtype),
        compiler_params=_cparams(("parallel", "parallel")), name=name,
    )(*args)


def _rope_tables(length, ident_rows):
    nf = HD // 4
    inv = ROPE_THETA ** (-jnp.arange(nf, dtype=F32) / nf)
    rows = length // GRID_W
    row = jnp.repeat(jnp.arange(rows, dtype=F32), GRID_W)
    col = jnp.tile(jnp.arange(GRID_W, dtype=F32), rows)
    ar = row[:, None] * inv
    ac = col[:, None] * inv
    ang = jnp.concatenate([ar, ar, ac, ac], axis=-1)
    cos, sin = jnp.cos(ang), jnp.sin(ang)
    lane = jnp.arange(HD) % (HD // 2)
    sin_signed = jnp.where(lane >= HD // 4, sin, -sin)
    ones = jnp.ones((ident_rows, HD), F32)
    zeros = jnp.zeros((ident_rows, HD), F32)
    return jnp.concatenate([cos, ones]), jnp.concatenate([sin_signed, zeros])


def _table_block(st, i, tm):
    pt = st.n_prompt // tm
    per = st.sl // tm
    return jnp.where(i < pt, per, (i - pt) % per)


def _rope(y, cos, sin_signed):
    partner = lax.broadcasted_iota(jnp.int32, y.shape, 1) ^ (HD // 4)
    return y * cos + jnp.take_along_axis(y, partner, axis=1) * sin_signed


GQA_PREP_ROWS = 256


def _head_norm(x, g):
    return x * lax.rsqrt(jnp.mean(x * x, axis=-1, keepdims=True) + EPS) * g


def _gqa_kprep_kernel(k_ref, gk_ref, cos_ref, sin_ref, ko_ref):
    cos, sin = cos_ref[...], sin_ref[...]
    for h in range(ATTN_KV_HEADS):
        ko_ref[:, h * HD:(h + 1) * HD] = _rope(_head_norm(k_ref[:, h * HD:(h + 1) * HD], gk_ref[...]),
                                               cos, sin)


def _gqa_kprep(st, qkv, g_k, tables):
    tm = GQA_PREP_ROWS
    qw = ATTN_HEADS * HD
    kw = ATTN_KV_HEADS * HD
    tab = pl.BlockSpec((tm, HD), lambda i: (_table_block(st, i, tm), 0))
    return pl.pallas_call(
        _gqa_kprep_kernel, grid=(st.n // tm,),
        in_specs=[pl.BlockSpec((tm, kw), lambda i: (i, qw // kw)),
                  pl.BlockSpec((1, HD), lambda i: (0, 0)), tab, tab],
        out_specs=pl.BlockSpec((tm, kw), lambda i: (i, 0)),
        out_shape=jax.ShapeDtypeStruct((st.n, kw), F32),
        compiler_params=_cparams(("parallel",)), name="gqa_kprep",
    )(qkv, g_k.reshape(1, HD), *tables)


def _softmax_parts(q, ks):
    ss = [lax.dot_general(q, k, (((1,), (1,)), ((), ())), preferred_element_type=F32) for k in ks]
    m = ss[0].max(axis=-1, keepdims=True)
    for s in ss[1:]:
        m = jnp.maximum(m, s.max(axis=-1, keepdims=True))
    ps = [jnp.exp2(s - m) for s in ss]
    l = ps[0].sum(axis=-1, keepdims=True)
    for p in ps[1:]:
        l = l + p.sum(axis=-1, keepdims=True)
    return ps, l


def _gqa_attn_kernel(q_ref, gq_ref, k_ref, v_ref, *rest, cache, aliased, rope):
    rest = list(rest)
    o_ref = rest.pop()
    if aliased:
        rest.pop()
    if rope:
        cos, sin = rest[0][...], rest[1][...]
        rest = rest[2:]
    if cache:
        ck_ref, cv_ref = rest
        ks = [ck_ref[...].astype(BF16), k_ref[...].astype(BF16)]
        vs = [cv_ref[...].astype(BF16), v_ref[...].astype(BF16)]
    else:
        ks = [k_ref[...].astype(BF16)]
        vs = [v_ref[...].astype(BF16)]
    grp = ATTN_HEADS // ATTN_KV_HEADS
    for h in range(grp):
        y = _head_norm(q_ref[:, h * HD:(h + 1) * HD], gq_ref[...])
        if rope:
            y = _rope(y, cos, sin)
        q = (y * Q_SCALE).astype(BF16)
        ps, l = _softmax_parts(q, ks)
        o = jnp.dot(ps[0].astype(BF16), vs[0], preferred_element_type=F32)
        for p, v in zip(ps[1:], vs[1:]):
            o = o + jnp.dot(p.astype(BF16), v, preferred_element_type=F32)
        o_ref[:, h * HD:(h + 1) * HD] = (o / l).astype(o_ref.dtype)


def _gqa_attn(st, qkv, g_q, k, row0, nb, seq, tables, cache_k, cache_v, layer_j, prev):
    tq = min(1024, seq)
    nq = seq // tq
    grp_w = (ATTN_HEADS // ATTN_KV_HEADS) * HD
    v_col0 = ATTN_HEADS + ATTN_KV_HEADS
    assert row0 % seq == 0, "a segment must start on a multiple of its sequence length"
    b0 = row0 // seq
    q0 = row0 // tq
    in_specs = [
        pl.BlockSpec((tq, grp_w), lambda b, g, i: (q0 + b * nq + i, g)),
        pl.BlockSpec((1, HD), lambda b, g, i: (0, 0)),
        pl.BlockSpec((seq, HD), lambda b, g, i: (b + b0, g)),
        pl.BlockSpec((seq, HD), lambda b, g, i: (b + b0, v_col0 + g)),
    ]
    args = [qkv, g_q.reshape(1, HD), k, qkv]
    if tables is not None:
        in_specs += [pl.BlockSpec((tq, HD), lambda b, g, i: (i, 0))] * 2
        args += list(tables)
    cache = cache_k is not None
    if cache:
        past = cache_k.shape[2]
        ck = cache_k.reshape(cache_k.shape[0], cache_k.shape[1], past, ATTN_KV_HEADS * HD)
        cv = cache_v.reshape(ck.shape)
        in_specs += [pl.BlockSpec((None, None, past, HD), lambda b, g, i: (b, layer_j, 0, g))] * 2
        args += [ck, cv]
    aliases = {}
    if prev is not None:
        in_specs.append(pl.BlockSpec(memory_space=pl.ANY))
        args.append(prev)
        aliases = {len(args) - 1: 0}
    return pl.pallas_call(
        functools.partial(_gqa_attn_kernel, cache=cache, aliased=prev is not None,
                          rope=tables is not None),
        grid=(nb, ATTN_KV_HEADS, nq), in_specs=in_specs,
        out_specs=pl.BlockSpec((tq, grp_w), lambda b, g, i: (q0 + b * nq + i, g)),
        out_shape=jax.ShapeDtypeStruct((st.n, ATTN_HEADS * HD), BF16),
        input_output_aliases=aliases,
        compiler_params=_cparams(("parallel", "parallel", "parallel")), name="gqa_attn",
    )(*args)


def _diff_attn_kernel(lam_ref, q_ref, k_ref, v_ref, g_ref, *rest, cache, out_scale, aliased):
    if aliased:
        rest = rest[:-2] + rest[-1:]
    if cache:
        ck_ref, cv_ref, o_ref = rest
        vs = [cv_ref[...].astype(BF16), v_ref[...].astype(BF16)]
    else:
        (o_ref,) = rest
        vs = [v_ref[...].astype(BF16)]
    lam = lam_ref[0]
    outs = []
    for m in range(2):
        q = q_ref[:, m * HD:(m + 1) * HD]
        ks = [k_ref[:, m * HD:(m + 1) * HD]]
        if cache:
            ks = [ck_ref[:, m * HD:(m + 1) * HD].astype(BF16)] + ks
        ps, l = _softmax_parts(q, ks)
        o = jnp.dot(ps[0].astype(BF16), vs[0], preferred_element_type=F32)
        for p, v in zip(ps[1:], vs[1:]):
            o = o + jnp.dot(p.astype(BF16), v, preferred_element_type=F32)
        outs.append(o / l)
    o = outs[0] - lam * outs[1]
    o = o * lax.rsqrt(jnp.mean(o * o, axis=-1, keepdims=True) + EPS) * g_ref[...]
    o_ref[...] = (o * out_scale).astype(o_ref.dtype)


def _diff_attn(st, lam, q, k, v, g_sub, lam_init, row0, nb, seq, cache_k, cache_v, layer_j, prev):
    tq = min(1024, seq)
    nq = seq // tq
    hw = 2 * HD
    assert row0 % seq == 0, "a segment must start on a multiple of its sequence length"
    b0 = row0 // seq
    q0 = row0 // tq
    in_specs = [
        pl.BlockSpec(memory_space=pltpu.SMEM),
        pl.BlockSpec((tq, hw), lambda b, h, i: (q0 + b * nq + i, h)),
        pl.BlockSpec((seq, hw), lambda b, h, i: (b + b0, h)),
        pl.BlockSpec((seq, hw), lambda b, h, i: (b + b0, h)),
        pl.BlockSpec((1, hw), lambda b, h, i: (0, 0)),
    ]
    args = [lam, q, k, v, g_sub.reshape(1, hw)]
    cache = cache_k is not None
    if cache:
        past = cache_k.shape[2]
        ck = cache_k.reshape(cache_k.shape[0], cache_k.shape[1], past, 2 * DIFF_HEADS * HD)
        cv = cache_v.reshape(ck.shape)
        in_specs += [pl.BlockSpec((None, None, past, hw), lambda b, h, i: (b, layer_j, 0, h))] * 2
        args += [ck, cv]
    aliases = {}
    if prev is not None:
        in_specs.append(pl.BlockSpec(memory_space=pl.ANY))
        args.append(prev)
        aliases = {len(args) - 1: 0}
    return pl.pallas_call(
        functools.partial(_diff_attn_kernel, cache=cache, out_scale=1.0 - lam_init,
                          aliased=prev is not None),
        grid=(nb, DIFF_HEADS, nq), in_specs=in_specs,
        out_specs=pl.BlockSpec((tq, hw), lambda b, h, i: (q0 + b * nq + i, h)),
        out_shape=jax.ShapeDtypeStruct((st.n, DIFF_HEADS * hw), BF16),
        input_output_aliases=aliases,
        compiler_params=_cparams(("parallel", "parallel", "parallel")), name="diff_attn",
    )(*args)


def _conv_kernel(x_ref, w_ref, b_ref, *rest):
    o_ref, pad_ref = rest[-2:]
    seq = x_ref.shape[0]
    half = SSD_CONV // 2
    zeros = jnp.zeros((8, x_ref.shape[1]), F32)
    pad_ref[0:8, :] = zeros
    pad_ref[seq + 8:seq + 16, :] = zeros
    pad_ref[8:seq + 8, :] = x_ref[...]
    xp = pad_ref[...]
    rows = seq + 16
    acc = jnp.broadcast_to(b_ref[...], x_ref.shape)
    for k in range(SSD_CONV):
        shifted = xp if k == half else pltpu.roll(xp, (half - k) % rows, 0)
        acc = acc + w_ref[k:k + 1, :] * shifted[8:8 + seq, :]
    o_ref[...] = _silu(acc)


def _ssd_conv(st, xbc, conv_w, conv_b, row0, nb, seq, prev):
    tc = 512
    assert row0 % seq == 0, "a segment must start on a multiple of its sequence length"
    b0 = row0 // seq
    in_specs = [pl.BlockSpec((seq, tc), lambda b, c: (b + b0, c)),
                pl.BlockSpec((SSD_CONV, tc), lambda b, c: (0, c)),
                pl.BlockSpec((1, tc), lambda b, c: (0, c))]
    args = [xbc, conv_w, conv_b.reshape(1, SSD_CONV_DIM)]
    aliases = {}
    if prev is not None:
        in_specs.append(pl.BlockSpec(memory_space=pl.ANY))
        args.append(prev)
        aliases = {len(args) - 1: 0}
    return pl.pallas_call(
        _conv_kernel, grid=(nb, SSD_CONV_DIM // tc),
        in_specs=in_specs,
        out_specs=pl.BlockSpec((seq, tc), lambda b, c: (b + b0, c)),
        out_shape=jax.ShapeDtypeStruct((st.n, SSD_CONV_DIM), F32),
        scratch_shapes=[pltpu.VMEM((seq + 16, tc), F32)],
        input_output_aliases=aliases,
        compiler_params=_cparams(("parallel", "parallel")), name="ssd_conv",
    )(*args)


def _ssd_scan_kernel(x_ref, b_ref, c_ref, dt_ref, bias_ref, alog_ref, *rest,
                     rev, nc, has_init, want_final, aliased):
    pos = 0
    h0_ref = None
    if has_init:
        h0_ref = rest[pos]; pos += 1
    if aliased:
        pos += 1
    y_ref = rest[pos]; pos += 1
    hf_ref = None
    if want_final:
        hf_ref = rest[pos]; pos += 1
    state_ref, acol_ref, arow_ref, dtrow_ref, wrow_ref, dec_ref = rest[pos:]
    c = pl.program_id(1)
    q = SSD_CHUNK
    lane0 = SSD_HEADS if rev else 0
    r_heads = SSD_HEADS // SSD_GROUPS
    gw = r_heads * SSD_HEADDIM

    ii = lax.broadcasted_iota(jnp.int32, (q, q), 0)
    jj = lax.broadcasted_iota(jnp.int32, (q, q), 1)
    causal = (jj >= ii) if rev else (jj <= ii)

    @pl.when(c == 0)
    def _():
        if has_init:
            for blk in range(SSD_INNER // 128):
                state_ref[:, blk * 128:(blk + 1) * 128] = h0_ref[blk * 128:(blk + 1) * 128, :].T
        else:
            state_ref[...] = jnp.zeros(state_ref.shape, F32)

    xdt = dt_ref[...] + bias_ref[...]
    dt = jnp.maximum(xdt, 0.0) + jnp.log(1.0 + jnp.exp(-jnp.abs(xdt)))
    la = dt * (-jnp.exp(alog_ref[...]))
    tri = causal.astype(BF16)
    tri_t = ((ii >= jj) if rev else (ii <= jj)).astype(BF16)
    la3 = _split3(la)
    acol_ref[...] = (jnp.dot(tri, la3[0], preferred_element_type=F32)
                     + jnp.dot(tri, la3[1], preferred_element_type=F32)
                     + jnp.dot(tri, la3[2], preferred_element_type=F32))
    arow = _dot3(_split3(la.T), tri_t)
    end = 0 if rev else q - 1
    a_end = jnp.broadcast_to(arow[:, end:end + 1], arow.shape)
    dt_t = dt.T
    arow_ref[...] = arow
    dtrow_ref[...] = dt_t
    wrow_ref[...] = jnp.exp(a_end - arow) * dt_t
    dec_ref[...] = jnp.exp(a_end)

    lane_lo = lax.broadcasted_iota(jnp.int32, (1, 128), 1) < SSD_HEADDIM
    zeros_blk = jnp.zeros((SSD_STATE, q), BF16)

    for g in range(SSD_GROUPS):
        bmat = b_ref[:, g * SSD_STATE:(g + 1) * SSD_STATE]
        cmat = c_ref[:, g * SSD_STATE:(g + 1) * SSD_STATE]
        cb = lax.dot_general(cmat.astype(BF16), bmat.astype(BF16), (((1,), (1,)), ((), ())),
                             preferred_element_type=F32)
        b_t = bmat.T
        for pair in range(r_heads // 2):
            col = g * gw + pair * 128
            x_pair = x_ref[:, col:col + 128].astype(BF16)
            st_old = state_ref[:, col:col + 128]
            rhs = jnp.concatenate([x_pair, st_old.astype(BF16)], axis=0)
            res = []
            for sub in range(2):
                hd = lane0 + g * r_heads + 2 * pair + sub
                a_i = jnp.broadcast_to(acol_ref[:, hd:hd + 1], (q, q))
                seg = a_i - arow_ref[hd:hd + 1, :]
                decay = jnp.exp(jnp.where(causal, seg, -jnp.inf))
                m_intra = (cb * decay * dtrow_ref[hd:hd + 1, :]).astype(BF16)
                c_scaled = (cmat * jnp.exp(a_i)).astype(BF16)
                bw_t = (b_t * wrow_ref[hd:hd + 1, :]).astype(BF16)
                lhs = jnp.concatenate(
                    [jnp.concatenate([m_intra, c_scaled], axis=1),
                     jnp.concatenate([bw_t, zeros_blk], axis=1)], axis=0)
                res.append(jnp.dot(lhs, rhs, preferred_element_type=F32))
            hd0 = lane0 + g * r_heads + 2 * pair
            y_ref[:, col:col + 128] = jnp.where(lane_lo, res[0][:q], res[1][:q]).astype(y_ref.dtype)
            dec_pair = jnp.where(lane_lo, dec_ref[hd0:hd0 + 1, :], dec_ref[hd0 + 1:hd0 + 2, :])
            upd = jnp.where(lane_lo, res[0][q:], res[1][q:])
            state_ref[:, col:col + 128] = dec_pair * st_old + upd

    if want_final:
        @pl.when(c == nc - 1)
        def _():
            for blk in range(SSD_INNER // 128):
                hf_ref[blk * 128:(blk + 1) * 128, :] = state_ref[:, blk * 128:(blk + 1) * 128].T


def _ssd_scan(st, xbc, dt, dt_bias, a_log, row0, nb, seq, rev, h0, layer_j, want_final, prev):
    nc = seq // SSD_CHUNK
    c0 = row0 // SSD_CHUNK

    def chunk(b, c):
        return c0 + b * nc + (nc - 1 - c if rev else c)

    in_specs = [
        pl.BlockSpec((SSD_CHUNK, SSD_INNER), lambda b, c: (chunk(b, c), 0)),
        pl.BlockSpec((SSD_CHUNK, SSD_GN), lambda b, c: (chunk(b, c), SSD_INNER // SSD_GN)),
        pl.BlockSpec((SSD_CHUNK, SSD_GN), lambda b, c: (chunk(b, c), SSD_INNER // SSD_GN + 1)),
        pl.BlockSpec((SSD_CHUNK, 128), lambda b, c: (chunk(b, c), 0)),
        pl.BlockSpec((1, 128), lambda b, c: (0, 0)),
        pl.BlockSpec((1, 128), lambda b, c: (0, 0)),
    ]
    args = [xbc, xbc, xbc, dt, dt_bias.reshape(1, 128), a_log.reshape(1, 128)]
    if h0 is not None:
        h0r = h0.reshape(h0.shape[0], h0.shape[1], SSD_INNER, SSD_STATE)
        in_specs.append(pl.BlockSpec((None, None, SSD_INNER, SSD_STATE),
                                     lambda b, c: (b, layer_j, 0, 0)))
        args.append(h0r)
    aliases = {}
    if prev is not None:
        in_specs.append(pl.BlockSpec(memory_space=pl.ANY))
        args.append(prev)
        aliases = {len(args) - 1: 0}
    out_shape = [jax.ShapeDtypeStruct((st.n, SSD_INNER), BF16)]
    out_specs = [pl.BlockSpec((SSD_CHUNK, SSD_INNER), lambda b, c: (chunk(b, c), 0))]
    if want_final:
        out_shape.append(jax.ShapeDtypeStruct((nb, SSD_INNER, SSD_STATE), F32))
        out_specs.append(pl.BlockSpec((None, SSD_INNER, SSD_STATE), lambda b, c: (b, 0, 0)))
    res = pl.pallas_call(
        functools.partial(_ssd_scan_kernel, rev=rev, nc=nc, has_init=h0 is not None,
                          want_final=want_final, aliased=prev is not None),
        grid=(nb, nc), in_specs=in_specs, out_specs=out_specs, out_shape=out_shape,
        scratch_shapes=[pltpu.VMEM((SSD_STATE, SSD_INNER), F32),
                        pltpu.VMEM((SSD_CHUNK, 128), F32)] + [pltpu.VMEM((128, SSD_CHUNK), F32)] * 4,
        input_output_aliases=aliases,
        compiler_params=_cparams(("parallel", "arbitrary")), name="ssd_scan",
    )(*args)
    return res


def _ssd_gate_kernel(yf_ref, yb_ref, x_ref, z_ref, d_ref, g_ref, o_ref):
    gw = SSD_INNER // SSD_GROUPS
    for grp in range(SSD_GROUPS):
        sl = slice(grp * gw, (grp + 1) * gw)
        y = (yf_ref[:, sl].astype(F32) + yb_ref[:, sl].astype(F32)) + d_ref[:, sl] * x_ref[:, sl]
        gated = y * _silu(z_ref[:, sl].astype(F32))
        normed = gated * lax.rsqrt(jnp.mean(gated * gated, axis=-1, keepdims=True) + EPS)
        o_ref[:, sl] = (normed * g_ref[:, sl]).astype(o_ref.dtype)


def _ssd_gate(yf, yb, xbc, z, d_sum, g_norm):
    n = yf.shape[0]
    tm = 256
    row = pl.BlockSpec((tm, SSD_INNER), lambda i: (i, 0))
    vec = pl.BlockSpec((1, SSD_INNER), lambda i: (0, 0))
    return pl.pallas_call(
        _ssd_gate_kernel, grid=(n // tm,),
        in_specs=[row, row, row, row, vec, vec], out_specs=row,
        out_shape=jax.ShapeDtypeStruct((n, SSD_INNER), BF16),
        compiler_params=_cparams(("parallel",)), name="ssd_gate",
    )(yf, yb, xbc, z, d_sum, g_norm.reshape(1, SSD_INNER))


def _moe_up_kernel(be_ref, nu_ref, a_ref, wg_ref, wu_ref, *rest, blk0):
    o_ref = rest[-1]
    m = pl.program_id(1) + blk0

    @pl.when(m < nu_ref[0])
    def _():
        a = a_ref[...]
        gate = jnp.dot(a, wg_ref[...].astype(BF16), preferred_element_type=F32)
        up = jnp.dot(a, wu_ref[...].astype(BF16), preferred_element_type=F32)
        o_ref[...] = (_silu(gate) * up).astype(o_ref.dtype)

    @pl.when(m >= nu_ref[0])
    def _():
        o_ref[...] = jnp.zeros(o_ref.shape, o_ref.dtype)


def _moe_down_kernel(be_ref, nu_ref, a_ref, w_ref, o_ref):
    m = pl.program_id(0)

    @pl.when(m < nu_ref[0])
    def _():
        o_ref[...] = jnp.dot(a_ref[...], w_ref[...].astype(BF16),
                             preferred_element_type=F32).astype(o_ref.dtype)

    @pl.when(m >= nu_ref[0])
    def _():
        o_ref[...] = jnp.zeros(o_ref.shape, o_ref.dtype)


def _moe(st, h, logits, w_gu, w_down, f):
    n = st.n
    tb = MOE_BLOCK
    n_assign = n * TOP_K
    n_blocks = n_assign // tb + N_EXPERTS
    slots = n_blocks * tb
    top_logit, top_e = lax.top_k(logits, TOP_K)
    gates = jax.nn.softmax(top_logit, axis=-1)
    flat_e = top_e.reshape(-1).astype(jnp.int32)
    onehot = (flat_e[:, None] == jnp.arange(N_EXPERTS, dtype=jnp.int32)[None, :]).astype(jnp.int32)
    rank = jnp.sum((jnp.cumsum(onehot, axis=0) - onehot) * onehot, axis=1)
    counts = jnp.sum(onehot, axis=0)
    padded = (counts + tb - 1) // tb * tb
    pad_end = jnp.cumsum(padded)
    pad_start = pad_end - padded
    dest = pad_start[flat_e] + rank
    block_start = jnp.arange(n_blocks, dtype=jnp.int32) * tb
    block_e = jnp.minimum(jnp.searchsorted(pad_end, block_start, side="right"),
                          N_EXPERTS - 1).astype(jnp.int32)
    n_used = (pad_end[-1] // tb).astype(jnp.int32).reshape(1)
    order = jnp.argsort(flat_e, stable=True).astype(jnp.int32)
    slot_e = jnp.repeat(block_e, tb)
    slot_r = jnp.arange(slots, dtype=jnp.int32) - pad_start[slot_e]
    filled = slot_r < counts[slot_e]
    src = jnp.where(filled, (jnp.cumsum(counts) - counts)[slot_e] + slot_r, 0)
    slot_tok = jnp.where(filled, order[src] // TOP_K, 0)

    tn = 512
    nf = EXPERT_DIM // tn

    def live_block(m, nu):
        return jnp.minimum(m, nu[0] - 1)

    head = min(MOE_HEAD_BLOCKS, n_blocks // 2)
    act = None
    for blk0, nblk in ((0, head), (head, n_blocks - head)):
        xs = h.at[slot_tok[blk0 * tb:(blk0 + nblk) * tb]].get(mode="promise_in_bounds")

        def local(m, nu, blk0=blk0, nblk=nblk):
            return jnp.clip(live_block(m + blk0, nu) - blk0, 0, nblk - 1)

        def expert(m, be, nu, blk0=blk0):
            return be[live_block(m + blk0, nu)]

        in_specs = [
            pl.BlockSpec((tb, D_MODEL), lambda j, m, be, nu, local=local: (local(m, nu), 0)),
            pl.BlockSpec((None, None, D_MODEL, tn),
                         lambda j, m, be, nu, expert=expert: (f, expert(m, be, nu), 0, j)),
            pl.BlockSpec((None, None, D_MODEL, tn),
                         lambda j, m, be, nu, expert=expert: (f, expert(m, be, nu), 0, j + nf)),
        ]
        args = [block_e, n_used, xs, w_gu, w_gu]
        aliases = {}
        if act is not None:
            in_specs.append(pl.BlockSpec(memory_space=pl.ANY))
            args.append(act)
            aliases = {len(args) - 1: 0}
        act = pl.pallas_call(
            functools.partial(_moe_up_kernel, blk0=blk0),
            grid_spec=pltpu.PrefetchScalarGridSpec(
                num_scalar_prefetch=2, grid=(nf, nblk), in_specs=in_specs,
                out_specs=pl.BlockSpec((tb, tn), lambda j, m, be, nu, blk0=blk0: (m + blk0, j)),
            ),
            out_shape=jax.ShapeDtypeStruct((slots, EXPERT_DIM), BF16),
            input_output_aliases=aliases,
            compiler_params=_cparams(("parallel", "arbitrary")), name="moe_up",
        )(*args)

    tn2 = 256
    yb = pl.pallas_call(
        _moe_down_kernel,
        grid_spec=pltpu.PrefetchScalarGridSpec(
            num_scalar_prefetch=2, grid=(n_blocks, D_MODEL // tn2),
            in_specs=[
                pl.BlockSpec((tb, EXPERT_DIM), lambda m, j, be, nu: (live_block(m, nu), 0)),
                pl.BlockSpec((None, None, EXPERT_DIM, tn2),
                             lambda m, j, be, nu: (f, be[live_block(m, nu)], 0, j)),
            ],
            out_specs=pl.BlockSpec((tb, tn2), lambda m, j, be, nu: (m, j)),
        ),
        out_shape=jax.ShapeDtypeStruct((slots, D_MODEL), BF16),
        compiler_params=_cparams(("arbitrary", "arbitrary")), name="moe_down",
    )(block_e, n_used, act, w_down)

    dest2 = dest.reshape(n, TOP_K)
    y0 = yb.at[dest2[:, 0]].get(mode="promise_in_bounds")
    y1 = yb.at[dest2[:, 1]].get(mode="promise_in_bounds")
    return y0, y1, gates


def _combine_kernel(x_ref, y0_ref, y1_ref, g_ref, gate_ref, *rest, norm):
    y = g_ref[:, 0:1] * y0_ref[...].astype(F32) + g_ref[:, 1:2] * y1_ref[...].astype(F32)
    xn = x_ref[...] + gate_ref[...] * y
    if norm:
        gn_ref, shift_ref, scale_ref, o_ref, h_ref = rest
        _norm_mod_store(xn, gn_ref, shift_ref, scale_ref, h_ref)
    else:
        (o_ref,) = rest
    o_ref[...] = xn


def _combine(st, x, y0, y1, gates, mods, layer, g_next=None):
    tm = st.tile(512)
    row = pl.BlockSpec((tm, D_MODEL), lambda i: (i, 0))

    def mod(lyr, which):
        return pl.BlockSpec((None, 1, D_MODEL),
                            lambda i: (_mod_index(lyr, which, st.mod_row(i, tm)), 0, 0))

    in_specs = [row, row, row, pl.BlockSpec((tm, TOP_K), lambda i: (i, 0)), mod(layer, 5)]
    args = [x, y0, y1, gates, mods]
    out_shape = [jax.ShapeDtypeStruct((st.n, D_MODEL), F32)]
    out_specs = [row]
    if g_next is not None:
        in_specs += [pl.BlockSpec((1, D_MODEL), lambda i: (0, 0)), mod(layer + 1, 0), mod(layer + 1, 1)]
        args += [g_next.reshape(1, D_MODEL), mods, mods]
        out_shape.append(jax.ShapeDtypeStruct((st.n, D_MODEL), BF16))
        out_specs.append(row)
    res = pl.pallas_call(
        functools.partial(_combine_kernel, norm=g_next is not None), grid=(st.n // tm,),
        in_specs=in_specs, out_specs=out_specs, out_shape=out_shape,
        compiler_params=_cparams(("parallel",)), name="moe_combine",
    )(*args)
    return res if g_next is not None else (res[0], None)


def kernel(x_prompt, x_sample, cache_attn_k, cache_attn_v, cache_diff_k, cache_diff_v, state_ssd_fwd, state_ssd_bwd, c, c_ctx, w_ada, b_ada, g_mix, g_ffn, g_final, attn_w_in, attn_g_q, attn_g_k, attn_w_out, diff_w_in, diff_lambda, diff_g_sub, diff_w_out, ssd_w_in, ssd_conv_w, ssd_conv_b, ssd_dt_bias, ssd_a_log, ssd_d, ssd_g_norm, ssd_w_out, ffn_w_gu, ffn_w_down, moe_w_router, moe_w_gu, moe_w_down):
    pb, pl_len, _ = x_prompt.shape
    sb, sl_len, _ = x_sample.shape
    st = _Stream(pb, pl_len, sb, sl_len)
    np_, ns = st.n_prompt, st.n_sample
    tm = st.tile(1024)

    x = jnp.concatenate([x_prompt.reshape(np_, D_MODEL), x_sample.reshape(ns, D_MODEL)], axis=0)

    cond = jnp.zeros((MOD_ROWS, D_MODEL), F32).at[:sb].set(c).at[CTX_ROW].set(c_ctx)
    cond_act = jax.nn.silu(cond).astype(BF16)
    mods = []
    for i in range(DEPTH):
        m = _matmul(cond_act, w_ada, (i,), n_off=0, n_out=6 * D_MODEL, tm=MOD_ROWS, tn=1024,
                    out_dtype=F32, name="ada")
        mods.append(m + b_ada[i][None, :])
    mods = jnp.stack(mods).reshape(DEPTH * MOD_ROWS * 6, 1, D_MODEL)

    tables = _rope_tables(sl_len, tm)
    prep_tables = _rope_tables(sl_len, GQA_PREP_ROWS)

    def table_index(i):
        return _table_block(st, i, tm)

    tm2 = st.tile(2048)

    def gate_index(layer, which, tile):
        return lambda i: _mod_index(layer, which, st.mod_row(i, tile))

    new_attn_k, new_attn_v, new_diff_k, new_diff_v, new_ssd_f, new_ssd_b = [], [], [], [], [], []
    def out_proj(o, w16, x, layer, fuse_norm):
        if fuse_norm:
            return _out_proj_norm(st, o, w16, x, mods, layer, g_ffn[layer])
        xo = _matmul(o, w16, (), n_off=0, n_out=D_MODEL, tm=tm2, tn=512,
                     out_dtype=F32, epilogue="resid", resid=x, mods=mods,
                     gate_index=gate_index(layer, 2, tm2), name="attn_out")
        return xo, None

    h_pre = None
    for i in range(DEPTH):
        kind, j = i % 3, i // 3
        f = i // 2
        w_router = moe_w_router[f] if i % 2 == 1 else None
        h = h_pre if h_pre is not None else _norm_mod(st, x, g_mix[i], mods, i, 0)
        h_pre = hf = logits = None
        if kind == 0:
            qkv = _matmul(h, attn_w_in, (j,), n_off=0, n_out=attn_w_in.shape[-1], tm=tm, tn=1024,
                          out_dtype=F32, name="attn_in")
            k = _gqa_kprep(st, qkv, attn_g_k[j], prep_tables)
            o = _gqa_attn(st, qkv, attn_g_q[j], k, 0, pb, pl_len, None, None, None, j, None)
            o = _gqa_attn(st, qkv, attn_g_q[j], k, np_, sb, sl_len, tables,
                          cache_attn_k, cache_attn_v, j, o)
            new_attn_k.append(k[:np_].reshape(pb, pl_len, ATTN_KV_HEADS, HD))
            new_attn_v.append(qkv[:np_, (ATTN_HEADS + ATTN_KV_HEADS) * HD:]
                              .reshape(pb, pl_len, ATTN_KV_HEADS, HD))
            x, hf = out_proj(o, attn_w_out[j].astype(BF16), x, i, w_router is None)
        elif kind == 1:
            lam_init = 0.8 - 0.6 * math.exp(-0.3 * i)
            lf = diff_lambda[j]
            lam = (jnp.exp(jnp.sum(lf[0] * lf[1])) - jnp.exp(jnp.sum(lf[2] * lf[3])) + lam_init).reshape(1)
            w = 2 * DIFF_HEADS * HD
            q = _matmul(h, diff_w_in, (j,), n_off=0, n_out=w, tm=tm, tn=1024, out_dtype=BF16,
                        epilogue="rope", scale=Q_SCALE,
                        tables=tables, table_index=table_index, name="diff_in_q")
            k = _matmul(h, diff_w_in, (j,), n_off=w, n_out=w, tm=tm, tn=1024, out_dtype=BF16,
                        epilogue="rope", tables=tables, table_index=table_index, name="diff_in_k")
            v = _matmul(h, diff_w_in, (j,), n_off=2 * w, n_out=w, tm=tm, tn=1024, out_dtype=F32,
                        name="diff_in_v")
            k_ctx = _matmul(h[:np_], diff_w_in, (j,), n_off=w, n_out=w, tm=tm, tn=1024,
                            out_dtype=F32, name="diff_in_kctx")
            o = _diff_attn(st, lam, q, k, v, diff_g_sub[j], lam_init, 0, pb, pl_len, None, None, j, None)
            o = _diff_attn(st, lam, q, k, v, diff_g_sub[j], lam_init, np_, sb, sl_len,
                           cache_diff_k, cache_diff_v, j, o)
            new_diff_k.append(k_ctx.reshape(pb, pl_len, 2 * DIFF_HEADS, HD))
            new_diff_v.append(v[:np_].reshape(pb, pl_len, DIFF_HEADS, 2 * HD))
            x, hf = out_proj(o, diff_w_out[j].astype(BF16), x, i, w_router is None)
        else:
            z = _matmul(h, ssd_w_in, (j,), n_off=0, n_out=SSD_INNER, tm=tm, tn=1024,
                        out_dtype=BF16, name="ssd_in_z")
            xbc = _matmul(h, ssd_w_in, (j,), n_off=SSD_INNER, n_out=SSD_CONV_DIM, tm=tm, tn=1024,
                          out_dtype=F32, name="ssd_in_xbc")
            dt = _matmul(h, ssd_w_in, (j,), n_off=SSD_INNER + SSD_CONV_DIM, n_out=2 * SSD_HEADS,
                         tm=tm, tn=128, out_dtype=F32, name="ssd_in_dt")
            xc = _ssd_conv(st, xbc, ssd_conv_w[j], ssd_conv_b[j], 0, pb, pl_len, None)
            xc = _ssd_conv(st, xbc, ssd_conv_w[j], ssd_conv_b[j], np_, sb, sl_len, xc)
            ys = []
            for rev, h0 in ((False, state_ssd_fwd), (True, state_ssd_bwd)):
                y, hfin = _ssd_scan(st, xc, dt, ssd_dt_bias[j], ssd_a_log[j], 0, pb, pl_len, rev,
                                    None, j, True, None)
                (y,) = _ssd_scan(st, xc, dt, ssd_dt_bias[j], ssd_a_log[j], np_, sb, sl_len, rev,
                                 h0, j, False, y)
                ys.append(y)
                (new_ssd_b if rev else new_ssd_f).append(
                    hfin.reshape(pb, SSD_HEADS, SSD_HEADDIM, SSD_STATE))
            d_sum = jnp.repeat(ssd_d[j][0] + ssd_d[j][1], SSD_HEADDIM).reshape(1, SSD_INNER)
            o = _ssd_gate(ys[0], ys[1], xc, z, d_sum, ssd_g_norm[j])
            x = _matmul(o, ssd_w_out[j].astype(BF16), (), n_off=0, n_out=D_MODEL, tm=tm, tn=512,
                        out_dtype=F32, epilogue="resid", resid=x, mods=mods,
                        gate_index=gate_index(i, 2, tm), name="ssd_out")

        if hf is None:
            if w_router is not None:
                hf, logits = _norm_mod(st, x, g_ffn[i], mods, i, 3, w_router=w_router)
            else:
                hf = _norm_mod(st, x, g_ffn[i], mods, i, 3)
        if i % 2 == 0:
            act = _matmul(hf, ffn_w_gu, (f,), n_off=0, n_out=FFN_DIM, tm=tm, tn=512,
                          out_dtype=BF16, epilogue="swiglu", up_off=FFN_DIM, name="ffn_up")
            x = _matmul(act, ffn_w_down[f].astype(BF16), (), n_off=0, n_out=D_MODEL, tm=tm, tn=512,
                        out_dtype=F32, epilogue="resid", resid=x, mods=mods,
                        gate_index=gate_index(i, 5, tm), name="ffn_down")
        else:
            y0, y1, gates = _moe(st, hf, logits[:, :N_EXPERTS], moe_w_gu, moe_w_down, f)
            x, h_pre = _combine(st, x, y0, y1, gates, mods, i,
                                g_next=g_mix[i + 1] if i + 1 < DEPTH else None)

    y_prompt = _final_norm(x, g_final, 0, np_).reshape(pb, pl_len, D_MODEL)
    y_sample = _final_norm(x, g_final, np_, ns).reshape(sb, sl_len, D_MODEL)
    return (y_prompt, y_sample,
            jnp.stack(new_attn_k, axis=1), jnp.stack(new_attn_v, axis=1),
            jnp.stack(new_diff_k, axis=1), jnp.stack(new_diff_v, axis=1),
            jnp.stack(new_ssd_f, axis=1), jnp.stack(new_ssd_b, axis=1))
```

```python
import functools
import math

import jax
import jax.numpy as jnp
from jax import lax
from jax.experimental import pallas as pl
from jax.experimental.pallas import tpu as pltpu

F32 = jnp.float32
BF16 = jnp.bfloat16

D_MODEL = 2048
DEPTH = 4
GRID_W = 64
ROPE_THETA = 10000.0
EPS = 1e-6
HD = 128
Q_SCALE = math.log2(math.e) / math.sqrt(HD)
ATTN_HEADS = 16
ATTN_KV_HEADS = 4
DIFF_HEADS = 8
SSD_INNER = 2 * D_MODEL
SSD_HEADDIM = 64
SSD_HEADS = SSD_INNER // SSD_HEADDIM
SSD_GROUPS = 8
SSD_STATE = 128
SSD_CONV = 5
SSD_CHUNK = 128
SSD_GN = SSD_GROUPS * SSD_STATE
SSD_CONV_DIM = SSD_INNER + 2 * SSD_GN
FFN_DIM = 5632
N_EXPERTS = 8
TOP_K = 2
EXPERT_DIM = 7168

MOD_ROWS = 16
CTX_ROW = 8
MOE_BLOCK = 1024
MOE_HEAD_BLOCKS = 8
VMEM_LIMIT = 56 * 1024 * 1024


def _cparams(sem):
    return pltpu.CompilerParams(dimension_semantics=sem, vmem_limit_bytes=VMEM_LIMIT)


def _silu(x):
    return x / (1.0 + jnp.exp(-x))


def _split3(x):
    hi = x.astype(BF16)
    r = x - hi.astype(F32)
    mid = r.astype(BF16)
    lo = (r - mid.astype(F32)).astype(BF16)
    return hi, mid, lo


def _dot3(a3, b):
    out = jnp.dot(a3[0], b, preferred_element_type=F32)
    out = out + jnp.dot(a3[1], b, preferred_element_type=F32)
    return out + jnp.dot(a3[2], b, preferred_element_type=F32)


class _Stream:
    def __init__(self, n_prompt_batch, prompt_len, n_sample_batch, sample_len):
        self.pb, self.pl_, self.sb, self.sl = n_prompt_batch, prompt_len, n_sample_batch, sample_len
        self.n_prompt = n_prompt_batch * prompt_len
        self.n_sample = n_sample_batch * sample_len
        self.n = self.n_prompt + self.n_sample

    def mod_row(self, i, tm):
        pt = self.n_prompt // tm
        per = self.sl // tm
        return jnp.where(i < pt, CTX_ROW, (i - pt) // per)

    def tile(self, want):
        t = want
        while self.n_prompt % t or self.sl % t:
            t //= 2
        return t


def _mod_index(layer, which, row):
    return (layer * MOD_ROWS + row) * 6 + which


def _norm_mod_store(x, g_ref, shift_ref, scale_ref, o_ref, wr_ref=None, lg_ref=None):
    y = x * lax.rsqrt(jnp.mean(x * x, axis=-1, keepdims=True) + EPS) * g_ref[...]
    h = y * (1.0 + scale_ref[...]) + shift_ref[...]
    if wr_ref is not None:
        w = wr_ref[...]
        whi = w.astype(BF16)
        wlo = (w - whi.astype(F32)).astype(BF16)
        hhi = h.astype(BF16)
        hlo = (h - hhi.astype(F32)).astype(BF16)
        lg = jnp.dot(hhi, whi, preferred_element_type=F32)
        lg = lg + jnp.dot(hlo, whi, preferred_element_type=F32)
        lg_ref[...] = lg + jnp.dot(hhi, wlo, preferred_element_type=F32)
    o_ref[...] = h.astype(o_ref.dtype)


def _norm_mod_kernel(x_ref, g_ref, shift_ref, scale_ref, *rest, router):
    if router:
        wr_ref, o_ref, lg_ref = rest
        _norm_mod_store(x_ref[...], g_ref, shift_ref, scale_ref, o_ref, wr_ref, lg_ref)
    else:
        (o_ref,) = rest
        _norm_mod_store(x_ref[...], g_ref, shift_ref, scale_ref, o_ref)


def _router_pad(w_router):
    return jnp.zeros((D_MODEL, 128), F32).at[:, :N_EXPERTS].set(w_router)


def _out_norm_kernel(a_ref, w_ref, x_ref, gate_ref, g_ref, shift_ref, scale_ref, xo_ref, ho_ref):
    xn = x_ref[...] + gate_ref[...] * jnp.dot(a_ref[...], w_ref[...], preferred_element_type=F32)
    _norm_mod_store(xn, g_ref, shift_ref, scale_ref, ho_ref)
    xo_ref[...] = xn


def _out_proj_norm(st, a, w, x, mods, layer, g_next):
    tm = st.tile(512)
    kdim = a.shape[1]

    def mod(which):
        return pl.BlockSpec((None, 1, D_MODEL),
                            lambda i: (_mod_index(layer, which, st.mod_row(i, tm)), 0, 0))

    row = pl.BlockSpec((tm, D_MODEL), lambda i: (i, 0))
    in_specs = [pl.BlockSpec((tm, kdim), lambda i: (i, 0)),
                pl.BlockSpec((kdim, D_MODEL), lambda i: (0, 0), pipeline_mode=pl.Buffered(1)),
                row, mod(2), pl.BlockSpec((1, D_MODEL), lambda i: (0, 0)), mod(3), mod(4)]
    args = [a, w, x, mods, g_next.reshape(1, D_MODEL), mods, mods]
    out_shape = [jax.ShapeDtypeStruct((st.n, D_MODEL), F32), jax.ShapeDtypeStruct((st.n, D_MODEL), BF16)]
    return pl.pallas_call(
        _out_norm_kernel,
        grid=(st.n // tm,), in_specs=in_specs, out_specs=[row, row], out_shape=out_shape,
        compiler_params=_cparams(("parallel",)), name="out_proj_norm",
    )(*args)


def _norm_mod(st, x, g, mods, layer, which_shift, w_router=None):
    tm = st.tile(512)
    nt = st.n // tm
    g2 = g.reshape(1, D_MODEL)
    in_specs = [
        pl.BlockSpec((tm, D_MODEL), lambda i: (i, 0)),
        pl.BlockSpec((1, D_MODEL), lambda i: (0, 0)),
        pl.BlockSpec((None, 1, D_MODEL),
                     lambda i: (_mod_index(layer, which_shift, st.mod_row(i, tm)), 0, 0)),
        pl.BlockSpec((None, 1, D_MODEL),
                     lambda i: (_mod_index(layer, which_shift + 1, st.mod_row(i, tm)), 0, 0)),
    ]
    args = [x, g2, mods, mods]
    out_shape = [jax.ShapeDtypeStruct((st.n, D_MODEL), BF16)]
    out_specs = [pl.BlockSpec((tm, D_MODEL), lambda i: (i, 0))]
    if w_router is not None:
        in_specs.append(pl.BlockSpec((D_MODEL, 128), lambda i: (0, 0)))
        args.append(_router_pad(w_router))
        out_shape.append(jax.ShapeDtypeStruct((st.n, 128), F32))
        out_specs.append(pl.BlockSpec((tm, 128), lambda i: (i, 0)))
    res = pl.pallas_call(
        functools.partial(_norm_mod_kernel, router=w_router is not None),
        grid=(nt,), in_specs=in_specs, out_specs=out_specs, out_shape=out_shape,
        compiler_params=_cparams(("parallel",)), name="norm_mod",
    )(*args)
    return res if w_router is not None else res[0]


def _final_norm_kernel(x_ref, g_ref, o_ref):
    x = x_ref[...]
    o_ref[...] = x * lax.rsqrt(jnp.mean(x * x, axis=-1, keepdims=True) + EPS) * g_ref[...]


def _final_norm(x, g, row0, nrows):
    tm = 512
    r0 = row0 // tm
    return pl.pallas_call(
        _final_norm_kernel, grid=(nrows // tm,),
        in_specs=[pl.BlockSpec((tm, D_MODEL), lambda i: (i + r0, 0)),
                  pl.BlockSpec((1, D_MODEL), lambda i: (0, 0))],
        out_specs=pl.BlockSpec((tm, D_MODEL), lambda i: (i, 0)),
        out_shape=jax.ShapeDtypeStruct((nrows, D_MODEL), F32),
        compiler_params=_cparams(("parallel",)), name="final_norm",
    )(x, g.reshape(1, D_MODEL))


def _mm_kernel(*refs, epilogue, scale):
    a_ref, w_ref = refs[0], refs[1]
    o_ref = refs[-1]
    a = a_ref[...]
    y = jnp.dot(a, w_ref[...].astype(BF16), preferred_element_type=F32)
    if epilogue == "swiglu":
        up = jnp.dot(a, refs[2][...].astype(BF16), preferred_element_type=F32)
        o_ref[...] = (_silu(y) * up).astype(o_ref.dtype)
    elif epilogue == "resid":
        x_ref, gate_ref = refs[2], refs[3]
        o_ref[...] = (x_ref[...] + gate_ref[...] * y).astype(o_ref.dtype)
    elif epilogue == "rope":
        cos, sin = refs[2][...], refs[3][...]
        for h in range(y.shape[1] // HD):
            yh = _rope(y[:, h * HD:(h + 1) * HD], cos, sin)
            o_ref[:, h * HD:(h + 1) * HD] = (yh * scale).astype(o_ref.dtype)
    else:
        o_ref[...] = y.astype(o_ref.dtype)


def _matmul(a, w, widx, *, n_off, n_out, tm, tn, out_dtype, epilogue="plain", up_off=None,
            resid=None, mods=None, gate_index=None, scale=1.0, tables=None, table_index=None,
            rows=None, name="mm"):
    m, kdim = (a.shape[0] if rows is None else rows), a.shape[1]
    assert m % tm == 0 and n_out % tn == 0 and n_off % tn == 0
    nb = n_off // tn
    lead = (None,) * len(widx)
    in_specs = [
        pl.BlockSpec((tm, kdim), lambda i, j: (i, 0)),
        pl.BlockSpec(lead + (kdim, tn), lambda i, j: widx + (0, j + nb)),
    ]
    args = [a, w]
    if epilogue == "swiglu":
        ub = up_off // tn
        in_specs.append(pl.BlockSpec(lead + (kdim, tn), lambda i, j: widx + (0, j + ub)))
        args.append(w)
    if epilogue == "resid":
        in_specs.append(pl.BlockSpec((tm, tn), lambda i, j: (i, j)))
        in_specs.append(pl.BlockSpec((None, 1, tn), lambda i, j: (gate_index(i), 0, j)))
        args += [resid, mods]
    if epilogue == "rope":
        in_specs += [pl.BlockSpec((tm, HD), lambda i, j: (table_index(i), 0))] * 2
        args += list(tables)
    return pl.pallas_call(
        functools.partial(_mm_kernel, epilogue=epilogue, scale=scale),
        grid=(m // tm, n_out // tn),
        in_specs=in_specs,
        out_specs=pl.BlockSpec((tm, tn), lambda i, j: (i, j)),
        out_shape=jax.ShapeDtypeStruct((m, n_out), out_dtype),
        compiler_params=_cparams(("parallel", "parallel")), name=name,
    )(*args)


def _rope_tables(length, ident_rows):
    nf = HD // 4
    inv = ROPE_THETA ** (-jnp.arange(nf, dtype=F32) / nf)
    rows = length // GRID_W
    row = jnp.repeat(jnp.arange(rows, dtype=F32), GRID_W)
    col = jnp.tile(jnp.arange(GRID_W, dtype=F32), rows)
    ar = row[:, None] * inv
    ac = col[:, None] * inv
    ang = jnp.concatenate([ar, ar, ac, ac], axis=-1)
    cos, sin = jnp.cos(ang), jnp.sin(ang)
    lane = jnp.arange(HD) % (HD // 2)
    sin_signed = jnp.where(lane >= HD // 4, sin, -sin)
    ones = jnp.ones((ident_rows, HD), F32)
    zeros = jnp.zeros((ident_rows, HD), F32)
    return jnp.concatenate([cos, ones]), jnp.concatenate([sin_signed, zeros])


def _table_block(st, i, tm):
    pt = st.n_prompt // tm
    per = st.sl // tm
    return jnp.where(i < pt, per, (i - pt) % per)


def _rope(y, cos, sin_signed):
    partner = lax.broadcasted_iota(jnp.int32, y.shape, 1) ^ (HD // 4)
    return y * cos + jnp.take_along_axis(y, partner, axis=1) * sin_signed


GQA_PREP_ROWS = 256


def _head_norm(x, g):
    return x * lax.rsqrt(jnp.mean(x * x, axis=-1, keepdims=True) + EPS) * g


def _gqa_kprep_kernel(k_ref, gk_ref, cos_ref, sin_ref, ko_ref):
    cos, sin = cos_ref[...], sin_ref[...]
    for h in range(ATTN_KV_HEADS):
        ko_ref[:, h * HD:(h + 1) * HD] = _rope(_head_norm(k_ref[:, h * HD:(h + 1) * HD], gk_ref[...]),
                                               cos, sin)


def _gqa_kprep(st, qkv, g_k, tables):
    tm = GQA_PREP_ROWS
    qw = ATTN_HEADS * HD
    kw = ATTN_KV_HEADS * HD
    tab = pl.BlockSpec((tm, HD), lambda i: (_table_block(st, i, tm), 0))
    return pl.pallas_call(
        _gqa_kprep_kernel, grid=(st.n // tm,),
        in_specs=[pl.BlockSpec((tm, kw), lambda i: (i, qw // kw)),
                  pl.BlockSpec((1, HD), lambda i: (0, 0)), tab, tab],
        out_specs=pl.BlockSpec((tm, kw), lambda i: (i, 0)),
        out_shape=jax.ShapeDtypeStruct((st.n, kw), F32),
        compiler_params=_cparams(("parallel",)), name="gqa_kprep",
    )(qkv, g_k.reshape(1, HD), *tables)


def _softmax_parts(q, ks):
    ss = [lax.dot_general(q, k, (((1,), (1,)), ((), ())), preferred_element_type=F32) for k in ks]
    m = ss[0].max(axis=-1, keepdims=True)
    for s in ss[1:]:
        m = jnp.maximum(m, s.max(axis=-1, keepdims=True))
    ps = [jnp.exp2(s - m) for s in ss]
    l = ps[0].sum(axis=-1, keepdims=True)
    for p in ps[1:]:
        l = l + p.sum(axis=-1, keepdims=True)
    return ps, l


def _gqa_attn_kernel(q_ref, gq_ref, k_ref, v_ref, *rest, cache, aliased, rope):
    rest = list(rest)
    o_ref = rest.pop()
    if aliased:
        rest.pop()
    if rope:
        cos, sin = rest[0][...], rest[1][...]
        rest = rest[2:]
    if cache:
        ck_ref, cv_ref = rest
        ks = [ck_ref[...].astype(BF16), k_ref[...].astype(BF16)]
        vs = [cv_ref[...].astype(BF16), v_ref[...].astype(BF16)]
    else:
        ks = [k_ref[...].astype(BF16)]
        vs = [v_ref[...].astype(BF16)]
    grp = ATTN_HEADS // ATTN_KV_HEADS
    for h in range(grp):
        y = _head_norm(q_ref[:, h * HD:(h + 1) * HD], gq_ref[...])
        if rope:
            y = _rope(y, cos, sin)
        q = (y * Q_SCALE).astype(BF16)
        ps, l = _softmax_parts(q, ks)
        o = jnp.dot(ps[0].astype(BF16), vs[0], preferred_element_type=F32)
        for p, v in zip(ps[1:], vs[1:]):
            o = o + jnp.dot(p.astype(BF16), v, preferred_element_type=F32)
        o_ref[:, h * HD:(h + 1) * HD] = (o / l).astype(o_ref.dtype)


def _gqa_attn(st, qkv, g_q, k, row0, nb, seq, tables, cache_k, cache_v, layer_j, prev):
    tq = min(1024, seq)
    nq = seq // tq
    grp_w = (ATTN_HEADS // ATTN_KV_HEADS) * HD
    v_col0 = ATTN_HEADS + ATTN_KV_HEADS
    assert row0 % seq == 0, "a segment must start on a multiple of its sequence length"
    b0 = row0 // seq
    q0 = row0 // tq
    in_specs = [
        pl.BlockSpec((tq, grp_w), lambda b, g, i: (q0 + b * nq + i, g)),
        pl.BlockSpec((1, HD), lambda b, g, i: (0, 0)),
        pl.BlockSpec((seq, HD), lambda b, g, i: (b + b0, g)),
        pl.BlockSpec((seq, HD), lambda b, g, i: (b + b0, v_col0 + g)),
    ]
    args = [qkv, g_q.reshape(1, HD), k, qkv]
    if tables is not None:
        in_specs += [pl.BlockSpec((tq, HD), lambda b, g, i: (i, 0))] * 2
        args += list(tables)
    cache = cache_k is not None
    if cache:
        past = cache_k.shape[2]
        ck = cache_k.reshape(cache_k.shape[0], cache_k.shape[1], past, ATTN_KV_HEADS * HD)
        cv = cache_v.reshape(ck.shape)
        in_specs += [pl.BlockSpec((None, None, past, HD), lambda b, g, i: (b, layer_j, 0, g))] * 2
        args += [ck, cv]
    aliases = {}
    if prev is not None:
        in_specs.append(pl.BlockSpec(memory_space=pl.ANY))
        args.append(prev)
        aliases = {len(args) - 1: 0}
    return pl.pallas_call(
        functools.partial(_gqa_attn_kernel, cache=cache, aliased=prev is not None,
                          rope=tables is not None),
        grid=(nb, ATTN_KV_HEADS, nq), in_specs=in_specs,
        out_specs=pl.BlockSpec((tq, grp_w), lambda b, g, i: (q0 + b * nq + i, g)),
        out_shape=jax.ShapeDtypeStruct((st.n, ATTN_HEADS * HD), BF16),
        input_output_aliases=aliases,
        compiler_params=_cparams(("parallel", "parallel", "parallel")), name="gqa_attn",
    )(*args)


def _diff_attn_kernel(lam_ref, q_ref, k_ref, v_ref, g_ref, *rest, cache, out_scale, aliased):
    if aliased:
        rest = rest[:-2] + rest[-1:]
    if cache:
        ck_ref, cv_ref, o_ref = rest
        vs = [cv_ref[...].astype(BF16), v_ref[...].astype(BF16)]
    else:
        (o_ref,) = rest
        vs = [v_ref[...].astype(BF16)]
    lam = lam_ref[0]
    outs = []
    for m in range(2):
        q = q_ref[:, m * HD:(m + 1) * HD]
        ks = [k_ref[:, m * HD:(m + 1) * HD]]
        if cache:
            ks = [ck_ref[:, m * HD:(m + 1) * HD].astype(BF16)] + ks
        ps, l = _softmax_parts(q, ks)
        o = jnp.dot(ps[0].astype(BF16), vs[0], preferred_element_type=F32)
        for p, v in zip(ps[1:], vs[1:]):
            o = o + jnp.dot(p.astype(BF16), v, preferred_element_type=F32)
        outs.append(o / l)
    o = outs[0] - lam * outs[1]
    o = o * lax.rsqrt(jnp.mean(o * o, axis=-1, keepdims=True) + EPS) * g_ref[...]
    o_ref[...] = (o * out_scale).astype(o_ref.dtype)


def _diff_attn(st, lam, q, k, v, g_sub, lam_init, row0, nb, seq, cache_k, cache_v, layer_j, prev):
    tq = min(1024, seq)
    nq = seq // tq
    hw = 2 * HD
    assert row0 % seq == 0, "a segment must start on a multiple of its sequence length"
    b0 = row0 // seq
    q0 = row0 // tq
    in_specs = [
        pl.BlockSpec(memory_space=pltpu.SMEM),
        pl.BlockSpec((tq, hw), lambda b, h, i: (q0 + b * nq + i, h)),
        pl.BlockSpec((seq, hw), lambda b, h, i: (b + b0, h)),
        pl.BlockSpec((seq, hw), lambda b, h, i: (b + b0, h)),
        pl.BlockSpec((1, hw), lambda b, h, i: (0, 0)),
    ]
    args = [lam, q, k, v, g_sub.reshape(1, hw)]
    cache = cache_k is not None
    if cache:
        past = cache_k.shape[2]
        ck = cache_k.reshape(cache_k.shape[0], cache_k.shape[1], past, 2 * DIFF_HEADS * HD)
        cv = cache_v.reshape(ck.shape)
        in_specs += [pl.BlockSpec((None, None, past, hw), lambda b, h, i: (b, layer_j, 0, h))] * 2
        args += [ck, cv]
    aliases = {}
    if prev is not None:
        in_specs.append(pl.BlockSpec(memory_space=pl.ANY))
        args.append(prev)
        aliases = {len(args) - 1: 0}
    return pl.pallas_call(
        functools.partial(_diff_attn_kernel, cache=cache, out_scale=1.0 - lam_init,
                          aliased=prev is not None),
        grid=(nb, DIFF_HEADS, nq), in_specs=in_specs,
        out_specs=pl.BlockSpec((tq, hw), lambda b, h, i: (q0 + b * nq + i, h)),
        out_shape=jax.ShapeDtypeStruct((st.n, DIFF_HEADS * hw), BF16),
        input_output_aliases=aliases,
        compiler_params=_cparams(("parallel", "parallel", "parallel")), name="diff_attn",
    )(*args)


def _conv_kernel(x_ref, w_ref, b_ref, *rest):
    o_ref, pad_ref = rest[-2:]
    seq = x_ref.shape[0]
    half = SSD_CONV // 2
    zeros = jnp.zeros((8, x_ref.shape[1]), F32)
    pad_ref[0:8, :] = zeros
    pad_ref[seq + 8:seq + 16, :] = zeros
    pad_ref[8:seq + 8, :] = x_ref[...]
    xp = pad_ref[...]
    rows = seq + 16
    acc = jnp.broadcast_to(b_ref[...], x_ref.shape)
    for k in range(SSD_CONV):
        shifted = xp if k == half else pltpu.roll(xp, (half - k) % rows, 0)
        acc = acc + w_ref[k:k + 1, :] * shifted[8:8 + seq, :]
    o_ref[...] = _silu(acc)


def _ssd_conv(st, xbc, conv_w, conv_b, row0, nb, seq, prev):
    tc = 512
    assert row0 % seq == 0, "a segment must start on a multiple of its sequence length"
    b0 = row0 // seq
    in_specs = [pl.BlockSpec((seq, tc), lambda b, c: (b + b0, c)),
                pl.BlockSpec((SSD_CONV, tc), lambda b, c: (0, c)),
                pl.BlockSpec((1, tc), lambda b, c: (0, c))]
    args = [xbc, conv_w, conv_b.reshape(1, SSD_CONV_DIM)]
    aliases = {}
    if prev is not None:
        in_specs.append(pl.BlockSpec(memory_space=pl.ANY))
        args.append(prev)
        aliases = {len(args) - 1: 0}
    return pl.pallas_call(
        _conv_kernel, grid=(nb, SSD_CONV_DIM // tc),
        in_specs=in_specs,
        out_specs=pl.BlockSpec((seq, tc), lambda b, c: (b + b0, c)),
        out_shape=jax.ShapeDtypeStruct((st.n, SSD_CONV_DIM), F32),
        scratch_shapes=[pltpu.VMEM((seq + 16, tc), F32)],
        input_output_aliases=aliases,
        compiler_params=_cparams(("parallel", "parallel")), name="ssd_conv",
    )(*args)


def _scan_direction(x_ref, b_ref, c_ref, dt_ref, bias_ref, alog_ref, h0_ref, y_ref, hf_ref,
                    state_ref, acol_ref, arow_ref, dtrow_ref, wrow_ref, dec_ref, *, rev, nc):
    c = pl.program_id(1)
    q = SSD_CHUNK
    lane0 = SSD_HEADS if rev else 0
    r_heads = SSD_HEADS // SSD_GROUPS
    gw = r_heads * SSD_HEADDIM

    ii = lax.broadcasted_iota(jnp.int32, (q, q), 0)
    jj = lax.broadcasted_iota(jnp.int32, (q, q), 1)
    causal = (jj >= ii) if rev else (jj <= ii)

    @pl.when(c == 0)
    def _():
        if h0_ref is not None:
            for blk in range(SSD_INNER // 128):
                state_ref[:, blk * 128:(blk + 1) * 128] = h0_ref[blk * 128:(blk + 1) * 128, :].T
        else:
            state_ref[...] = jnp.zeros(state_ref.shape, F32)

    xdt = dt_ref[...] + bias_ref[...]
    dt = jnp.maximum(xdt, 0.0) + jnp.log(1.0 + jnp.exp(-jnp.abs(xdt)))
    la = dt * (-jnp.exp(alog_ref[...]))
    tri = causal.astype(BF16)
    tri_t = ((ii >= jj) if rev else (ii <= jj)).astype(BF16)
    la3 = _split3(la)
    acol_ref[...] = (jnp.dot(tri, la3[0], preferred_element_type=F32)
                     + jnp.dot(tri, la3[1], preferred_element_type=F32)
                     + jnp.dot(tri, la3[2], preferred_element_type=F32))
    arow = _dot3(_split3(la.T), tri_t)
    end = 0 if rev else q - 1
    a_end = jnp.broadcast_to(arow[:, end:end + 1], arow.shape)
    dt_t = dt.T
    arow_ref[...] = arow
    dtrow_ref[...] = dt_t
    wrow_ref[...] = jnp.exp(a_end - arow) * dt_t
    dec_ref[...] = jnp.exp(a_end)

    lane_lo = lax.broadcasted_iota(jnp.int32, (1, 128), 1) < SSD_HEADDIM

    for g in range(SSD_GROUPS):
        bmat = b_ref[:, g * SSD_STATE:(g + 1) * SSD_STATE]
        cmat = c_ref[:, g * SSD_STATE:(g + 1) * SSD_STATE]
        cb = lax.dot_general(cmat.astype(BF16), bmat.astype(BF16), (((1,), (1,)), ((), ())),
                             preferred_element_type=F32)
        b_t = bmat.T
        for pair in range(r_heads // 2):
            col = g * gw + pair * 128
            x_pair = x_ref[:, col:col + 128].astype(BF16)
            st_old = state_ref[:, col:col + 128]
            rhs = jnp.concatenate([x_pair, st_old.astype(BF16)], axis=0)
            ys, upds = [], []
            for sub in range(2):
                hd = lane0 + g * r_heads + 2 * pair + sub
                a_i = jnp.broadcast_to(acol_ref[:, hd:hd + 1], (q, q))
                seg = a_i - arow_ref[hd:hd + 1, :]
                decay = jnp.exp(jnp.where(causal, seg, -jnp.inf))
                m_intra = (cb * decay * dtrow_ref[hd:hd + 1, :]).astype(BF16)
                c_scaled = (cmat * jnp.exp(a_i)).astype(BF16)
                bw_t = (b_t * wrow_ref[hd:hd + 1, :]).astype(BF16)
                ys.append(jnp.dot(jnp.concatenate([m_intra, c_scaled], axis=1), rhs,
                                  preferred_element_type=F32))
                upds.append(jnp.dot(bw_t, x_pair, preferred_element_type=F32))
            hd0 = lane0 + g * r_heads + 2 * pair
            y_ref[:, col:col + 128] = jnp.where(lane_lo, ys[0], ys[1]).astype(y_ref.dtype)
            dec_pair = jnp.where(lane_lo, dec_ref[hd0:hd0 + 1, :], dec_ref[hd0 + 1:hd0 + 2, :])
            state_ref[:, col:col + 128] = dec_pair * st_old + jnp.where(lane_lo, upds[0], upds[1])

    if hf_ref is not None:
        @pl.when(c == nc - 1)
        def _():
            for blk in range(SSD_INNER // 128):
                hf_ref[blk * 128:(blk + 1) * 128, :] = state_ref[:, blk * 128:(blk + 1) * 128].T


def _ssd_scan_kernel(*refs, nc, has_init, want_final, aliased):
    fwd_in, bwd_in, (bias_ref, alog_ref) = refs[0:4], refs[4:8], refs[8:10]
    pos = 10
    h0 = (None, None)
    if has_init:
        h0 = refs[pos:pos + 2]; pos += 2
    if aliased:
        pos += 2
    ys = refs[pos:pos + 2]; pos += 2
    hf = (None, None)
    if want_final:
        hf = refs[pos:pos + 2]; pos += 2
    scratch = refs[pos:]
    per = len(scratch) // 2
    for d, ins in enumerate((fwd_in, bwd_in)):
        _scan_direction(*ins, bias_ref, alog_ref, h0[d], ys[d], hf[d],
                        *scratch[d * per:(d + 1) * per], rev=d == 1, nc=nc)


def _ssd_scan(st, xbc, dt, dt_bias, a_log, row0, nb, seq, h0s, layer_j, want_final, prevs):
    nc = seq // SSD_CHUNK
    c0 = row0 // SSD_CHUNK

    def chunk(rev):
        return lambda b, c: c0 + b * nc + (nc - 1 - c if rev else c)

    in_specs, args = [], []
    for rev in (False, True):
        ch = chunk(rev)
        in_specs += [
            pl.BlockSpec((SSD_CHUNK, SSD_INNER), lambda b, c, ch=ch: (ch(b, c), 0)),
            pl.BlockSpec((SSD_CHUNK, SSD_GN), lambda b, c, ch=ch: (ch(b, c), SSD_INNER // SSD_GN)),
            pl.BlockSpec((SSD_CHUNK, SSD_GN), lambda b, c, ch=ch: (ch(b, c), SSD_INNER // SSD_GN + 1)),
            pl.BlockSpec((SSD_CHUNK, 128), lambda b, c, ch=ch: (ch(b, c), 0)),
        ]
        args += [xbc, xbc, xbc, dt]
    in_specs += [pl.BlockSpec((1, 128), lambda b, c: (0, 0))] * 2
    args += [dt_bias.reshape(1, 128), a_log.reshape(1, 128)]
    if h0s is not None:
        for h0 in h0s:
            in_specs.append(pl.BlockSpec((None, None, SSD_INNER, SSD_STATE),
                                         lambda b, c: (b, layer_j, 0, 0)))
            args.append(h0.reshape(h0.shape[0], h0.shape[1], SSD_INNER, SSD_STATE))
    aliases = {}
    if prevs is not None:
        for d, prev in enumerate(prevs):
            in_specs.append(pl.BlockSpec(memory_space=pl.ANY))
            args.append(prev)
            aliases[len(args) - 1] = d
    out_shape = [jax.ShapeDtypeStruct((st.n, SSD_INNER), BF16)] * 2
    out_specs = [pl.BlockSpec((SSD_CHUNK, SSD_INNER), lambda b, c, ch=chunk(rev): (ch(b, c), 0))
                 for rev in (False, True)]
    if want_final:
        out_shape += [jax.ShapeDtypeStruct((nb, SSD_INNER, SSD_STATE), F32)] * 2
        out_specs += [pl.BlockSpec((None, SSD_INNER, SSD_STATE), lambda b, c: (b, 0, 0))] * 2
    per_dir = [pltpu.VMEM((SSD_STATE, SSD_INNER), F32),
               pltpu.VMEM((SSD_CHUNK, 128), F32)] + [pltpu.VMEM((128, SSD_CHUNK), F32)] * 4
    return pl.pallas_call(
        functools.partial(_ssd_scan_kernel, nc=nc, has_init=h0s is not None,
                          want_final=want_final, aliased=prevs is not None),
        grid=(nb, nc), in_specs=in_specs, out_specs=out_specs, out_shape=out_shape,
        scratch_shapes=per_dir * 2,
        input_output_aliases=aliases,
        compiler_params=_cparams(("parallel", "arbitrary")), name="ssd_scan",
    )(*args)


def _ssd_gate_kernel(yf_ref, yb_ref, x_ref, z_ref, d_ref, g_ref, o_ref):
    gw = SSD_INNER // SSD_GROUPS
    for grp in range(SSD_GROUPS):
        sl = slice(grp * gw, (grp + 1) * gw)
        y = (yf_ref[:, sl].astype(F32) + yb_ref[:, sl].astype(F32)) + d_ref[:, sl] * x_ref[:, sl]
        gated = y * _silu(z_ref[:, sl].astype(F32))
        normed = gated * lax.rsqrt(jnp.mean(gated * gated, axis=-1, keepdims=True) + EPS)
        o_ref[:, sl] = (normed * g_ref[:, sl]).astype(o_ref.dtype)


def _ssd_gate(yf, yb, xbc, z, d_sum, g_norm):
    n = yf.shape[0]
    tm = 256
    row = pl.BlockSpec((tm, SSD_INNER), lambda i: (i, 0))
    vec = pl.BlockSpec((1, SSD_INNER), lambda i: (0, 0))
    return pl.pallas_call(
        _ssd_gate_kernel, grid=(n // tm,),
        in_specs=[row, row, row, row, vec, vec], out_specs=row,
        out_shape=jax.ShapeDtypeStruct((n, SSD_INNER), BF16),
        compiler_params=_cparams(("parallel",)), name="ssd_gate",
    )(yf, yb, xbc, z, d_sum, g_norm.reshape(1, SSD_INNER))


def _moe_up_kernel(be_ref, nu_ref, a_ref, wg_ref, wu_ref, *rest, blk0):
    o_ref = rest[-1]
    m = pl.program_id(1) + blk0

    @pl.when(m < nu_ref[0])
    def _():
        a = a_ref[...]
        gate = jnp.dot(a, wg_ref[...].astype(BF16), preferred_element_type=F32)
        up = jnp.dot(a, wu_ref[...].astype(BF16), preferred_element_type=F32)
        o_ref[...] = (_silu(gate) * up).astype(o_ref.dtype)

    @pl.when(m >= nu_ref[0])
    def _():
        o_ref[...] = jnp.zeros(o_ref.shape, o_ref.dtype)


def _moe_down_kernel(be_ref, nu_ref, a_ref, w_ref, o_ref):
    m = pl.program_id(0)

    @pl.when(m < nu_ref[0])
    def _():
        o_ref[...] = jnp.dot(a_ref[...], w_ref[...].astype(BF16),
                             preferred_element_type=F32).astype(o_ref.dtype)

    @pl.when(m >= nu_ref[0])
    def _():
        o_ref[...] = jnp.zeros(o_ref.shape, o_ref.dtype)


def _moe(st, h, logits, w_gu, w_down, f):
    n = st.n
    tb = MOE_BLOCK
    n_assign = n * TOP_K
    n_blocks = n_assign // tb + N_EXPERTS
    slots = n_blocks * tb
    top_logit, top_e = lax.top_k(logits, TOP_K)
    gates = jax.nn.softmax(top_logit, axis=-1)
    flat_e = top_e.reshape(-1).astype(jnp.int32)
    onehot = (flat_e[:, None] == jnp.arange(N_EXPERTS, dtype=jnp.int32)[None, :]).astype(jnp.int32)
    rank = jnp.sum((jnp.cumsum(onehot, axis=0) - onehot) * onehot, axis=1)
    counts = jnp.sum(onehot, axis=0)
    padded = (counts + tb - 1) // tb * tb
    pad_end = jnp.cumsum(padded)
    pad_start = pad_end - padded
    dest = pad_start[flat_e] + rank
    block_start = jnp.arange(n_blocks, dtype=jnp.int32) * tb
    block_e = jnp.minimum(jnp.searchsorted(pad_end, block_start, side="right"),
                          N_EXPERTS - 1).astype(jnp.int32)
    n_used = (pad_end[-1] // tb).astype(jnp.int32).reshape(1)
    order = jnp.argsort(flat_e, stable=True).astype(jnp.int32)
    slot_e = jnp.repeat(block_e, tb)
    slot_r = jnp.arange(slots, dtype=jnp.int32) - pad_start[slot_e]
    filled = slot_r < counts[slot_e]
    src = jnp.where(filled, (jnp.cumsum(counts) - counts)[slot_e] + slot_r, 0)
    slot_tok = jnp.where(filled, order[src] // TOP_K, 0)

    tn = 512
    nf = EXPERT_DIM // tn

    def live_block(m, nu):
        return jnp.minimum(m, nu[0] - 1)

    head = min(MOE_HEAD_BLOCKS, n_blocks // 2)
    act = None
    for blk0, nblk in ((0, head), (head, n_blocks - head)):
        xs = h.at[slot_tok[blk0 * tb:(blk0 + nblk) * tb]].get(mode="promise_in_bounds")

        def local(m, nu, blk0=blk0, nblk=nblk):
            return jnp.clip(live_block(m + blk0, nu) - blk0, 0, nblk - 1)

        def expert(m, be, nu, blk0=blk0):
            return be[live_block(m + blk0, nu)]

        in_specs = [
            pl.BlockSpec((tb, D_MODEL), lambda j, m, be, nu, local=local: (local(m, nu), 0)),
            pl.BlockSpec((None, None, D_MODEL, tn),
                         lambda j, m, be, nu, expert=expert: (f, expert(m, be, nu), 0, j)),
            pl.BlockSpec((None, None, D_MODEL, tn),
                         lambda j, m, be, nu, expert=expert: (f, expert(m, be, nu), 0, j + nf)),
        ]
        args = [block_e, n_used, xs, w_gu, w_gu]
        aliases = {}
        if act is not None:
            in_specs.append(pl.BlockSpec(memory_space=pl.ANY))
            args.append(act)
            aliases = {len(args) - 1: 0}
        act = pl.pallas_call(
            functools.partial(_moe_up_kernel, blk0=blk0),
            grid_spec=pltpu.PrefetchScalarGridSpec(
                num_scalar_prefetch=2, grid=(nf, nblk), in_specs=in_specs,
                out_specs=pl.BlockSpec((tb, tn), lambda j, m, be, nu, blk0=blk0: (m + blk0, j)),
            ),
            out_shape=jax.ShapeDtypeStruct((slots, EXPERT_DIM), BF16),
            input_output_aliases=aliases,
            compiler_params=_cparams(("parallel", "arbitrary")), name="moe_up",
        )(*args)

    tn2 = 256
    yb = pl.pallas_call(
        _moe_down_kernel,
        grid_spec=pltpu.PrefetchScalarGridSpec(
            num_scalar_prefetch=2, grid=(n_blocks, D_MODEL // tn2),
            in_specs=[
                pl.BlockSpec((tb, EXPERT_DIM), lambda m, j, be, nu: (live_block(m, nu), 0)),
                pl.BlockSpec((None, None, EXPERT_DIM, tn2),
                             lambda m, j, be, nu: (f, be[live_block(m, nu)], 0, j)),
            ],
            out_specs=pl.BlockSpec((tb, tn2), lambda m, j, be, nu: (m, j)),
        ),
        out_shape=jax.ShapeDtypeStruct((slots, D_MODEL), BF16),
        compiler_params=_cparams(("arbitrary", "arbitrary")), name="moe_down",
    )(block_e, n_used, act, w_down)

    dest2 = dest.reshape(n, TOP_K)
    y0 = yb.at[dest2[:, 0]].get(mode="promise_in_bounds")
    y1 = yb.at[dest2[:, 1]].get(mode="promise_in_bounds")
    return y0, y1, gates


def _combine_kernel(x_ref, y0_ref, y1_ref, g_ref, gate_ref, *rest, norm):
    y = g_ref[:, 0:1] * y0_ref[...].astype(F32) + g_ref[:, 1:2] * y1_ref[...].astype(F32)
    xn = x_ref[...] + gate_ref[...] * y
    if norm:
        gn_ref, shift_ref, scale_ref, o_ref, h_ref = rest
        _norm_mod_store(xn, gn_ref, shift_ref, scale_ref, h_ref)
    else:
        (o_ref,) = rest
    o_ref[...] = xn


def _combine(st, x, y0, y1, gates, mods, layer, g_next=None):
    tm = st.tile(512)
    row = pl.BlockSpec((tm, D_MODEL), lambda i: (i, 0))

    def mod(lyr, which):
        return pl.BlockSpec((None, 1, D_MODEL),
                            lambda i: (_mod_index(lyr, which, st.mod_row(i, tm)), 0, 0))

    in_specs = [row, row, row, pl.BlockSpec((tm, TOP_K), lambda i: (i, 0)), mod(layer, 5)]
    args = [x, y0, y1, gates, mods]
    out_shape = [jax.ShapeDtypeStruct((st.n, D_MODEL), F32)]
    out_specs = [row]
    if g_next is not None:
        in_specs += [pl.BlockSpec((1, D_MODEL), lambda i: (0, 0)), mod(layer + 1, 0), mod(layer + 1, 1)]
        args += [g_next.reshape(1, D_MODEL), mods, mods]
        out_shape.append(jax.ShapeDtypeStruct((st.n, D_MODEL), BF16))
        out_specs.append(row)
    res = pl.pallas_call(
        functools.partial(_combine_kernel, norm=g_next is not None), grid=(st.n // tm,),
        in_specs=in_specs, out_specs=out_specs, out_shape=out_shape,
        compiler_params=_cparams(("parallel",)), name="moe_combine",
    )(*args)
    return res if g_next is not None else (res[0], None)


def kernel(x_prompt, x_sample, cache_attn_k, cache_attn_v, cache_diff_k, cache_diff_v, state_ssd_fwd, state_ssd_bwd, c, c_ctx, w_ada, b_ada, g_mix, g_ffn, g_final, attn_w_in, attn_g_q, attn_g_k, attn_w_out, diff_w_in, diff_lambda, diff_g_sub, diff_w_out, ssd_w_in, ssd_conv_w, ssd_conv_b, ssd_dt_bias, ssd_a_log, ssd_d, ssd_g_norm, ssd_w_out, ffn_w_gu, ffn_w_down, moe_w_router, moe_w_gu, moe_w_down):
    pb, pl_len, _ = x_prompt.shape
    sb, sl_len, _ = x_sample.shape
    st = _Stream(pb, pl_len, sb, sl_len)
    np_, ns = st.n_prompt, st.n_sample
    tm = st.tile(1024)

    x = jnp.concatenate([x_prompt.reshape(np_, D_MODEL), x_sample.reshape(ns, D_MODEL)], axis=0)

    cond = jnp.zeros((MOD_ROWS, D_MODEL), F32).at[:sb].set(c).at[CTX_ROW].set(c_ctx)
    cond_act = jax.nn.silu(cond).astype(BF16)
    mods = []
    for i in range(DEPTH):
        m = _matmul(cond_act, w_ada, (i,), n_off=0, n_out=6 * D_MODEL, tm=MOD_ROWS, tn=1024,
                    out_dtype=F32, name="ada")
        mods.append(m + b_ada[i][None, :])
    mods = jnp.stack(mods).reshape(DEPTH * MOD_ROWS * 6, 1, D_MODEL)

    tables = _rope_tables(sl_len, tm)
    prep_tables = _rope_tables(sl_len, GQA_PREP_ROWS)

    def table_index(i):
        return _table_block(st, i, tm)

    tm2 = st.tile(2048)

    def gate_index(layer, which, tile):
        return lambda i: _mod_index(layer, which, st.mod_row(i, tile))

    new_attn_k, new_attn_v, new_diff_k, new_diff_v, new_ssd_f, new_ssd_b = [], [], [], [], [], []
    def out_proj(o, w16, x, layer, fuse_norm):
        if fuse_norm:
            return _out_proj_norm(st, o, w16, x, mods, layer, g_ffn[layer])
        xo = _matmul(o, w16, (), n_off=0, n_out=D_MODEL, tm=tm2, tn=512,
                     out_dtype=F32, epilogue="resid", resid=x, mods=mods,
                     gate_index=gate_index(layer, 2, tm2), name="attn_out")
        return xo, None

    h_pre = None
    for i in range(DEPTH):
        kind, j = i % 3, i // 3
        f = i // 2
        w_router = moe_w_router[f] if i % 2 == 1 else None
        h = h_pre if h_pre is not None else _norm_mod(st, x, g_mix[i], mods, i, 0)
        h_pre = hf = logits = None
        if kind == 0:
            qkv = _matmul(h, attn_w_in, (j,), n_off=0, n_out=attn_w_in.shape[-1], tm=tm, tn=1024,
                          out_dtype=F32, name="attn_in")
            k = _gqa_kprep(st, qkv, attn_g_k[j], prep_tables)
            o = _gqa_attn(st, qkv, attn_g_q[j], k, 0, pb, pl_len, None, None, None, j, None)
            o = _gqa_attn(st, qkv, attn_g_q[j], k, np_, sb, sl_len, tables,
                          cache_attn_k, cache_attn_v, j, o)
            new_attn_k.append(k[:np_].reshape(pb, pl_len, ATTN_KV_HEADS, HD))
            new_attn_v.append(qkv[:np_, (ATTN_HEADS + ATTN_KV_HEADS) * HD:]
                              .reshape(pb, pl_len, ATTN_KV_HEADS, HD))
            x, hf = out_proj(o, attn_w_out[j].astype(BF16), x, i, w_router is None)
        elif kind == 1:
            lam_init = 0.8 - 0.6 * math.exp(-0.3 * i)
            lf = diff_lambda[j]
            lam = (jnp.exp(jnp.sum(lf[0] * lf[1])) - jnp.exp(jnp.sum(lf[2] * lf[3])) + lam_init).reshape(1)
            w = 2 * DIFF_HEADS * HD
            q = _matmul(h, diff_w_in, (j,), n_off=0, n_out=w, tm=tm, tn=1024, out_dtype=BF16,
                        epilogue="rope", scale=Q_SCALE,
                        tables=tables, table_index=table_index, name="diff_in_q")
            k = _matmul(h, diff_w_in, (j,), n_off=w, n_out=w, tm=tm, tn=1024, out_dtype=BF16,
                        epilogue="rope", tables=tables, table_index=table_index, name="diff_in_k")
            v = _matmul(h, diff_w_in, (j,), n_off=2 * w, n_out=w, tm=tm, tn=1024, out_dtype=F32,
                        name="diff_in_v")
            k_ctx = _matmul(h, diff_w_in, (j,), n_off=w, n_out=w, tm=tm, tn=1024, rows=np_,
                            out_dtype=F32, name="diff_in_kctx")
            o = _diff_attn(st, lam, q, k, v, diff_g_sub[j], lam_init, 0, pb, pl_len, None, None, j, None)
            o = _diff_attn(st, lam, q, k, v, diff_g_sub[j], lam_init, np_, sb, sl_len,
                           cache_diff_k, cache_diff_v, j, o)
            new_diff_k.append(k_ctx.reshape(pb, pl_len, 2 * DIFF_HEADS, HD))
            new_diff_v.append(v[:np_].reshape(pb, pl_len, DIFF_HEADS, 2 * HD))
            x, hf = out_proj(o, diff_w_out[j].astype(BF16), x, i, w_router is None)
        else:
            z = _matmul(h, ssd_w_in, (j,), n_off=0, n_out=SSD_INNER, tm=tm, tn=1024,
                        out_dtype=BF16, name="ssd_in_z")
            xbc = _matmul(h, ssd_w_in, (j,), n_off=SSD_INNER, n_out=SSD_CONV_DIM, tm=tm, tn=1024,
                          out_dtype=F32, name="ssd_in_xbc")
            dt = _matmul(h, ssd_w_in, (j,), n_off=SSD_INNER + SSD_CONV_DIM, n_out=2 * SSD_HEADS,
                         tm=tm, tn=128, out_dtype=F32, name="ssd_in_dt")
            xc = _ssd_conv(st, xbc, ssd_conv_w[j], ssd_conv_b[j], 0, pb, pl_len, None)
            xc = _ssd_conv(st, xbc, ssd_conv_w[j], ssd_conv_b[j], np_, sb, sl_len, xc)
            yf, yb, hfin_f, hfin_b = _ssd_scan(st, xc, dt, ssd_dt_bias[j], ssd_a_log[j], 0, pb, pl_len,
                                               None, j, True, None)
            ys = _ssd_scan(st, xc, dt, ssd_dt_bias[j], ssd_a_log[j], np_, sb, sl_len,
                           (state_ssd_fwd, state_ssd_bwd), j, False, (yf, yb))
            new_ssd_f.append(hfin_f.reshape(pb, SSD_HEADS, SSD_HEADDIM, SSD_STATE))
            new_ssd_b.append(hfin_b.reshape(pb, SSD_HEADS, SSD_HEADDIM, SSD_STATE))
            d_sum = jnp.repeat(ssd_d[j][0] + ssd_d[j][1], SSD_HEADDIM).reshape(1, SSD_INNER)
            o = _ssd_gate(ys[0], ys[1], xc, z, d_sum, ssd_g_norm[j])
            x = _matmul(o, ssd_w_out[j].astype(BF16), (), n_off=0, n_out=D_MODEL, tm=tm, tn=512,
                        out_dtype=F32, epilogue="resid", resid=x, mods=mods,
                        gate_index=gate_index(i, 2, tm), name="ssd_out")

        if hf is None:
            if w_router is not None:
                hf, logits = _norm_mod(st, x, g_ffn[i], mods, i, 3, w_router=w_router)
            else:
                hf = _norm_mod(st, x, g_ffn[i], mods, i, 3)
        if i % 2 == 0:
            act = _matmul(hf, ffn_w_gu, (f,), n_off=0, n_out=FFN_DIM, tm=tm, tn=512,
                          out_dtype=BF16, epilogue="swiglu", up_off=FFN_DIM, name="ffn_up")
            x = _matmul(act, ffn_w_down[f].astype(BF16), (), n_off=0, n_out=D_MODEL, tm=tm, tn=512,
                        out_dtype=F32, epilogue="resid", resid=x, mods=mods,
                        gate_index=gate_index(i, 5, tm), name="ffn_down")
        else:
            y0, y1, gates = _moe(st, hf, logits[:, :N_EXPERTS], moe_w_gu, moe_w_down, f)
            x, h_pre = _combine(st, x, y0, y1, gates, mods, i,
                                g_next=g_mix[i + 1] if i + 1 < DEPTH else None)

    y_prompt = _final_norm(x, g_final, 0, np_).reshape(pb, pl_len, D_MODEL)
    y_sample = _final_norm(x, g_final, np_, ns).reshape(sb, sl_len, D_MODEL)
    return (y_prompt, y_sample,
            jnp.stack(new_attn_k, axis=1), jnp.stack(new_attn_v, axis=1),
            jnp.stack(new_diff_k, axis=1), jnp.stack(new_diff_v, axis=1),
            jnp.stack(new_ssd_f, axis=1), jnp.stack(new_ssd_b, axis=1))
```

```python
import functools
import math

import jax
import jax.numpy as jnp
from jax import lax
from jax.experimental import pallas as pl
from jax.experimental.pallas import tpu as pltpu

F32 = jnp.float32
BF16 = jnp.bfloat16

D_MODEL = 2048
DEPTH = 4
GRID_W = 64
ROPE_THETA = 10000.0
EPS = 1e-6
HD = 128
Q_SCALE = math.log2(math.e) / math.sqrt(HD)
ATTN_HEADS = 16
ATTN_KV_HEADS = 4
DIFF_HEADS = 8
SSD_INNER = 2 * D_MODEL
SSD_HEADDIM = 64
SSD_HEADS = SSD_INNER // SSD_HEADDIM
SSD_GROUPS = 8
SSD_STATE = 128
SSD_CONV = 5
SSD_CHUNK = 128
SSD_GN = SSD_GROUPS * SSD_STATE
SSD_CONV_DIM = SSD_INNER + 2 * SSD_GN
FFN_DIM = 5632
N_EXPERTS = 8
TOP_K = 2
EXPERT_DIM = 7168

MOD_ROWS = 16
CTX_ROW = 8
MOE_BLOCK = 1024
MOE_HEAD_BLOCKS = 8
VMEM_LIMIT = 56 * 1024 * 1024


def _cparams(sem):
    return pltpu.CompilerParams(dimension_semantics=sem, vmem_limit_bytes=VMEM_LIMIT)


def _silu(x):
    return x / (1.0 + jnp.exp(-x))


def _split3(x):
    hi = x.astype(BF16)
    r = x - hi.astype(F32)
    mid = r.astype(BF16)
    lo = (r - mid.astype(F32)).astype(BF16)
    return hi, mid, lo


def _dot3(a3, b):
    out = jnp.dot(a3[0], b, preferred_element_type=F32)
    out = out + jnp.dot(a3[1], b, preferred_element_type=F32)
    return out + jnp.dot(a3[2], b, preferred_element_type=F32)


class _Stream:
    def __init__(self, n_prompt_batch, prompt_len, n_sample_batch, sample_len):
        self.pb, self.pl_, self.sb, self.sl = n_prompt_batch, prompt_len, n_sample_batch, sample_len
        self.n_prompt = n_prompt_batch * prompt_len
        self.n_sample = n_sample_batch * sample_len
        self.n = self.n_prompt + self.n_sample

    def mod_row(self, i, tm):
        pt = self.n_prompt // tm
        per = self.sl // tm
        return jnp.where(i < pt, CTX_ROW, (i - pt) // per)

    def tile(self, want):
        t = want
        while self.n_prompt % t or self.sl % t:
            t //= 2
        return t


def _mod_index(layer, which, row):
    return (layer * MOD_ROWS + row) * 6 + which


def _norm_mod_store(x, g_ref, shift_ref, scale_ref, o_ref, wr_ref=None, lg_ref=None):
    y = x * lax.rsqrt(jnp.mean(x * x, axis=-1, keepdims=True) + EPS) * g_ref[...]
    h = y * (1.0 + scale_ref[...]) + shift_ref[...]
    if wr_ref is not None:
        w = wr_ref[...]
        whi = w.astype(BF16)
        wlo = (w - whi.astype(F32)).astype(BF16)
        hhi = h.astype(BF16)
        hlo = (h - hhi.astype(F32)).astype(BF16)
        lg = jnp.dot(hhi, whi, preferred_element_type=F32)
        lg = lg + jnp.dot(hlo, whi, preferred_element_type=F32)
        lg_ref[...] = lg + jnp.dot(hhi, wlo, preferred_element_type=F32)
    o_ref[...] = h.astype(o_ref.dtype)


def _norm_mod_kernel(x_ref, g_ref, shift_ref, scale_ref, *rest, router):
    if router:
        wr_ref, o_ref, lg_ref = rest
        _norm_mod_store(x_ref[...], g_ref, shift_ref, scale_ref, o_ref, wr_ref, lg_ref)
    else:
        (o_ref,) = rest
        _norm_mod_store(x_ref[...], g_ref, shift_ref, scale_ref, o_ref)


def _router_pad(w_router):
    return jnp.zeros((D_MODEL, 128), F32).at[:, :N_EXPERTS].set(w_router)


def _out_norm_kernel(a_ref, w_ref, x_ref, gate_ref, g_ref, shift_ref, scale_ref, xo_ref, ho_ref):
    xn = x_ref[...] + gate_ref[...] * jnp.dot(a_ref[...], w_ref[...], preferred_element_type=F32)
    _norm_mod_store(xn, g_ref, shift_ref, scale_ref, ho_ref)
    xo_ref[...] = xn


def _out_proj_norm(st, a, w, x, mods, layer, g_next):
    tm = st.tile(512)
    kdim = a.shape[1]

    def mod(which):
        return pl.BlockSpec((None, 1, D_MODEL),
                            lambda i: (_mod_index(layer, which, st.mod_row(i, tm)), 0, 0))

    row = pl.BlockSpec((tm, D_MODEL), lambda i: (i, 0))
    in_specs = [pl.BlockSpec((tm, kdim), lambda i: (i, 0)),
                pl.BlockSpec((kdim, D_MODEL), lambda i: (0, 0), pipeline_mode=pl.Buffered(1)),
                row, mod(2), pl.BlockSpec((1, D_MODEL), lambda i: (0, 0)), mod(3), mod(4)]
    args = [a, w, x, mods, g_next.reshape(1, D_MODEL), mods, mods]
    out_shape = [jax.ShapeDtypeStruct((st.n, D_MODEL), F32), jax.ShapeDtypeStruct((st.n, D_MODEL), BF16)]
    return pl.pallas_call(
        _out_norm_kernel,
        grid=(st.n // tm,), in_specs=in_specs, out_specs=[row, row], out_shape=out_shape,
        compiler_params=_cparams(("parallel",)), name="out_proj_norm",
    )(*args)


def _norm_mod(st, x, g, mods, layer, which_shift, w_router=None):
    tm = st.tile(512)
    nt = st.n // tm
    g2 = g.reshape(1, D_MODEL)
    in_specs = [
        pl.BlockSpec((tm, D_MODEL), lambda i: (i, 0)),
        pl.BlockSpec((1, D_MODEL), lambda i: (0, 0)),
        pl.BlockSpec((None, 1, D_MODEL),
                     lambda i: (_mod_index(layer, which_shift, st.mod_row(i, tm)), 0, 0)),
        pl.BlockSpec((None, 1, D_MODEL),
                     lambda i: (_mod_index(layer, which_shift + 1, st.mod_row(i, tm)), 0, 0)),
    ]
    args = [x, g2, mods, mods]
    out_shape = [jax.ShapeDtypeStruct((st.n, D_MODEL), BF16)]
    out_specs = [pl.BlockSpec((tm, D_MODEL), lambda i: (i, 0))]
    if w_router is not None:
        in_specs.append(pl.BlockSpec((D_MODEL, 128), lambda i: (0, 0)))
        args.append(_router_pad(w_router))
        out_shape.append(jax.ShapeDtypeStruct((st.n, 128), F32))
        out_specs.append(pl.BlockSpec((tm, 128), lambda i: (i, 0)))
    res = pl.pallas_call(
        functools.partial(_norm_mod_kernel, router=w_router is not None),
        grid=(nt,), in_specs=in_specs, out_specs=out_specs, out_shape=out_shape,
        compiler_params=_cparams(("parallel",)), name="norm_mod",
    )(*args)
    return res if w_router is not None else res[0]


def _final_norm_kernel(x_ref, g_ref, o_ref):
    x = x_ref[...]
    o_ref[...] = x * lax.rsqrt(jnp.mean(x * x, axis=-1, keepdims=True) + EPS) * g_ref[...]


def _final_norm(x, g, row0, nrows):
    tm = 512
    r0 = row0 // tm
    return pl.pallas_call(
        _final_norm_kernel, grid=(nrows // tm,),
        in_specs=[pl.BlockSpec((tm, D_MODEL), lambda i: (i + r0, 0)),
                  pl.BlockSpec((1, D_MODEL), lambda i: (0, 0))],
        out_specs=pl.BlockSpec((tm, D_MODEL), lambda i: (i, 0)),
        out_shape=jax.ShapeDtypeStruct((nrows, D_MODEL), F32),
        compiler_params=_cparams(("parallel",)), name="final_norm",
    )(x, g.reshape(1, D_MODEL))


def _mm_kernel(*refs, epilogue, scale):
    a_ref, w_ref = refs[0], refs[1]
    o_ref = refs[-1]
    a = a_ref[...]
    y = jnp.dot(a, w_ref[...].astype(BF16), preferred_element_type=F32)
    if epilogue == "swiglu":
        up = jnp.dot(a, refs[2][...].astype(BF16), preferred_element_type=F32)
        o_ref[...] = (_silu(y) * up).astype(o_ref.dtype)
    elif epilogue == "resid":
        x_ref, gate_ref = refs[2], refs[3]
        o_ref[...] = (x_ref[...] + gate_ref[...] * y).astype(o_ref.dtype)
    elif epilogue == "rope":
        cos, sin = refs[2][...], refs[3][...]
        for h in range(y.shape[1] // HD):
            yh = _rope(y[:, h * HD:(h + 1) * HD], cos, sin)
            o_ref[:, h * HD:(h + 1) * HD] = (yh * scale).astype(o_ref.dtype)
    else:
        o_ref[...] = y.astype(o_ref.dtype)


def _matmul(a, w, widx, *, n_off, n_out, tm, tn, out_dtype, epilogue="plain", up_off=None,
            resid=None, mods=None, gate_index=None, scale=1.0, tables=None, table_index=None,
            rows=None, name="mm"):
    m, kdim = (a.shape[0] if rows is None else rows), a.shape[1]
    assert m % tm == 0 and n_out % tn == 0 and n_off % tn == 0
    nb = n_off // tn
    lead = (None,) * len(widx)
    in_specs = [
        pl.BlockSpec((tm, kdim), lambda i, j: (i, 0)),
        pl.BlockSpec(lead + (kdim, tn), lambda i, j: widx + (0, j + nb)),
    ]
    args = [a, w]
    if epilogue == "swiglu":
        ub = up_off // tn
        in_specs.append(pl.BlockSpec(lead + (kdim, tn), lambda i, j: widx + (0, j + ub)))
        args.append(w)
    if epilogue == "resid":
        in_specs.append(pl.BlockSpec((tm, tn), lambda i, j: (i, j)))
        in_specs.append(pl.BlockSpec((None, 1, tn), lambda i, j: (gate_index(i), 0, j)))
        args += [resid, mods]
    if epilogue == "rope":
        in_specs += [pl.BlockSpec((tm, HD), lambda i, j: (table_index(i), 0))] * 2
        args += list(tables)
    return pl.pallas_call(
        functools.partial(_mm_kernel, epilogue=epilogue, scale=scale),
        grid=(m // tm, n_out // tn),
        in_specs=in_specs,
        out_specs=pl.BlockSpec((tm, tn), lambda i, j: (i, j)),
        out_shape=jax.ShapeDtypeStruct((m, n_out), out_dtype),
        compiler_params=_cparams(("parallel", "parallel")), name=name,
    )(*args)


def _rope_tables(length, ident_rows):
    nf = HD // 4
    inv = ROPE_THETA ** (-jnp.arange(nf, dtype=F32) / nf)
    rows = length // GRID_W
    row = jnp.repeat(jnp.arange(rows, dtype=F32), GRID_W)
    col = jnp.tile(jnp.arange(GRID_W, dtype=F32), rows)
    ar = row[:, None] * inv
    ac = col[:, None] * inv
    ang = jnp.concatenate([ar, ar, ac, ac], axis=-1)
    cos, sin = jnp.cos(ang), jnp.sin(ang)
    lane = jnp.arange(HD) % (HD // 2)
    sin_signed = jnp.where(lane >= HD // 4, sin, -sin)
    ones = jnp.ones((ident_rows, HD), F32)
    zeros = jnp.zeros((ident_rows, HD), F32)
    return jnp.concatenate([cos, ones]), jnp.concatenate([sin_signed, zeros])


def _table_block(st, i, tm):
    pt = st.n_prompt // tm
    per = st.sl // tm
    return jnp.where(i < pt, per, (i - pt) % per)


def _rope(y, cos, sin_signed):
    partner = lax.broadcasted_iota(jnp.int32, y.shape, 1) ^ (HD // 4)
    return y * cos + jnp.take_along_axis(y, partner, axis=1) * sin_signed


GQA_PREP_ROWS = 256


def _head_norm(x, g):
    return x * lax.rsqrt(jnp.mean(x * x, axis=-1, keepdims=True) + EPS) * g


def _gqa_kprep_kernel(k_ref, gk_ref, cos_ref, sin_ref, ko_ref):
    cos, sin = cos_ref[...], sin_ref[...]
    for h in range(ATTN_KV_HEADS):
        ko_ref[:, h * HD:(h + 1) * HD] = _rope(_head_norm(k_ref[:, h * HD:(h + 1) * HD], gk_ref[...]),
                                               cos, sin)


def _gqa_kprep(st, qkv, g_k, tables):
    tm = GQA_PREP_ROWS
    qw = ATTN_HEADS * HD
    kw = ATTN_KV_HEADS * HD
    tab = pl.BlockSpec((tm, HD), lambda i: (_table_block(st, i, tm), 0))
    return pl.pallas_call(
        _gqa_kprep_kernel, grid=(st.n // tm,),
        in_specs=[pl.BlockSpec((tm, kw), lambda i: (i, qw // kw)),
                  pl.BlockSpec((1, HD), lambda i: (0, 0)), tab, tab],
        out_specs=pl.BlockSpec((tm, kw), lambda i: (i, 0)),
        out_shape=jax.ShapeDtypeStruct((st.n, kw), F32),
        compiler_params=_cparams(("parallel",)), name="gqa_kprep",
    )(qkv, g_k.reshape(1, HD), *tables)


def _softmax_parts(q, ks):
    ss = [lax.dot_general(q, k, (((1,), (1,)), ((), ())), preferred_element_type=F32) for k in ks]
    m = ss[0].max(axis=-1, keepdims=True)
    for s in ss[1:]:
        m = jnp.maximum(m, s.max(axis=-1, keepdims=True))
    ps = [jnp.exp2(s - m) for s in ss]
    l = ps[0].sum(axis=-1, keepdims=True)
    for p in ps[1:]:
        l = l + p.sum(axis=-1, keepdims=True)
    return ps, l


def _gqa_attn_kernel(q_ref, gq_ref, k_ref, v_ref, *rest, cache, aliased, rope):
    rest = list(rest)
    o_ref = rest.pop()
    if aliased:
        rest.pop()
    if rope:
        cos, sin = rest[0][...], rest[1][...]
        rest = rest[2:]
    if cache:
        ck_ref, cv_ref = rest
        ks = [ck_ref[...].astype(BF16), k_ref[...].astype(BF16)]
        vs = [cv_ref[...].astype(BF16), v_ref[...].astype(BF16)]
    else:
        ks = [k_ref[...].astype(BF16)]
        vs = [v_ref[...].astype(BF16)]
    grp = ATTN_HEADS // ATTN_KV_HEADS
    for h in range(grp):
        y = _head_norm(q_ref[:, h * HD:(h + 1) * HD], gq_ref[...])
        if rope:
            y = _rope(y, cos, sin)
        q = (y * Q_SCALE).astype(BF16)
        ps, l = _softmax_parts(q, ks)
        o = jnp.dot(ps[0].astype(BF16), vs[0], preferred_element_type=F32)
        for p, v in zip(ps[1:], vs[1:]):
            o = o + jnp.dot(p.astype(BF16), v, preferred_element_type=F32)
        o_ref[:, h * HD:(h + 1) * HD] = (o / l).astype(o_ref.dtype)


def _gqa_attn(st, qkv, g_q, k, row0, nb, seq, tables, cache_k, cache_v, layer_j, prev):
    tq = min(1024, seq)
    nq = seq // tq
    grp_w = (ATTN_HEADS // ATTN_KV_HEADS) * HD
    v_col0 = ATTN_HEADS + ATTN_KV_HEADS
    assert row0 % seq == 0, "a segment must start on a multiple of its sequence length"
    b0 = row0 // seq
    q0 = row0 // tq
    in_specs = [
        pl.BlockSpec((tq, grp_w), lambda b, g, i: (q0 + b * nq + i, g)),
        pl.BlockSpec((1, HD), lambda b, g, i: (0, 0)),
        pl.BlockSpec((seq, HD), lambda b, g, i: (b + b0, g)),
        pl.BlockSpec((seq, HD), lambda b, g, i: (b + b0, v_col0 + g)),
    ]
    args = [qkv, g_q.reshape(1, HD), k, qkv]
    if tables is not None:
        in_specs += [pl.BlockSpec((tq, HD), lambda b, g, i: (i, 0))] * 2
        args += list(tables)
    cache = cache_k is not None
    if cache:
        past = cache_k.shape[2]
        ck = cache_k.reshape(cache_k.shape[0], cache_k.shape[1], past, ATTN_KV_HEADS * HD)
        cv = cache_v.reshape(ck.shape)
        in_specs += [pl.BlockSpec((None, None, past, HD), lambda b, g, i: (b, layer_j, 0, g))] * 2
        args += [ck, cv]
    aliases = {}
    if prev is not None:
        in_specs.append(pl.BlockSpec(memory_space=pl.ANY))
        args.append(prev)
        aliases = {len(args) - 1: 0}
    return pl.pallas_call(
        functools.partial(_gqa_attn_kernel, cache=cache, aliased=prev is not None,
                          rope=tables is not None),
        grid=(nb, ATTN_KV_HEADS, nq), in_specs=in_specs,
        out_specs=pl.BlockSpec((tq, grp_w), lambda b, g, i: (q0 + b * nq + i, g)),
        out_shape=jax.ShapeDtypeStruct((st.n, ATTN_HEADS * HD), BF16),
        input_output_aliases=aliases,
        compiler_params=_cparams(("parallel", "parallel", "parallel")), name="gqa_attn",
    )(*args)


def _diff_attn_kernel(lam_ref, q_ref, k_ref, v_ref, g_ref, *rest, cache, out_scale, aliased):
    if aliased:
        rest = rest[:-2] + rest[-1:]
    if cache:
        ck_ref, cv_ref, o_ref = rest
        vs = [cv_ref[...].astype(BF16), v_ref[...].astype(BF16)]
    else:
        (o_ref,) = rest
        vs = [v_ref[...].astype(BF16)]
    lam = lam_ref[0]
    outs = []
    for m in range(2):
        q = q_ref[:, m * HD:(m + 1) * HD]
        ks = [k_ref[:, m * HD:(m + 1) * HD]]
        if cache:
            ks = [ck_ref[:, m * HD:(m + 1) * HD].astype(BF16)] + ks
        ps, l = _softmax_parts(q, ks)
        o = jnp.dot(ps[0].astype(BF16), vs[0], preferred_element_type=F32)
        for p, v in zip(ps[1:], vs[1:]):
            o = o + jnp.dot(p.astype(BF16), v, preferred_element_type=F32)
        outs.append(o / l)
    o = outs[0] - lam * outs[1]
    o = o * lax.rsqrt(jnp.mean(o * o, axis=-1, keepdims=True) + EPS) * g_ref[...]
    o_ref[...] = (o * out_scale).astype(o_ref.dtype)


def _diff_attn(st, lam, q, k, v, g_sub, lam_init, row0, nb, seq, cache_k, cache_v, layer_j, prev):
    tq = min(1024, seq)
    nq = seq // tq
    hw = 2 * HD
    assert row0 % seq == 0, "a segment must start on a multiple of its sequence length"
    b0 = row0 // seq
    q0 = row0 // tq
    in_specs = [
        pl.BlockSpec(memory_space=pltpu.SMEM),
        pl.BlockSpec((tq, hw), lambda b, h, i: (q0 + b * nq + i, h)),
        pl.BlockSpec((seq, hw), lambda b, h, i: (b + b0, h)),
        pl.BlockSpec((seq, hw), lambda b, h, i: (b + b0, h)),
        pl.BlockSpec((1, hw), lambda b, h, i: (0, 0)),
    ]
    args = [lam, q, k, v, g_sub.reshape(1, hw)]
    cache = cache_k is not None
    if cache:
        past = cache_k.shape[2]
        ck = cache_k.reshape(cache_k.shape[0], cache_k.shape[1], past, 2 * DIFF_HEADS * HD)
        cv = cache_v.reshape(ck.shape)
        in_specs += [pl.BlockSpec((None, None, past, hw), lambda b, h, i: (b, layer_j, 0, h))] * 2
        args += [ck, cv]
    aliases = {}
    if prev is not None:
        in_specs.append(pl.BlockSpec(memory_space=pl.ANY))
        args.append(prev)
        aliases = {len(args) - 1: 0}
    return pl.pallas_call(
        functools.partial(_diff_attn_kernel, cache=cache, out_scale=1.0 - lam_init,
                          aliased=prev is not None),
        grid=(nb, DIFF_HEADS, nq), in_specs=in_specs,
        out_specs=pl.BlockSpec((tq, hw), lambda b, h, i: (q0 + b * nq + i, h)),
        out_shape=jax.ShapeDtypeStruct((st.n, DIFF_HEADS * hw), BF16),
        input_output_aliases=aliases,
        compiler_params=_cparams(("parallel", "parallel", "parallel")), name="diff_attn",
    )(*args)


def _conv_kernel(x_ref, w_ref, b_ref, *rest):
    o_ref, pad_ref = rest[-2:]
    seq = x_ref.shape[0]
    half = SSD_CONV // 2
    zeros = jnp.zeros((8, x_ref.shape[1]), F32)
    pad_ref[0:8, :] = zeros
    pad_ref[seq + 8:seq + 16, :] = zeros
    pad_ref[8:seq + 8, :] = x_ref[...]
    xp = pad_ref[...]
    rows = seq + 16
    acc = jnp.broadcast_to(b_ref[...], x_ref.shape)
    for k in range(SSD_CONV):
        shifted = xp if k == half else pltpu.roll(xp, (half - k) % rows, 0)
        acc = acc + w_ref[k:k + 1, :] * shifted[8:8 + seq, :]
    o_ref[...] = _silu(acc)


def _ssd_conv(st, xbc, conv_w, conv_b, row0, nb, seq, prev):
    tc = 512
    assert row0 % seq == 0, "a segment must start on a multiple of its sequence length"
    b0 = row0 // seq
    in_specs = [pl.BlockSpec((seq, tc), lambda b, c: (b + b0, c)),
                pl.BlockSpec((SSD_CONV, tc), lambda b, c: (0, c)),
                pl.BlockSpec((1, tc), lambda b, c: (0, c))]
    args = [xbc, conv_w, conv_b.reshape(1, SSD_CONV_DIM)]
    aliases = {}
    if prev is not None:
        in_specs.append(pl.BlockSpec(memory_space=pl.ANY))
        args.append(prev)
        aliases = {len(args) - 1: 0}
    return pl.pallas_call(
        _conv_kernel, grid=(nb, SSD_CONV_DIM // tc),
        in_specs=in_specs,
        out_specs=pl.BlockSpec((seq, tc), lambda b, c: (b + b0, c)),
        out_shape=jax.ShapeDtypeStruct((st.n, SSD_CONV_DIM), F32),
        scratch_shapes=[pltpu.VMEM((seq + 16, tc), F32)],
        input_output_aliases=aliases,
        compiler_params=_cparams(("parallel", "parallel")), name="ssd_conv",
    )(*args)


def _scan_direction(x_ref, b_ref, c_ref, dt_ref, bias_ref, alog_ref, h0_ref, y_ref, hf_ref,
                    state_ref, acol_ref, arow_ref, dtrow_ref, wrow_ref, dec_ref, *, rev, nc):
    c = pl.program_id(1)
    q = SSD_CHUNK
    lane0 = SSD_HEADS if rev else 0
    r_heads = SSD_HEADS // SSD_GROUPS
    gw = r_heads * SSD_HEADDIM

    ii = lax.broadcasted_iota(jnp.int32, (q, q), 0)
    jj = lax.broadcasted_iota(jnp.int32, (q, q), 1)
    causal = (jj >= ii) if rev else (jj <= ii)

    @pl.when(c == 0)
    def _():
        if h0_ref is not None:
            for blk in range(SSD_INNER // 128):
                state_ref[:, blk * 128:(blk + 1) * 128] = h0_ref[blk * 128:(blk + 1) * 128, :].T
        else:
            state_ref[...] = jnp.zeros(state_ref.shape, F32)

    xdt = dt_ref[...] + bias_ref[...]
    dt = jnp.maximum(xdt, 0.0) + jnp.log(1.0 + jnp.exp(-jnp.abs(xdt)))
    la = dt * (-jnp.exp(alog_ref[...]))
    tri = causal.astype(BF16)
    tri_t = ((ii >= jj) if rev else (ii <= jj)).astype(BF16)
    la3 = _split3(la)
    acol_ref[...] = (jnp.dot(tri, la3[0], preferred_element_type=F32)
                     + jnp.dot(tri, la3[1], preferred_element_type=F32)
                     + jnp.dot(tri, la3[2], preferred_element_type=F32))
    arow = _dot3(_split3(la.T), tri_t)
    end = 0 if rev else q - 1
    a_end = jnp.broadcast_to(arow[:, end:end + 1], arow.shape)
    dt_t = dt.T
    arow_ref[...] = arow
    dtrow_ref[...] = dt_t
    wrow_ref[...] = jnp.exp(a_end - arow) * dt_t
    dec_ref[...] = jnp.exp(a_end)

    lane_lo = lax.broadcasted_iota(jnp.int32, (1, 128), 1) < SSD_HEADDIM

    for g in range(SSD_GROUPS):
        bmat = b_ref[:, g * SSD_STATE:(g + 1) * SSD_STATE]
        cmat = c_ref[:, g * SSD_STATE:(g + 1) * SSD_STATE]
        cb = lax.dot_general(cmat.astype(BF16), bmat.astype(BF16), (((1,), (1,)), ((), ())),
                             preferred_element_type=F32)
        b_t = bmat.T
        for pair in range(r_heads // 2):
            col = g * gw + pair * 128
            x_pair = x_ref[:, col:col + 128].astype(BF16)
            st_old = state_ref[:, col:col + 128]
            rhs = jnp.concatenate([x_pair, st_old.astype(BF16)], axis=0)
            ys, upds = [], []
            for sub in range(2):
                hd = lane0 + g * r_heads + 2 * pair + sub
                a_i = jnp.broadcast_to(acol_ref[:, hd:hd + 1], (q, q))
                seg = a_i - arow_ref[hd:hd + 1, :]
                decay = jnp.exp(jnp.where(causal, seg, -jnp.inf))
                m_intra = (cb * decay * dtrow_ref[hd:hd + 1, :]).astype(BF16)
                c_scaled = (cmat * jnp.exp(a_i)).astype(BF16)
                bw_t = (b_t * wrow_ref[hd:hd + 1, :]).astype(BF16)
                ys.append(jnp.dot(jnp.concatenate([m_intra, c_scaled], axis=1), rhs,
                                  preferred_element_type=F32))
                upds.append(jnp.dot(bw_t, x_pair, preferred_element_type=F32))
            hd0 = lane0 + g * r_heads + 2 * pair
            y_ref[:, col:col + 128] = jnp.where(lane_lo, ys[0], ys[1]).astype(y_ref.dtype)
            dec_pair = jnp.where(lane_lo, dec_ref[hd0:hd0 + 1, :], dec_ref[hd0 + 1:hd0 + 2, :])
            state_ref[:, col:col + 128] = dec_pair * st_old + jnp.where(lane_lo, upds[0], upds[1])

    if hf_ref is not None:
        @pl.when(c == nc - 1)
        def _():
            for blk in range(SSD_INNER // 128):
                hf_ref[blk * 128:(blk + 1) * 128, :] = state_ref[:, blk * 128:(blk + 1) * 128].T


def _ssd_scan_kernel(*refs, nc, has_init, want_final, aliased):
    fwd_in, bwd_in, (bias_ref, alog_ref) = refs[0:4], refs[4:8], refs[8:10]
    pos = 10
    h0 = (None, None)
    if has_init:
        h0 = refs[pos:pos + 2]; pos += 2
    if aliased:
        pos += 2
    ys = refs[pos:pos + 2]; pos += 2
    hf = (None, None)
    if want_final:
        hf = refs[pos:pos + 2]; pos += 2
    scratch = refs[pos:]
    per = len(scratch) // 2
    for d, ins in enumerate((fwd_in, bwd_in)):
        _scan_direction(*ins, bias_ref, alog_ref, h0[d], ys[d], hf[d],
                        *scratch[d * per:(d + 1) * per], rev=d == 1, nc=nc)


def _ssd_scan(st, xbc, dt, dt_bias, a_log, row0, nb, seq, h0s, layer_j, want_final, prevs):
    nc = seq // SSD_CHUNK
    c0 = row0 // SSD_CHUNK

    def chunk(rev):
        return lambda b, c: c0 + b * nc + (nc - 1 - c if rev else c)

    in_specs, args = [], []
    for rev in (False, True):
        ch = chunk(rev)
        in_specs += [
            pl.BlockSpec((SSD_CHUNK, SSD_INNER), lambda b, c, ch=ch: (ch(b, c), 0)),
            pl.BlockSpec((SSD_CHUNK, SSD_GN), lambda b, c, ch=ch: (ch(b, c), SSD_INNER // SSD_GN)),
            pl.BlockSpec((SSD_CHUNK, SSD_GN), lambda b, c, ch=ch: (ch(b, c), SSD_INNER // SSD_GN + 1)),
            pl.BlockSpec((SSD_CHUNK, 128), lambda b, c, ch=ch: (ch(b, c), 0)),
        ]
        args += [xbc, xbc, xbc, dt]
    in_specs += [pl.BlockSpec((1, 128), lambda b, c: (0, 0))] * 2
    args += [dt_bias.reshape(1, 128), a_log.reshape(1, 128)]
    if h0s is not None:
        for h0 in h0s:
            in_specs.append(pl.BlockSpec((None, None, SSD_INNER, SSD_STATE),
                                         lambda b, c: (b, layer_j, 0, 0)))
            args.append(h0.reshape(h0.shape[0], h0.shape[1], SSD_INNER, SSD_STATE))
    aliases = {}
    if prevs is not None:
        for d, prev in enumerate(prevs):
            in_specs.append(pl.BlockSpec(memory_space=pl.ANY))
            args.append(prev)
            aliases[len(args) - 1] = d
    out_shape = [jax.ShapeDtypeStruct((st.n, SSD_INNER), BF16)] * 2
    out_specs = [pl.BlockSpec((SSD_CHUNK, SSD_INNER), lambda b, c, ch=chunk(rev): (ch(b, c), 0))
                 for rev in (False, True)]
    if want_final:
        out_shape += [jax.ShapeDtypeStruct((nb, SSD_INNER, SSD_STATE), F32)] * 2
        out_specs += [pl.BlockSpec((None, SSD_INNER, SSD_STATE), lambda b, c: (b, 0, 0))] * 2
    per_dir = [pltpu.VMEM((SSD_STATE, SSD_INNER), F32),
               pltpu.VMEM((SSD_CHUNK, 128), F32)] + [pltpu.VMEM((128, SSD_CHUNK), F32)] * 4
    return pl.pallas_call(
        functools.partial(_ssd_scan_kernel, nc=nc, has_init=h0s is not None,
                          want_final=want_final, aliased=prevs is not None),
        grid=(nb, nc), in_specs=in_specs, out_specs=out_specs, out_shape=out_shape,
        scratch_shapes=per_dir * 2,
        input_output_aliases=aliases,
        compiler_params=_cparams(("parallel", "arbitrary")), name="ssd_scan",
    )(*args)


def _ssd_gate_kernel(yf_ref, yb_ref, x_ref, z_ref, d_ref, g_ref, o_ref):
    gw = SSD_INNER // SSD_GROUPS
    for grp in range(SSD_GROUPS):
        sl = slice(grp * gw, (grp + 1) * gw)
        y = (yf_ref[:, sl].astype(F32) + yb_ref[:, sl].astype(F32)) + d_ref[:, sl] * x_ref[:, sl]
        gated = y * _silu(z_ref[:, sl].astype(F32))
        normed = gated * lax.rsqrt(jnp.mean(gated * gated, axis=-1, keepdims=True) + EPS)
        o_ref[:, sl] = (normed * g_ref[:, sl]).astype(o_ref.dtype)


def _ssd_gate(yf, yb, xbc, z, d_sum, g_norm):
    n = yf.shape[0]
    tm = 256
    row = pl.BlockSpec((tm, SSD_INNER), lambda i: (i, 0))
    vec = pl.BlockSpec((1, SSD_INNER), lambda i: (0, 0))
    return pl.pallas_call(
        _ssd_gate_kernel, grid=(n // tm,),
        in_specs=[row, row, row, row, vec, vec], out_specs=row,
        out_shape=jax.ShapeDtypeStruct((n, SSD_INNER), BF16),
        compiler_params=_cparams(("parallel",)), name="ssd_gate",
    )(yf, yb, xbc, z, d_sum, g_norm.reshape(1, SSD_INNER))


def _moe_up_kernel(be_ref, nu_ref, a_ref, wg_ref, wu_ref, *rest, blk0):
    o_ref = rest[-1]
    m = pl.program_id(1) + blk0

    @pl.when(m < nu_ref[0])
    def _():
        a = a_ref[...]
        gate = jnp.dot(a, wg_ref[...].astype(BF16), preferred_element_type=F32)
        up = jnp.dot(a, wu_ref[...].astype(BF16), preferred_element_type=F32)
        o_ref[...] = (_silu(gate) * up).astype(o_ref.dtype)

    @pl.when(m >= nu_ref[0])
    def _():
        o_ref[...] = jnp.zeros(o_ref.shape, o_ref.dtype)


def _moe_down_kernel(be_ref, nu_ref, a_ref, w_ref, o_ref):
    m = pl.program_id(0)

    @pl.when(m < nu_ref[0])
    def _():
        o_ref[...] = jnp.dot(a_ref[...], w_ref[...].astype(BF16),
                             preferred_element_type=F32).astype(o_ref.dtype)

    @pl.when(m >= nu_ref[0])
    def _():
        o_ref[...] = jnp.zeros(o_ref.shape, o_ref.dtype)


def _moe(st, h, logits, w_gu, w_down, f):
    n = st.n
    tb = MOE_BLOCK
    n_assign = n * TOP_K
    n_blocks = n_assign // tb + N_EXPERTS
    slots = n_blocks * tb
    top_logit, top_e = lax.top_k(logits, TOP_K)
    gates = jax.nn.softmax(top_logit, axis=-1)
    flat_e = top_e.reshape(-1).astype(jnp.int32)
    onehot = (flat_e[:, None] == jnp.arange(N_EXPERTS, dtype=jnp.int32)[None, :]).astype(jnp.int32)
    rank = jnp.sum((jnp.cumsum(onehot, axis=0) - onehot) * onehot, axis=1)
    counts = jnp.sum(onehot, axis=0)
    padded = (counts + tb - 1) // tb * tb
    pad_end = jnp.cumsum(padded)
    pad_start = pad_end - padded
    dest = pad_start[flat_e] + rank
    block_start = jnp.arange(n_blocks, dtype=jnp.int32) * tb
    block_e = jnp.minimum(jnp.searchsorted(pad_end, block_start, side="right"),
                          N_EXPERTS - 1).astype(jnp.int32)
    n_used = (pad_end[-1] // tb).astype(jnp.int32).reshape(1)
    order = jnp.argsort(flat_e, stable=True).astype(jnp.int32)
    slot_e = jnp.repeat(block_e, tb)
    slot_r = jnp.arange(slots, dtype=jnp.int32) - pad_start[slot_e]
    filled = slot_r < counts[slot_e]
    src = jnp.where(filled, (jnp.cumsum(counts) - counts)[slot_e] + slot_r, 0)
    slot_tok = jnp.where(filled, order[src] // TOP_K, 0)

    tn = 512
    nf = EXPERT_DIM // tn

    def live_block(m, nu):
        return jnp.minimum(m, nu[0] - 1)

    head = min(MOE_HEAD_BLOCKS, n_blocks // 2)
    act = None
    for blk0, nblk in ((0, head), (head, n_blocks - head)):
        xs = h.at[slot_tok[blk0 * tb:(blk0 + nblk) * tb]].get(mode="promise_in_bounds")

        def local(m, nu, blk0=blk0, nblk=nblk):
            return jnp.clip(live_block(m + blk0, nu) - blk0, 0, nblk - 1)

        def expert(m, be, nu, blk0=blk0):
            return be[live_block(m + blk0, nu)]

        in_specs = [
            pl.BlockSpec((tb, D_MODEL), lambda j, m, be, nu, local=local: (local(m, nu), 0)),
            pl.BlockSpec((None, None, D_MODEL, tn),
                         lambda j, m, be, nu, expert=expert: (f, expert(m, be, nu), 0, j)),
            pl.BlockSpec((None, None, D_MODEL, tn),
                         lambda j, m, be, nu, expert=expert: (f, expert(m, be, nu), 0, j + nf)),
        ]
        args = [block_e, n_used, xs, w_gu, w_gu]
        aliases = {}
        if act is not None:
            in_specs.append(pl.BlockSpec(memory_space=pl.ANY))
            args.append(act)
            aliases = {len(args) - 1: 0}
        act = pl.pallas_call(
            functools.partial(_moe_up_kernel, blk0=blk0),
            grid_spec=pltpu.PrefetchScalarGridSpec(
                num_scalar_prefetch=2, grid=(nf, nblk), in_specs=in_specs,
                out_specs=pl.BlockSpec((tb, tn), lambda j, m, be, nu, blk0=blk0: (m + blk0, j)),
            ),
            out_shape=jax.ShapeDtypeStruct((slots, EXPERT_DIM), BF16),
            input_output_aliases=aliases,
            compiler_params=_cparams(("parallel", "arbitrary")), name="moe_up",
        )(*args)

    tn2 = 256
    yb = pl.pallas_call(
        _moe_down_kernel,
        grid_spec=pltpu.PrefetchScalarGridSpec(
            num_scalar_prefetch=2, grid=(n_blocks, D_MODEL // tn2),
            in_specs=[
                pl.BlockSpec((tb, EXPERT_DIM), lambda m, j, be, nu: (live_block(m, nu), 0)),
                pl.BlockSpec((None, None, EXPERT_DIM, tn2),
                             lambda m, j, be, nu: (f, be[live_block(m, nu)], 0, j)),
            ],
            out_specs=pl.BlockSpec((tb, tn2), lambda m, j, be, nu: (m, j)),
        ),
        out_shape=jax.ShapeDtypeStruct((slots, D_MODEL), BF16),
        compiler_params=_cparams(("arbitrary", "arbitrary")), name="moe_down",
    )(block_e, n_used, act, w_down)

    dest2 = dest.reshape(n, TOP_K)
    y0 = yb.at[dest2[:, 0]].get(mode="promise_in_bounds")
    y1 = yb.at[dest2[:, 1]].get(mode="promise_in_bounds")
    return y0, y1, gates


def _combine_kernel(x_ref, y0_ref, y1_ref, g_ref, gate_ref, *rest, norm):
    y = g_ref[:, 0:1] * y0_ref[...].astype(F32) + g_ref[:, 1:2] * y1_ref[...].astype(F32)
    xn = x_ref[...] + gate_ref[...] * y
    if norm:
        gn_ref, shift_ref, scale_ref, o_ref, h_ref = rest
        _norm_mod_store(xn, gn_ref, shift_ref, scale_ref, h_ref)
    else:
        (o_ref,) = rest
    o_ref[...] = xn


def _combine(st, x, y0, y1, gates, mods, layer, g_next=None):
    tm = st.tile(512)
    row = pl.BlockSpec((tm, D_MODEL), lambda i: (i, 0))

    def mod(lyr, which):
        return pl.BlockSpec((None, 1, D_MODEL),
                            lambda i: (_mod_index(lyr, which, st.mod_row(i, tm)), 0, 0))

    in_specs = [row, row, row, pl.BlockSpec((tm, TOP_K), lambda i: (i, 0)), mod(layer, 5)]
    args = [x, y0, y1, gates, mods]
    out_shape = [jax.ShapeDtypeStruct((st.n, D_MODEL), F32)]
    out_specs = [row]
    if g_next is not None:
        in_specs += [pl.BlockSpec((1, D_MODEL), lambda i: (0, 0)), mod(layer + 1, 0), mod(layer + 1, 1)]
        args += [g_next.reshape(1, D_MODEL), mods, mods]
        out_shape.append(jax.ShapeDtypeStruct((st.n, D_MODEL), BF16))
        out_specs.append(row)
    res = pl.pallas_call(
        functools.partial(_combine_kernel, norm=g_next is not None), grid=(st.n // tm,),
        in_specs=in_specs, out_specs=out_specs, out_shape=out_shape,
        compiler_params=_cparams(("parallel",)), name="moe_combine",
    )(*args)
    return res if g_next is not None else (res[0], None)


def kernel(x_prompt, x_sample, cache_attn_k, cache_attn_v, cache_diff_k, cache_diff_v, state_ssd_fwd, state_ssd_bwd, c, c_ctx, w_ada, b_ada, g_mix, g_ffn, g_final, attn_w_in, attn_g_q, attn_g_k, attn_w_out, diff_w_in, diff_lambda, diff_g_sub, diff_w_out, ssd_w_in, ssd_conv_w, ssd_conv_b, ssd_dt_bias, ssd_a_log, ssd_d, ssd_g_norm, ssd_w_out, ffn_w_gu, ffn_w_down, moe_w_router, moe_w_gu, moe_w_down):
    pb, pl_len, _ = x_prompt.shape
    sb, sl_len, _ = x_sample.shape
    st = _Stream(pb, pl_len, sb, sl_len)
    np_, ns = st.n_prompt, st.n_sample
    tm = st.tile(1024)

    x = jnp.concatenate([x_prompt.reshape(np_, D_MODEL), x_sample.reshape(ns, D_MODEL)], axis=0)

    cond = jnp.zeros((MOD_ROWS, D_MODEL), F32).at[:sb].set(c).at[CTX_ROW].set(c_ctx)
    cond_act = jax.nn.silu(cond).astype(BF16)
    mods = []
    for i in range(DEPTH):
        m = _matmul(cond_act, w_ada, (i,), n_off=0, n_out=6 * D_MODEL, tm=MOD_ROWS, tn=1024,
                    out_dtype=F32, name="ada")
        mods.append(m + b_ada[i][None, :])
    mods = jnp.stack(mods).reshape(DEPTH * MOD_ROWS * 6, 1, D_MODEL)

    tables = _rope_tables(sl_len, tm)
    prep_tables = _rope_tables(sl_len, GQA_PREP_ROWS)

    def table_index(i):
        return _table_block(st, i, tm)

    tm2 = st.tile(2048)

    def gate_index(layer, which, tile):
        return lambda i: _mod_index(layer, which, st.mod_row(i, tile))

    new_attn_k, new_attn_v, new_diff_k, new_diff_v, new_ssd_f, new_ssd_b = [], [], [], [], [], []
    def out_proj(o, w16, x, layer, fuse_norm):
        if fuse_norm:
            return _out_proj_norm(st, o, w16, x, mods, layer, g_ffn[layer])
        xo = _matmul(o, w16, (), n_off=0, n_out=D_MODEL, tm=tm2, tn=512,
                     out_dtype=F32, epilogue="resid", resid=x, mods=mods,
                     gate_index=gate_index(layer, 2, tm2), name="attn_out")
        return xo, None

    h_pre = None
    for i in range(DEPTH):
        kind, j = i % 3, i // 3
        f = i // 2
        w_router = moe_w_router[f] if i % 2 == 1 else None
        h = h_pre if h_pre is not None else _norm_mod(st, x, g_mix[i], mods, i, 0)
        h_pre = hf = logits = None
        if kind == 0:
            qkv = _matmul(h, attn_w_in, (j,), n_off=0, n_out=attn_w_in.shape[-1], tm=tm2, tn=512,
                          out_dtype=F32, name="attn_in")
            k = _gqa_kprep(st, qkv, attn_g_k[j], prep_tables)
            o = _gqa_attn(st, qkv, attn_g_q[j], k, 0, pb, pl_len, None, None, None, j, None)
            o = _gqa_attn(st, qkv, attn_g_q[j], k, np_, sb, sl_len, tables,
                          cache_attn_k, cache_attn_v, j, o)
            new_attn_k.append(k[:np_].reshape(pb, pl_len, ATTN_KV_HEADS, HD))
            new_attn_v.append(qkv[:np_, (ATTN_HEADS + ATTN_KV_HEADS) * HD:]
                              .reshape(pb, pl_len, ATTN_KV_HEADS, HD))
            x, hf = out_proj(o, attn_w_out[j].astype(BF16), x, i, w_router is None)
        elif kind == 1:
            lam_init = 0.8 - 0.6 * math.exp(-0.3 * i)
            lf = diff_lambda[j]
            lam = (jnp.exp(jnp.sum(lf[0] * lf[1])) - jnp.exp(jnp.sum(lf[2] * lf[3])) + lam_init).reshape(1)
            w = 2 * DIFF_HEADS * HD
            q = _matmul(h, diff_w_in, (j,), n_off=0, n_out=w, tm=tm, tn=1024, out_dtype=BF16,
                        epilogue="rope", scale=Q_SCALE,
                        tables=tables, table_index=table_index, name="diff_in_q")
            k = _matmul(h, diff_w_in, (j,), n_off=w, n_out=w, tm=tm, tn=1024, out_dtype=BF16,
                        epilogue="rope", tables=tables, table_index=table_index, name="diff_in_k")
            v = _matmul(h, diff_w_in, (j,), n_off=2 * w, n_out=w, tm=tm2, tn=512, out_dtype=F32,
                        name="diff_in_v")
            k_ctx = _matmul(h, diff_w_in, (j,), n_off=w, n_out=w, tm=tm, tn=1024, rows=np_,
                            out_dtype=F32, name="diff_in_kctx")
            o = _diff_attn(st, lam, q, k, v, diff_g_sub[j], lam_init, 0, pb, pl_len, None, None, j, None)
            o = _diff_attn(st, lam, q, k, v, diff_g_sub[j], lam_init, np_, sb, sl_len,
                           cache_diff_k, cache_diff_v, j, o)
            new_diff_k.append(k_ctx.reshape(pb, pl_len, 2 * DIFF_HEADS, HD))
            new_diff_v.append(v[:np_].reshape(pb, pl_len, DIFF_HEADS, 2 * HD))
            x, hf = out_proj(o, diff_w_out[j].astype(BF16), x, i, w_router is None)
        else:
            z = _matmul(h, ssd_w_in, (j,), n_off=0, n_out=SSD_INNER, tm=tm2, tn=512,
                        out_dtype=BF16, name="ssd_in_z")
            xbc = _matmul(h, ssd_w_in, (j,), n_off=SSD_INNER, n_out=SSD_CONV_DIM, tm=tm2, tn=512,
                          out_dtype=F32, name="ssd_in_xbc")
            dt = _matmul(h, ssd_w_in, (j,), n_off=SSD_INNER + SSD_CONV_DIM, n_out=2 * SSD_HEADS,
                         tm=tm, tn=128, out_dtype=F32, name="ssd_in_dt")
            xc = _ssd_conv(st, xbc, ssd_conv_w[j], ssd_conv_b[j], 0, pb, pl_len, None)
            xc = _ssd_conv(st, xbc, ssd_conv_w[j], ssd_conv_b[j], np_, sb, sl_len, xc)
            yf, yb, hfin_f, hfin_b = _ssd_scan(st, xc, dt, ssd_dt_bias[j], ssd_a_log[j], 0, pb, pl_len,
                                               None, j, True, None)
            ys = _ssd_scan(st, xc, dt, ssd_dt_bias[j], ssd_a_log[j], np_, sb, sl_len,
                           (state_ssd_fwd, state_ssd_bwd), j, False, (yf, yb))
            new_ssd_f.append(hfin_f.reshape(pb, SSD_HEADS, SSD_HEADDIM, SSD_STATE))
            new_ssd_b.append(hfin_b.reshape(pb, SSD_HEADS, SSD_HEADDIM, SSD_STATE))
            d_sum = jnp.repeat(ssd_d[j][0] + ssd_d[j][1], SSD_HEADDIM).reshape(1, SSD_INNER)
            o = _ssd_gate(ys[0], ys[1], xc, z, d_sum, ssd_g_norm[j])
            x = _matmul(o, ssd_w_out[j].astype(BF16), (), n_off=0, n_out=D_MODEL, tm=tm, tn=512,
                        out_dtype=F32, epilogue="resid", resid=x, mods=mods,
                        gate_index=gate_index(i, 2, tm), name="ssd_out")

        if hf is None:
            if w_router is not None:
                hf, logits = _norm_mod(st, x, g_ffn[i], mods, i, 3, w_router=w_router)
            else:
                hf = _norm_mod(st, x, g_ffn[i], mods, i, 3)
        if i % 2 == 0:
            act = _matmul(hf, ffn_w_gu, (f,), n_off=0, n_out=FFN_DIM, tm=tm, tn=512,
                          out_dtype=BF16, epilogue="swiglu", up_off=FFN_DIM, name="ffn_up")
            x = _matmul(act, ffn_w_down[f].astype(BF16), (), n_off=0, n_out=D_MODEL, tm=tm, tn=512,
                        out_dtype=F32, epilogue="resid", resid=x, mods=mods,
                        gate_index=gate_index(i, 5, tm), name="ffn_down")
        else:
            y0, y1, gates = _moe(st, hf, logits[:, :N_EXPERTS], moe_w_gu, moe_w_down, f)
            x, h_pre = _combine(st, x, y0, y1, gates, mods, i,
                                g_next=g_mix[i + 1] if i + 1 < DEPTH else None)

    y_prompt = _final_norm(x, g_final, 0, np_).reshape(pb, pl_len, D_MODEL)
    y_sample = _final_norm(x, g_final, np_, ns).reshape(sb, sl_len, D_MODEL)
    return (y_prompt, y_sample,
            jnp.stack(new_attn_k, axis=1), jnp.stack(new_attn_v, axis=1),
            jnp.stack(new_diff_k, axis=1), jnp.stack(new_diff_v, axis=1),
            jnp.stack(new_ssd_f, axis=1), jnp.stack(new_ssd_b, axis=1))
```

```python
import functools
import math

import jax
import jax.numpy as jnp
from jax import lax
from jax.experimental import pallas as pl
from jax.experimental.pallas import tpu as pltpu

F32 = jnp.float32
BF16 = jnp.bfloat16

D_MODEL = 2048
DEPTH = 4
GRID_W = 64
ROPE_THETA = 10000.0
EPS = 1e-6
HD = 128
Q_SCALE = math.log2(math.e) / math.sqrt(HD)
ATTN_HEADS = 16
ATTN_KV_HEADS = 4
DIFF_HEADS = 8
SSD_INNER = 2 * D_MODEL
SSD_HEADDIM = 64
SSD_HEADS = SSD_INNER // SSD_HEADDIM
SSD_GROUPS = 8
SSD_STATE = 128
SSD_CONV = 5
SSD_CHUNK = 128
SSD_GN = SSD_GROUPS * SSD_STATE
SSD_CONV_DIM = SSD_INNER + 2 * SSD_GN
FFN_DIM = 5632
N_EXPERTS = 8
TOP_K = 2
EXPERT_DIM = 7168

MOD_ROWS = 16
CTX_ROW = 8
MOE_BLOCK = 1024
MOE_HEAD_BLOCKS = 8
VMEM_LIMIT = 56 * 1024 * 1024


def _cparams(sem):
    return pltpu.CompilerParams(dimension_semantics=sem, vmem_limit_bytes=VMEM_LIMIT)


def _silu(x):
    return x / (1.0 + jnp.exp(-x))


def _split3(x):
    hi = x.astype(BF16)
    r = x - hi.astype(F32)
    mid = r.astype(BF16)
    lo = (r - mid.astype(F32)).astype(BF16)
    return hi, mid, lo


def _dot3(a3, b):
    out = jnp.dot(a3[0], b, preferred_element_type=F32)
    out = out + jnp.dot(a3[1], b, preferred_element_type=F32)
    return out + jnp.dot(a3[2], b, preferred_element_type=F32)


class _Stream:
    def __init__(self, n_prompt_batch, prompt_len, n_sample_batch, sample_len):
        self.pb, self.pl_, self.sb, self.sl = n_prompt_batch, prompt_len, n_sample_batch, sample_len
        self.n_prompt = n_prompt_batch * prompt_len
        self.n_sample = n_sample_batch * sample_len
        self.n = self.n_prompt + self.n_sample

    def mod_row(self, i, tm):
        pt = self.n_prompt // tm
        per = self.sl // tm
        return jnp.where(i < pt, CTX_ROW, (i - pt) // per)

    def tile(self, want):
        t = want
        while self.n_prompt % t or self.sl % t:
            t //= 2
        return t


def _mod_index(layer, which, row):
    return (layer * MOD_ROWS + row) * 6 + which


def _norm_mod_store(x, g_ref, shift_ref, scale_ref, o_ref, wr_ref=None, lg_ref=None):
    y = x * lax.rsqrt(jnp.mean(x * x, axis=-1, keepdims=True) + EPS) * g_ref[...]
    h = y * (1.0 + scale_ref[...]) + shift_ref[...]
    if wr_ref is not None:
        w = wr_ref[...]
        whi = w.astype(BF16)
        wlo = (w - whi.astype(F32)).astype(BF16)
        hhi = h.astype(BF16)
        hlo = (h - hhi.astype(F32)).astype(BF16)
        lg = jnp.dot(hhi, whi, preferred_element_type=F32)
        lg = lg + jnp.dot(hlo, whi, preferred_element_type=F32)
        lg_ref[...] = lg + jnp.dot(hhi, wlo, preferred_element_type=F32)
    o_ref[...] = h.astype(o_ref.dtype)


def _norm_mod_kernel(x_ref, g_ref, shift_ref, scale_ref, *rest, router):
    if router:
        wr_ref, o_ref, lg_ref = rest
        _norm_mod_store(x_ref[...], g_ref, shift_ref, scale_ref, o_ref, wr_ref, lg_ref)
    else:
        (o_ref,) = rest
        _norm_mod_store(x_ref[...], g_ref, shift_ref, scale_ref, o_ref)


def _router_pad(w_router):
    return jnp.zeros((D_MODEL, 128), F32).at[:, :N_EXPERTS].set(w_router)


def _out_norm_kernel(a_ref, w_ref, x_ref, gate_ref, g_ref, shift_ref, scale_ref, xo_ref, ho_ref):
    xn = x_ref[...] + gate_ref[...] * jnp.dot(a_ref[...], w_ref[...], preferred_element_type=F32)
    _norm_mod_store(xn, g_ref, shift_ref, scale_ref, ho_ref)
    xo_ref[...] = xn


def _out_proj_norm(st, a, w, x, mods, layer, g_next):
    tm = st.tile(512)
    kdim = a.shape[1]

    def mod(which):
        return pl.BlockSpec((None, 1, D_MODEL),
                            lambda i: (_mod_index(layer, which, st.mod_row(i, tm)), 0, 0))

    row = pl.BlockSpec((tm, D_MODEL), lambda i: (i, 0))
    in_specs = [pl.BlockSpec((tm, kdim), lambda i: (i, 0)),
                pl.BlockSpec((kdim, D_MODEL), lambda i: (0, 0), pipeline_mode=pl.Buffered(1)),
                row, mod(2), pl.BlockSpec((1, D_MODEL), lambda i: (0, 0)), mod(3), mod(4)]
    args = [a, w, x, mods, g_next.reshape(1, D_MODEL), mods, mods]
    out_shape = [jax.ShapeDtypeStruct((st.n, D_MODEL), F32), jax.ShapeDtypeStruct((st.n, D_MODEL), BF16)]
    return pl.pallas_call(
        _out_norm_kernel,
        grid=(st.n // tm,), in_specs=in_specs, out_specs=[row, row], out_shape=out_shape,
        compiler_params=_cparams(("parallel",)), name="out_proj_norm",
    )(*args)


def _norm_mod(st, x, g, mods, layer, which_shift, w_router=None):
    tm = st.tile(512)
    nt = st.n // tm
    g2 = g.reshape(1, D_MODEL)
    in_specs = [
        pl.BlockSpec((tm, D_MODEL), lambda i: (i, 0)),
        pl.BlockSpec((1, D_MODEL), lambda i: (0, 0)),
        pl.BlockSpec((None, 1, D_MODEL),
                     lambda i: (_mod_index(layer, which_shift, st.mod_row(i, tm)), 0, 0)),
        pl.BlockSpec((None, 1, D_MODEL),
                     lambda i: (_mod_index(layer, which_shift + 1, st.mod_row(i, tm)), 0, 0)),
    ]
    args = [x, g2, mods, mods]
    out_shape = [jax.ShapeDtypeStruct((st.n, D_MODEL), BF16)]
    out_specs = [pl.BlockSpec((tm, D_MODEL), lambda i: (i, 0))]
    if w_router is not None:
        in_specs.append(pl.BlockSpec((D_MODEL, 128), lambda i: (0, 0)))
        args.append(_router_pad(w_router))
        out_shape.append(jax.ShapeDtypeStruct((st.n, 128), F32))
        out_specs.append(pl.BlockSpec((tm, 128), lambda i: (i, 0)))
    res = pl.pallas_call(
        functools.partial(_norm_mod_kernel, router=w_router is not None),
        grid=(nt,), in_specs=in_specs, out_specs=out_specs, out_shape=out_shape,
        compiler_params=_cparams(("parallel",)), name="norm_mod",
    )(*args)
    return res if w_router is not None else res[0]


def _final_norm_kernel(x_ref, g_ref, o_ref):
    x = x_ref[...]
    o_ref[...] = x * lax.rsqrt(jnp.mean(x * x, axis=-1, keepdims=True) + EPS) * g_ref[...]


def _final_norm(x, g, row0, nrows):
    tm = 512
    r0 = row0 // tm
    return pl.pallas_call(
        _final_norm_kernel, grid=(nrows // tm,),
        in_specs=[pl.BlockSpec((tm, D_MODEL), lambda i: (i + r0, 0)),
                  pl.BlockSpec((1, D_MODEL), lambda i: (0, 0))],
        out_specs=pl.BlockSpec((tm, D_MODEL), lambda i: (i, 0)),
        out_shape=jax.ShapeDtypeStruct((nrows, D_MODEL), F32),
        compiler_params=_cparams(("parallel",)), name="final_norm",
    )(x, g.reshape(1, D_MODEL))


def _mm_kernel(*refs, epilogue, scale):
    a_ref, w_ref = refs[0], refs[1]
    o_ref = refs[-1]
    a = a_ref[...]
    y = jnp.dot(a, w_ref[...].astype(BF16), preferred_element_type=F32)
    if epilogue == "swiglu":
        up = jnp.dot(a, refs[2][...].astype(BF16), preferred_element_type=F32)
        o_ref[...] = (_silu(y) * up).astype(o_ref.dtype)
    elif epilogue == "resid":
        x_ref, gate_ref = refs[2], refs[3]
        o_ref[...] = (x_ref[...] + gate_ref[...] * y).astype(o_ref.dtype)
    elif epilogue == "rope":
        cos, sin = refs[2][...], refs[3][...]
        for h in range(y.shape[1] // HD):
            yh = _rope(y[:, h * HD:(h + 1) * HD], cos, sin)
            o_ref[:, h * HD:(h + 1) * HD] = (yh * scale).astype(o_ref.dtype)
    else:
        o_ref[...] = y.astype(o_ref.dtype)


def _matmul(a, w, widx, *, n_off, n_out, tm, tn, out_dtype, epilogue="plain", up_off=None,
            resid=None, mods=None, gate_index=None, scale=1.0, tables=None, table_index=None,
            rows=None, name="mm"):
    m, kdim = (a.shape[0] if rows is None else rows), a.shape[1]
    assert m % tm == 0 and n_out % tn == 0 and n_off % tn == 0
    nb = n_off // tn
    lead = (None,) * len(widx)
    in_specs = [
        pl.BlockSpec((tm, kdim), lambda i, j: (i, 0)),
        pl.BlockSpec(lead + (kdim, tn), lambda i, j: widx + (0, j + nb)),
    ]
    args = [a, w]
    if epilogue == "swiglu":
        ub = up_off // tn
        in_specs.append(pl.BlockSpec(lead + (kdim, tn), lambda i, j: widx + (0, j + ub)))
        args.append(w)
    if epilogue == "resid":
        in_specs.append(pl.BlockSpec((tm, tn), lambda i, j: (i, j)))
        in_specs.append(pl.BlockSpec((None, 1, tn), lambda i, j: (gate_index(i), 0, j)))
        args += [resid, mods]
    if epilogue == "rope":
        in_specs += [pl.BlockSpec((tm, HD), lambda i, j: (table_index(i), 0))] * 2
        args += list(tables)
    return pl.pallas_call(
        functools.partial(_mm_kernel, epilogue=epilogue, scale=scale),
        grid=(m // tm, n_out // tn),
        in_specs=in_specs,
        out_specs=pl.BlockSpec((tm, tn), lambda i, j: (i, j)),
        out_shape=jax.ShapeDtypeStruct((m, n_out), out_dtype),
        compiler_params=_cparams(("parallel", "parallel")), name=name,
    )(*args)


def _rope_tables(length, ident_rows):
    nf = HD // 4
    inv = ROPE_THETA ** (-jnp.arange(nf, dtype=F32) / nf)
    rows = length // GRID_W
    row = jnp.repeat(jnp.arange(rows, dtype=F32), GRID_W)
    col = jnp.tile(jnp.arange(GRID_W, dtype=F32), rows)
    ar = row[:, None] * inv
    ac = col[:, None] * inv
    ang = jnp.concatenate([ar, ar, ac, ac], axis=-1)
    cos, sin = jnp.cos(ang), jnp.sin(ang)
    lane = jnp.arange(HD) % (HD // 2)
    sin_signed = jnp.where(lane >= HD // 4, sin, -sin)
    ones = jnp.ones((ident_rows, HD), F32)
    zeros = jnp.zeros((ident_rows, HD), F32)
    return jnp.concatenate([cos, ones]), jnp.concatenate([sin_signed, zeros])


def _table_block(st, i, tm):
    pt = st.n_prompt // tm
    per = st.sl // tm
    return jnp.where(i < pt, per, (i - pt) % per)


def _rope(y, cos, sin_signed):
    partner = lax.broadcasted_iota(jnp.int32, y.shape, 1) ^ (HD // 4)
    return y * cos + jnp.take_along_axis(y, partner, axis=1) * sin_signed


GQA_PREP_ROWS = 256


def _head_norm(x, g):
    return x * lax.rsqrt(jnp.mean(x * x, axis=-1, keepdims=True) + EPS) * g


def _gqa_kprep_kernel(k_ref, gk_ref, cos_ref, sin_ref, ko_ref):
    cos, sin = cos_ref[...], sin_ref[...]
    for h in range(ATTN_KV_HEADS):
        ko_ref[:, h * HD:(h + 1) * HD] = _rope(_head_norm(k_ref[:, h * HD:(h + 1) * HD], gk_ref[...]),
                                               cos, sin)


def _gqa_kprep(st, qkv, g_k, tables):
    tm = GQA_PREP_ROWS
    qw = ATTN_HEADS * HD
    kw = ATTN_KV_HEADS * HD
    tab = pl.BlockSpec((tm, HD), lambda i: (_table_block(st, i, tm), 0))
    return pl.pallas_call(
        _gqa_kprep_kernel, grid=(st.n // tm,),
        in_specs=[pl.BlockSpec((tm, kw), lambda i: (i, qw // kw)),
                  pl.BlockSpec((1, HD), lambda i: (0, 0)), tab, tab],
        out_specs=pl.BlockSpec((tm, kw), lambda i: (i, 0)),
        out_shape=jax.ShapeDtypeStruct((st.n, kw), F32),
        compiler_params=_cparams(("parallel",)), name="gqa_kprep",
    )(qkv, g_k.reshape(1, HD), *tables)


def _softmax_parts(q, ks):
    ss = [lax.dot_general(q, k, (((1,), (1,)), ((), ())), preferred_element_type=F32) for k in ks]
    m = ss[0].max(axis=-1, keepdims=True)
    for s in ss[1:]:
        m = jnp.maximum(m, s.max(axis=-1, keepdims=True))
    ps = [jnp.exp2(s - m) for s in ss]
    l = ps[0].sum(axis=-1, keepdims=True)
    for p in ps[1:]:
        l = l + p.sum(axis=-1, keepdims=True)
    return ps, l


def _gqa_attn_kernel(q_ref, gq_ref, k_ref, v_ref, *rest, cache, aliased, rope):
    rest = list(rest)
    o_ref = rest.pop()
    if aliased:
        rest.pop()
    if rope:
        cos, sin = rest[0][...], rest[1][...]
        rest = rest[2:]
    if cache:
        ck_ref, cv_ref = rest
        ks = [ck_ref[...].astype(BF16), k_ref[...].astype(BF16)]
        vs = [cv_ref[...].astype(BF16), v_ref[...].astype(BF16)]
    else:
        ks = [k_ref[...].astype(BF16)]
        vs = [v_ref[...].astype(BF16)]
    grp = ATTN_HEADS // ATTN_KV_HEADS
    for h in range(grp):
        y = _head_norm(q_ref[:, h * HD:(h + 1) * HD], gq_ref[...])
        if rope:
            y = _rope(y, cos, sin)
        q = (y * Q_SCALE).astype(BF16)
        ps, l = _softmax_parts(q, ks)
        o = jnp.dot(ps[0].astype(BF16), vs[0], preferred_element_type=F32)
        for p, v in zip(ps[1:], vs[1:]):
            o = o + jnp.dot(p.astype(BF16), v, preferred_element_type=F32)
        o_ref[:, h * HD:(h + 1) * HD] = (o / l).astype(o_ref.dtype)


def _gqa_attn(st, qkv, g_q, k, row0, nb, seq, tables, cache_k, cache_v, layer_j, prev):
    tq = min(1024, seq)
    nq = seq // tq
    grp_w = (ATTN_HEADS // ATTN_KV_HEADS) * HD
    v_col0 = ATTN_HEADS + ATTN_KV_HEADS
    assert row0 % seq == 0, "a segment must start on a multiple of its sequence length"
    b0 = row0 // seq
    q0 = row0 // tq
    in_specs = [
        pl.BlockSpec((tq, grp_w), lambda b, g, i: (q0 + b * nq + i, g)),
        pl.BlockSpec((1, HD), lambda b, g, i: (0, 0)),
        pl.BlockSpec((seq, HD), lambda b, g, i: (b + b0, g)),
        pl.BlockSpec((seq, HD), lambda b, g, i: (b + b0, v_col0 + g)),
    ]
    args = [qkv, g_q.reshape(1, HD), k, qkv]
    if tables is not None:
        in_specs += [pl.BlockSpec((tq, HD), lambda b, g, i: (i, 0))] * 2
        args += list(tables)
    cache = cache_k is not None
    if cache:
        past = cache_k.shape[2]
        ck = cache_k.reshape(cache_k.shape[0], cache_k.shape[1], past, ATTN_KV_HEADS * HD)
        cv = cache_v.reshape(ck.shape)
        in_specs += [pl.BlockSpec((None, None, past, HD), lambda b, g, i: (b, layer_j, 0, g))] * 2
        args += [ck, cv]
    aliases = {}
    if prev is not None:
        in_specs.append(pl.BlockSpec(memory_space=pl.ANY))
        args.append(prev)
        aliases = {len(args) - 1: 0}
    return pl.pallas_call(
        functools.partial(_gqa_attn_kernel, cache=cache, aliased=prev is not None,
                          rope=tables is not None),
        grid=(nb, ATTN_KV_HEADS, nq), in_specs=in_specs,
        out_specs=pl.BlockSpec((tq, grp_w), lambda b, g, i: (q0 + b * nq + i, g)),
        out_shape=jax.ShapeDtypeStruct((st.n, ATTN_HEADS * HD), BF16),
        input_output_aliases=aliases,
        compiler_params=_cparams(("parallel", "parallel", "parallel")), name="gqa_attn",
    )(*args)


def _diff_attn_kernel(lam_ref, q_ref, k_ref, v_ref, g_ref, *rest, cache, out_scale, aliased):
    if aliased:
        rest = rest[:-2] + rest[-1:]
    if cache:
        ck_ref, cv_ref, o_ref = rest
        vs = [cv_ref[...].astype(BF16), v_ref[...].astype(BF16)]
    else:
        (o_ref,) = rest
        vs = [v_ref[...].astype(BF16)]
    lam = lam_ref[0]
    outs = []
    for m in range(2):
        q = q_ref[:, m * HD:(m + 1) * HD]
        ks = [k_ref[:, m * HD:(m + 1) * HD]]
        if cache:
            ks = [ck_ref[:, m * HD:(m + 1) * HD].astype(BF16)] + ks
        ps, l = _softmax_parts(q, ks)
        o = jnp.dot(ps[0].astype(BF16), vs[0], preferred_element_type=F32)
        for p, v in zip(ps[1:], vs[1:]):
            o = o + jnp.dot(p.astype(BF16), v, preferred_element_type=F32)
        outs.append(o / l)
    o = outs[0] - lam * outs[1]
    o = o * lax.rsqrt(jnp.mean(o * o, axis=-1, keepdims=True) + EPS) * g_ref[...]
    o_ref[...] = (o * out_scale).astype(o_ref.dtype)


def _diff_attn(st, lam, q, k, v, g_sub, lam_init, row0, nb, seq, cache_k, cache_v, layer_j, prev):
    tq = min(1024, seq)
    nq = seq // tq
    hw = 2 * HD
    assert row0 % seq == 0, "a segment must start on a multiple of its sequence length"
    b0 = row0 // seq
    q0 = row0 // tq
    in_specs = [
        pl.BlockSpec(memory_space=pltpu.SMEM),
        pl.BlockSpec((tq, hw), lambda b, h, i: (q0 + b * nq + i, h)),
        pl.BlockSpec((seq, hw), lambda b, h, i: (b + b0, h)),
        pl.BlockSpec((seq, hw), lambda b, h, i: (b + b0, h)),
        pl.BlockSpec((1, hw), lambda b, h, i: (0, 0)),
    ]
    args = [lam, q, k, v, g_sub.reshape(1, hw)]
    cache = cache_k is not None
    if cache:
        past = cache_k.shape[2]
        ck = cache_k.reshape(cache_k.shape[0], cache_k.shape[1], past, 2 * DIFF_HEADS * HD)
        cv = cache_v.reshape(ck.shape)
        in_specs += [pl.BlockSpec((None, None, past, hw), lambda b, h, i: (b, layer_j, 0, h))] * 2
        args += [ck, cv]
    aliases = {}
    if prev is not None:
        in_specs.append(pl.BlockSpec(memory_space=pl.ANY))
        args.append(prev)
        aliases = {len(args) - 1: 0}
    return pl.pallas_call(
        functools.partial(_diff_attn_kernel, cache=cache, out_scale=1.0 - lam_init,
                          aliased=prev is not None),
        grid=(nb, DIFF_HEADS, nq), in_specs=in_specs,
        out_specs=pl.BlockSpec((tq, hw), lambda b, h, i: (q0 + b * nq + i, h)),
        out_shape=jax.ShapeDtypeStruct((st.n, DIFF_HEADS * hw), BF16),
        input_output_aliases=aliases,
        compiler_params=_cparams(("parallel", "parallel", "parallel")), name="diff_attn",
    )(*args)


def _conv_kernel(x_ref, w_ref, b_ref, *rest):
    o_ref, pad_ref = rest[-2:]
    seq = x_ref.shape[0]
    half = SSD_CONV // 2
    zeros = jnp.zeros((8, x_ref.shape[1]), F32)
    pad_ref[0:8, :] = zeros
    pad_ref[seq + 8:seq + 16, :] = zeros
    pad_ref[8:seq + 8, :] = x_ref[...]
    xp = pad_ref[...]
    rows = seq + 16
    acc = jnp.broadcast_to(b_ref[...], x_ref.shape)
    for k in range(SSD_CONV):
        shifted = xp if k == half else pltpu.roll(xp, (half - k) % rows, 0)
        acc = acc + w_ref[k:k + 1, :] * shifted[8:8 + seq, :]
    o_ref[...] = _silu(acc)


def _ssd_conv(st, xbc, conv_w, conv_b, row0, nb, seq, prev):
    tc = 512
    assert row0 % seq == 0, "a segment must start on a multiple of its sequence length"
    b0 = row0 // seq
    in_specs = [pl.BlockSpec((seq, tc), lambda b, c: (b + b0, c)),
                pl.BlockSpec((SSD_CONV, tc), lambda b, c: (0, c)),
                pl.BlockSpec((1, tc), lambda b, c: (0, c))]
    args = [xbc, conv_w, conv_b.reshape(1, SSD_CONV_DIM)]
    aliases = {}
    if prev is not None:
        in_specs.append(pl.BlockSpec(memory_space=pl.ANY))
        args.append(prev)
        aliases = {len(args) - 1: 0}
    return pl.pallas_call(
        _conv_kernel, grid=(nb, SSD_CONV_DIM // tc),
        in_specs=in_specs,
        out_specs=pl.BlockSpec((seq, tc), lambda b, c: (b + b0, c)),
        out_shape=jax.ShapeDtypeStruct((st.n, SSD_CONV_DIM), F32),
        scratch_shapes=[pltpu.VMEM((seq + 16, tc), F32)],
        input_output_aliases=aliases,
        compiler_params=_cparams(("parallel", "parallel")), name="ssd_conv",
    )(*args)


def _scan_direction(x_ref, b_ref, c_ref, dt_ref, bias_ref, alog_ref, h0_ref, y_ref, hf_ref,
                    state_ref, acol_ref, arow_ref, dtrow_ref, wrow_ref, dec_ref, *, rev, nc):
    c = pl.program_id(1)
    q = SSD_CHUNK
    lane0 = SSD_HEADS if rev else 0
    r_heads = SSD_HEADS // SSD_GROUPS
    gw = r_heads * SSD_HEADDIM

    ii = lax.broadcasted_iota(jnp.int32, (q, q), 0)
    jj = lax.broadcasted_iota(jnp.int32, (q, q), 1)
    causal = (jj >= ii) if rev else (jj <= ii)

    @pl.when(c == 0)
    def _():
        if h0_ref is not None:
            for blk in range(SSD_INNER // 128):
                state_ref[:, blk * 128:(blk + 1) * 128] = h0_ref[blk * 128:(blk + 1) * 128, :].T
        else:
            state_ref[...] = jnp.zeros(state_ref.shape, F32)

    xdt = dt_ref[...] + bias_ref[...]
    dt = jnp.maximum(xdt, 0.0) + jnp.log(1.0 + jnp.exp(-jnp.abs(xdt)))
    la = dt * (-jnp.exp(alog_ref[...]))
    tri = causal.astype(BF16)
    tri_t = ((ii >= jj) if rev else (ii <= jj)).astype(BF16)
    la3 = _split3(la)
    acol_ref[...] = (jnp.dot(tri, la3[0], preferred_element_type=F32)
                     + jnp.dot(tri, la3[1], preferred_element_type=F32)
                     + jnp.dot(tri, la3[2], preferred_element_type=F32))
    arow = _dot3(_split3(la.T), tri_t)
    end = 0 if rev else q - 1
    a_end = jnp.broadcast_to(arow[:, end:end + 1], arow.shape)
    dt_t = dt.T
    arow_ref[...] = arow
    dtrow_ref[...] = dt_t
    wrow_ref[...] = jnp.exp(a_end - arow) * dt_t
    dec_ref[...] = jnp.exp(a_end)

    lane_lo = lax.broadcasted_iota(jnp.int32, (1, 128), 1) < SSD_HEADDIM

    for g in range(SSD_GROUPS):
        bmat = b_ref[:, g * SSD_STATE:(g + 1) * SSD_STATE]
        cmat = c_ref[:, g * SSD_STATE:(g + 1) * SSD_STATE]
        cb = lax.dot_general(cmat.astype(BF16), bmat.astype(BF16), (((1,), (1,)), ((), ())),
                             preferred_element_type=F32)
        b_t = bmat.T
        for pair in range(r_heads // 2):
            col = g * gw + pair * 128
            x_pair = x_ref[:, col:col + 128].astype(BF16)
            st_old = state_ref[:, col:col + 128]
            rhs = jnp.concatenate([x_pair, st_old.astype(BF16)], axis=0)
            ys, upds = [], []
            for sub in range(2):
                hd = lane0 + g * r_heads + 2 * pair + sub
                a_i = jnp.broadcast_to(acol_ref[:, hd:hd + 1], (q, q))
                seg = a_i - arow_ref[hd:hd + 1, :]
                decay = jnp.exp(jnp.where(causal, seg, -jnp.inf))
                m_intra = (cb * decay * dtrow_ref[hd:hd + 1, :]).astype(BF16)
                c_scaled = (cmat * jnp.exp(a_i)).astype(BF16)
                bw_t = (b_t * wrow_ref[hd:hd + 1, :]).astype(BF16)
                ys.append(jnp.dot(jnp.concatenate([m_intra, c_scaled], axis=1), rhs,
                                  preferred_element_type=F32))
                upds.append(jnp.dot(bw_t, x_pair, preferred_element_type=F32))
            hd0 = lane0 + g * r_heads + 2 * pair
            y_ref[:, col:col + 128] = jnp.where(lane_lo, ys[0], ys[1]).astype(y_ref.dtype)
            dec_pair = jnp.where(lane_lo, dec_ref[hd0:hd0 + 1, :], dec_ref[hd0 + 1:hd0 + 2, :])
            state_ref[:, col:col + 128] = dec_pair * st_old + jnp.where(lane_lo, upds[0], upds[1])

    if hf_ref is not None:
        @pl.when(c == nc - 1)
        def _():
            for blk in range(SSD_INNER // 128):
                hf_ref[blk * 128:(blk + 1) * 128, :] = state_ref[:, blk * 128:(blk + 1) * 128].T


def _ssd_scan_kernel(*refs, nc, has_init, want_final, aliased):
    fwd_in, bwd_in, (bias_ref, alog_ref) = refs[0:4], refs[4:8], refs[8:10]
    pos = 10
    h0 = (None, None)
    if has_init:
        h0 = refs[pos:pos + 2]; pos += 2
    if aliased:
        pos += 2
    ys = refs[pos:pos + 2]; pos += 2
    hf = (None, None)
    if want_final:
        hf = refs[pos:pos + 2]; pos += 2
    scratch = refs[pos:]
    per = len(scratch) // 2
    for d, ins in enumerate((fwd_in, bwd_in)):
        _scan_direction(*ins, bias_ref, alog_ref, h0[d], ys[d], hf[d],
                        *scratch[d * per:(d + 1) * per], rev=d == 1, nc=nc)


def _ssd_scan(st, xbc, dt, dt_bias, a_log, row0, nb, seq, h0s, layer_j, want_final, prevs):
    nc = seq // SSD_CHUNK
    c0 = row0 // SSD_CHUNK

    def chunk(rev):
        return lambda b, c: c0 + b * nc + (nc - 1 - c if rev else c)

    in_specs, args = [], []
    for rev in (False, True):
        ch = chunk(rev)
        in_specs += [
            pl.BlockSpec((SSD_CHUNK, SSD_INNER), lambda b, c, ch=ch: (ch(b, c), 0)),
            pl.BlockSpec((SSD_CHUNK, SSD_GN), lambda b, c, ch=ch: (ch(b, c), SSD_INNER // SSD_GN)),
            pl.BlockSpec((SSD_CHUNK, SSD_GN), lambda b, c, ch=ch: (ch(b, c), SSD_INNER // SSD_GN + 1)),
            pl.BlockSpec((SSD_CHUNK, 128), lambda b, c, ch=ch: (ch(b, c), 0)),
        ]
        args += [xbc, xbc, xbc, dt]
    in_specs += [pl.BlockSpec((1, 128), lambda b, c: (0, 0))] * 2
    args += [dt_bias.reshape(1, 128), a_log.reshape(1, 128)]
    if h0s is not None:
        for h0 in h0s:
            in_specs.append(pl.BlockSpec((None, None, SSD_INNER, SSD_STATE),
                                         lambda b, c: (b, layer_j, 0, 0)))
            args.append(h0.reshape(h0.shape[0], h0.shape[1], SSD_INNER, SSD_STATE))
    aliases = {}
    if prevs is not None:
        for d, prev in enumerate(prevs):
            in_specs.append(pl.BlockSpec(memory_space=pl.ANY))
            args.append(prev)
            aliases[len(args) - 1] = d
    out_shape = [jax.ShapeDtypeStruct((st.n, SSD_INNER), BF16)] * 2
    out_specs = [pl.BlockSpec((SSD_CHUNK, SSD_INNER), lambda b, c, ch=chunk(rev): (ch(b, c), 0))
                 for rev in (False, True)]
    if want_final:
        out_shape += [jax.ShapeDtypeStruct((nb, SSD_INNER, SSD_STATE), F32)] * 2
        out_specs += [pl.BlockSpec((None, SSD_INNER, SSD_STATE), lambda b, c: (b, 0, 0))] * 2
    per_dir = [pltpu.VMEM((SSD_STATE, SSD_INNER), F32),
               pltpu.VMEM((SSD_CHUNK, 128), F32)] + [pltpu.VMEM((128, SSD_CHUNK), F32)] * 4
    return pl.pallas_call(
        functools.partial(_ssd_scan_kernel, nc=nc, has_init=h0s is not None,
                          want_final=want_final, aliased=prevs is not None),
        grid=(nb, nc), in_specs=in_specs, out_specs=out_specs, out_shape=out_shape,
        scratch_shapes=per_dir * 2,
        input_output_aliases=aliases,
        compiler_params=_cparams(("parallel", "arbitrary")), name="ssd_scan",
    )(*args)


def _ssd_gate_kernel(yf_ref, yb_ref, x_ref, z_ref, d_ref, g_ref, o_ref):
    gw = SSD_INNER // SSD_GROUPS
    for grp in range(SSD_GROUPS):
        sl = slice(grp * gw, (grp + 1) * gw)
        y = (yf_ref[:, sl].astype(F32) + yb_ref[:, sl].astype(F32)) + d_ref[:, sl] * x_ref[:, sl]
        gated = y * _silu(z_ref[:, sl].astype(F32))
        normed = gated * lax.rsqrt(jnp.mean(gated * gated, axis=-1, keepdims=True) + EPS)
        o_ref[:, sl] = (normed * g_ref[:, sl]).astype(o_ref.dtype)


def _ssd_gate(yf, yb, xbc, z, d_sum, g_norm):
    n = yf.shape[0]
    tm = 256
    row = pl.BlockSpec((tm, SSD_INNER), lambda i: (i, 0))
    vec = pl.BlockSpec((1, SSD_INNER), lambda i: (0, 0))
    return pl.pallas_call(
        _ssd_gate_kernel, grid=(n // tm,),
        in_specs=[row, row, row, row, vec, vec], out_specs=row,
        out_shape=jax.ShapeDtypeStruct((n, SSD_INNER), BF16),
        compiler_params=_cparams(("parallel",)), name="ssd_gate",
    )(yf, yb, xbc, z, d_sum, g_norm.reshape(1, SSD_INNER))


def _moe_up_kernel(be_ref, nu_ref, a_ref, wg_ref, wu_ref, *rest, blk0):
    o_ref = rest[-1]
    m = pl.program_id(1) + blk0

    @pl.when(m < nu_ref[0])
    def _():
        a = a_ref[...]
        gate = jnp.dot(a, wg_ref[...].astype(BF16), preferred_element_type=F32)
        up = jnp.dot(a, wu_ref[...].astype(BF16), preferred_element_type=F32)
        o_ref[...] = (_silu(gate) * up).astype(o_ref.dtype)

    @pl.when(m >= nu_ref[0])
    def _():
        o_ref[...] = jnp.zeros(o_ref.shape, o_ref.dtype)


def _moe_down_kernel(be_ref, nu_ref, a_hbm, w_ref, o_ref, abuf, sem, *, tb):
    m = pl.program_id(0)
    j = pl.program_id(1)
    n_live = nu_ref[0]

    def fetch(blk, slot):
        rows = pl.ds(pl.multiple_of(blk * tb, tb), tb)
        return pltpu.make_async_copy(a_hbm.at[rows, :], abuf.at[slot], sem.at[slot])

    @pl.when(jnp.logical_and(m == 0, j == 0))
    def _():
        fetch(0, 0).start()

    @pl.when(jnp.logical_and(j == 0, m < n_live))
    def _():
        fetch(m, m % 2).wait()

        @pl.when(m + 1 < n_live)
        def _():
            fetch(m + 1, (m + 1) % 2).start()

    @pl.when(m < n_live)
    def _():
        o_ref[...] = jnp.dot(abuf[m % 2], w_ref[...].astype(BF16),
                             preferred_element_type=F32).astype(o_ref.dtype)

    @pl.when(m >= n_live)
    def _():
        o_ref[...] = jnp.zeros(o_ref.shape, o_ref.dtype)


def _moe(st, h, logits, w_gu, w_down, f):
    n = st.n
    tb = MOE_BLOCK
    n_assign = n * TOP_K
    n_blocks = n_assign // tb + N_EXPERTS
    slots = n_blocks * tb
    top_logit, top_e = lax.top_k(logits, TOP_K)
    gates = jax.nn.softmax(top_logit, axis=-1)
    flat_e = top_e.reshape(-1).astype(jnp.int32)
    onehot = (flat_e[:, None] == jnp.arange(N_EXPERTS, dtype=jnp.int32)[None, :]).astype(jnp.int32)
    rank = jnp.sum((jnp.cumsum(onehot, axis=0) - onehot) * onehot, axis=1)
    counts = jnp.sum(onehot, axis=0)
    padded = (counts + tb - 1) // tb * tb
    pad_end = jnp.cumsum(padded)
    pad_start = pad_end - padded
    dest = pad_start[flat_e] + rank
    block_start = jnp.arange(n_blocks, dtype=jnp.int32) * tb
    block_e = jnp.minimum(jnp.searchsorted(pad_end, block_start, side="right"),
                          N_EXPERTS - 1).astype(jnp.int32)
    n_used = (pad_end[-1] // tb).astype(jnp.int32).reshape(1)
    order = jnp.argsort(flat_e, stable=True).astype(jnp.int32)
    slot_e = jnp.repeat(block_e, tb)
    slot_r = jnp.arange(slots, dtype=jnp.int32) - pad_start[slot_e]
    filled = slot_r < counts[slot_e]
    src = jnp.where(filled, (jnp.cumsum(counts) - counts)[slot_e] + slot_r, 0)
    slot_tok = jnp.where(filled, order[src] // TOP_K, 0)

    tn = 512
    nf = EXPERT_DIM // tn

    def live_block(m, nu):
        return jnp.minimum(m, nu[0] - 1)

    head = min(MOE_HEAD_BLOCKS, n_blocks // 2)
    act = None
    for blk0, nblk in ((0, head), (head, n_blocks - head)):
        xs = h.at[slot_tok[blk0 * tb:(blk0 + nblk) * tb]].get(mode="promise_in_bounds")

        def local(m, nu, blk0=blk0, nblk=nblk):
            return jnp.clip(live_block(m + blk0, nu) - blk0, 0, nblk - 1)

        def expert(m, be, nu, blk0=blk0):
            return be[live_block(m + blk0, nu)]

        in_specs = [
            pl.BlockSpec((tb, D_MODEL), lambda j, m, be, nu, local=local: (local(m, nu), 0)),
            pl.BlockSpec((None, None, D_MODEL, tn),
                         lambda j, m, be, nu, expert=expert: (f, expert(m, be, nu), 0, j)),
            pl.BlockSpec((None, None, D_MODEL, tn),
                         lambda j, m, be, nu, expert=expert: (f, expert(m, be, nu), 0, j + nf)),
        ]
        args = [block_e, n_used, xs, w_gu, w_gu]
        aliases = {}
        if act is not None:
            in_specs.append(pl.BlockSpec(memory_space=pl.ANY))
            args.append(act)
            aliases = {len(args) - 1: 0}
        act = pl.pallas_call(
            functools.partial(_moe_up_kernel, blk0=blk0),
            grid_spec=pltpu.PrefetchScalarGridSpec(
                num_scalar_prefetch=2, grid=(nf, nblk), in_specs=in_specs,
                out_specs=pl.BlockSpec((tb, tn), lambda j, m, be, nu, blk0=blk0: (m + blk0, j)),
            ),
            out_shape=jax.ShapeDtypeStruct((slots, EXPERT_DIM), BF16),
            input_output_aliases=aliases,
            compiler_params=_cparams(("parallel", "arbitrary")), name="moe_up",
        )(*args)

    tn2 = 256
    yb = pl.pallas_call(
        functools.partial(_moe_down_kernel, tb=tb),
        grid_spec=pltpu.PrefetchScalarGridSpec(
            num_scalar_prefetch=2, grid=(n_blocks, D_MODEL // tn2),
            in_specs=[
                pl.BlockSpec(memory_space=pl.ANY),
                pl.BlockSpec((None, None, EXPERT_DIM, tn2),
                             lambda m, j, be, nu: (f, be[live_block(m, nu)], 0, j)),
            ],
            out_specs=pl.BlockSpec((tb, tn2), lambda m, j, be, nu: (m, j)),
            scratch_shapes=[pltpu.VMEM((2, tb, EXPERT_DIM), BF16), pltpu.SemaphoreType.DMA((2,))],
        ),
        out_shape=jax.ShapeDtypeStruct((slots, D_MODEL), BF16),
        compiler_params=_cparams(("arbitrary", "arbitrary")), name="moe_down",
    )(block_e, n_used, act, w_down)

    dest2 = dest.reshape(n, TOP_K)
    y0 = yb.at[dest2[:, 0]].get(mode="promise_in_bounds")
    y1 = yb.at[dest2[:, 1]].get(mode="promise_in_bounds")
    return y0, y1, gates


def _combine_kernel(x_ref, y0_ref, y1_ref, g_ref, gate_ref, *rest, norm):
    y = g_ref[:, 0:1] * y0_ref[...].astype(F32) + g_ref[:, 1:2] * y1_ref[...].astype(F32)
    xn = x_ref[...] + gate_ref[...] * y
    if norm:
        gn_ref, shift_ref, scale_ref, o_ref, h_ref = rest
        _norm_mod_store(xn, gn_ref, shift_ref, scale_ref, h_ref)
    else:
        (o_ref,) = rest
    o_ref[...] = xn


def _combine(st, x, y0, y1, gates, mods, layer, g_next=None):
    tm = st.tile(512)
    row = pl.BlockSpec((tm, D_MODEL), lambda i: (i, 0))

    def mod(lyr, which):
        return pl.BlockSpec((None, 1, D_MODEL),
                            lambda i: (_mod_index(lyr, which, st.mod_row(i, tm)), 0, 0))

    in_specs = [row, row, row, pl.BlockSpec((tm, TOP_K), lambda i: (i, 0)), mod(layer, 5)]
    args = [x, y0, y1, gates, mods]
    out_shape = [jax.ShapeDtypeStruct((st.n, D_MODEL), F32)]
    out_specs = [row]
    if g_next is not None:
        in_specs += [pl.BlockSpec((1, D_MODEL), lambda i: (0, 0)), mod(layer + 1, 0), mod(layer + 1, 1)]
        args += [g_next.reshape(1, D_MODEL), mods, mods]
        out_shape.append(jax.ShapeDtypeStruct((st.n, D_MODEL), BF16))
        out_specs.append(row)
    res = pl.pallas_call(
        functools.partial(_combine_kernel, norm=g_next is not None), grid=(st.n // tm,),
        in_specs=in_specs, out_specs=out_specs, out_shape=out_shape,
        compiler_params=_cparams(("parallel",)), name="moe_combine",
    )(*args)
    return res if g_next is not None else (res[0], None)


def kernel(x_prompt, x_sample, cache_attn_k, cache_attn_v, cache_diff_k, cache_diff_v, state_ssd_fwd, state_ssd_bwd, c, c_ctx, w_ada, b_ada, g_mix, g_ffn, g_final, attn_w_in, attn_g_q, attn_g_k, attn_w_out, diff_w_in, diff_lambda, diff_g_sub, diff_w_out, ssd_w_in, ssd_conv_w, ssd_conv_b, ssd_dt_bias, ssd_a_log, ssd_d, ssd_g_norm, ssd_w_out, ffn_w_gu, ffn_w_down, moe_w_router, moe_w_gu, moe_w_down):
    pb, pl_len, _ = x_prompt.shape
    sb, sl_len, _ = x_sample.shape
    st = _Stream(pb, pl_len, sb, sl_len)
    np_, ns = st.n_prompt, st.n_sample
    tm = st.tile(1024)

    x = jnp.concatenate([x_prompt.reshape(np_, D_MODEL), x_sample.reshape(ns, D_MODEL)], axis=0)

    cond = jnp.zeros((MOD_ROWS, D_MODEL), F32).at[:sb].set(c).at[CTX_ROW].set(c_ctx)
    cond_act = jax.nn.silu(cond).astype(BF16)
    mods = []
    for i in range(DEPTH):
        m = _matmul(cond_act, w_ada, (i,), n_off=0, n_out=6 * D_MODEL, tm=MOD_ROWS, tn=1024,
                    out_dtype=F32, name="ada")
        mods.append(m + b_ada[i][None, :])
    mods = jnp.stack(mods).reshape(DEPTH * MOD_ROWS * 6, 1, D_MODEL)

    tables = _rope_tables(sl_len, tm)
    prep_tables = _rope_tables(sl_len, GQA_PREP_ROWS)

    def table_index(i):
        return _table_block(st, i, tm)

    tm2 = st.tile(2048)

    def gate_index(layer, which, tile):
        return lambda i: _mod_index(layer, which, st.mod_row(i, tile))

    new_attn_k, new_attn_v, new_diff_k, new_diff_v, new_ssd_f, new_ssd_b = [], [], [], [], [], []
    def out_proj(o, w16, x, layer, fuse_norm):
        if fuse_norm:
            return _out_proj_norm(st, o, w16, x, mods, layer, g_ffn[layer])
        xo = _matmul(o, w16, (), n_off=0, n_out=D_MODEL, tm=tm2, tn=512,
                     out_dtype=F32, epilogue="resid", resid=x, mods=mods,
                     gate_index=gate_index(layer, 2, tm2), name="attn_out")
        return xo, None

    h_pre = None
    for i in range(DEPTH):
        kind, j = i % 3, i // 3
        f = i // 2
        w_router = moe_w_router[f] if i % 2 == 1 else None
        h = h_pre if h_pre is not None else _norm_mod(st, x, g_mix[i], mods, i, 0)
        h_pre = hf = logits = None
        if kind == 0:
            qkv = _matmul(h, attn_w_in, (j,), n_off=0, n_out=attn_w_in.shape[-1], tm=tm2, tn=512,
                          out_dtype=F32, name="attn_in")
            k = _gqa_kprep(st, qkv, attn_g_k[j], prep_tables)
            o = _gqa_attn(st, qkv, attn_g_q[j], k, 0, pb, pl_len, None, None, None, j, None)
            o = _gqa_attn(st, qkv, attn_g_q[j], k, np_, sb, sl_len, tables,
                          cache_attn_k, cache_attn_v, j, o)
            new_attn_k.append(k[:np_].reshape(pb, pl_len, ATTN_KV_HEADS, HD))
            new_attn_v.append(qkv[:np_, (ATTN_HEADS + ATTN_KV_HEADS) * HD:]
                              .reshape(pb, pl_len, ATTN_KV_HEADS, HD))
            x, hf = out_proj(o, attn_w_out[j].astype(BF16), x, i, w_router is None)
        elif kind == 1:
            lam_init = 0.8 - 0.6 * math.exp(-0.3 * i)
            lf = diff_lambda[j]
            lam = (jnp.exp(jnp.sum(lf[0] * lf[1])) - jnp.exp(jnp.sum(lf[2] * lf[3])) + lam_init).reshape(1)
            w = 2 * DIFF_HEADS * HD
            q = _matmul(h, diff_w_in, (j,), n_off=0, n_out=w, tm=tm, tn=1024, out_dtype=BF16,
                        epilogue="rope", scale=Q_SCALE,
                        tables=tables, table_index=table_index, name="diff_in_q")
            k = _matmul(h, diff_w_in, (j,), n_off=w, n_out=w, tm=tm, tn=1024, out_dtype=BF16,
                        epilogue="rope", tables=tables, table_index=table_index, name="diff_in_k")
            v = _matmul(h, diff_w_in, (j,), n_off=2 * w, n_out=w, tm=tm2, tn=512, out_dtype=F32,
                        name="diff_in_v")
            k_ctx = _matmul(h, diff_w_in, (j,), n_off=w, n_out=w, tm=tm, tn=1024, rows=np_,
                            out_dtype=F32, name="diff_in_kctx")
            o = _diff_attn(st, lam, q, k, v, diff_g_sub[j], lam_init, 0, pb, pl_len, None, None, j, None)
            o = _diff_attn(st, lam, q, k, v, diff_g_sub[j], lam_init, np_, sb, sl_len,
                           cache_diff_k, cache_diff_v, j, o)
            new_diff_k.append(k_ctx.reshape(pb, pl_len, 2 * DIFF_HEADS, HD))
            new_diff_v.append(v[:np_].reshape(pb, pl_len, DIFF_HEADS, 2 * HD))
            x, hf = out_proj(o, diff_w_out[j].astype(BF16), x, i, w_router is None)
        else:
            z = _matmul(h, ssd_w_in, (j,), n_off=0, n_out=SSD_INNER, tm=tm2, tn=512,
                        out_dtype=BF16, name="ssd_in_z")
            xbc = _matmul(h, ssd_w_in, (j,), n_off=SSD_INNER, n_out=SSD_CONV_DIM, tm=tm2, tn=512,
                          out_dtype=F32, name="ssd_in_xbc")
            dt = _matmul(h, ssd_w_in, (j,), n_off=SSD_INNER + SSD_CONV_DIM, n_out=2 * SSD_HEADS,
                         tm=tm, tn=128, out_dtype=F32, name="ssd_in_dt")
            xc = _ssd_conv(st, xbc, ssd_conv_w[j], ssd_conv_b[j], 0, pb, pl_len, None)
            xc = _ssd_conv(st, xbc, ssd_conv_w[j], ssd_conv_b[j], np_, sb, sl_len, xc)
            yf, yb, hfin_f, hfin_b = _ssd_scan(st, xc, dt, ssd_dt_bias[j], ssd_a_log[j], 0, pb, pl_len,
                                               None, j, True, None)
            ys = _ssd_scan(st, xc, dt, ssd_dt_bias[j], ssd_a_log[j], np_, sb, sl_len,
                           (state_ssd_fwd, state_ssd_bwd), j, False, (yf, yb))
            new_ssd_f.append(hfin_f.reshape(pb, SSD_HEADS, SSD_HEADDIM, SSD_STATE))
            new_ssd_b.append(hfin_b.reshape(pb, SSD_HEADS, SSD_HEADDIM, SSD_STATE))
            d_sum = jnp.repeat(ssd_d[j][0] + ssd_d[j][1], SSD_HEADDIM).reshape(1, SSD_INNER)
            o = _ssd_gate(ys[0], ys[1], xc, z, d_sum, ssd_g_norm[j])
            x = _matmul(o, ssd_w_out[j].astype(BF16), (), n_off=0, n_out=D_MODEL, tm=tm, tn=512,
                        out_dtype=F32, epilogue="resid", resid=x, mods=mods,
                        gate_index=gate_index(i, 2, tm), name="ssd_out")

        if hf is None:
            if w_router is not None:
                hf, logits = _norm_mod(st, x, g_ffn[i], mods, i, 3, w_router=w_router)
            else:
                hf = _norm_mod(st, x, g_ffn[i], mods, i, 3)
        if i % 2 == 0:
            act = _matmul(hf, ffn_w_gu, (f,), n_off=0, n_out=FFN_DIM, tm=tm, tn=512,
                          out_dtype=BF16, epilogue="swiglu", up_off=FFN_DIM, name="ffn_up")
            x = _matmul(act, ffn_w_down[f].astype(BF16), (), n_off=0, n_out=D_MODEL, tm=tm, tn=512,
                        out_dtype=F32, epilogue="resid", resid=x, mods=mods,
                        gate_index=gate_index(i, 5, tm), name="ffn_down")
        else:
            y0, y1, gates = _moe(st, hf, logits[:, :N_EXPERTS], moe_w_gu, moe_w_down, f)
            x, h_pre = _combine(st, x, y0, y1, gates, mods, i,
                                g_next=g_mix[i + 1] if i + 1 < DEPTH else None)

    y_prompt = _final_norm(x, g_final, 0, np_).reshape(pb, pl_len, D_MODEL)
    y_sample = _final_norm(x, g_final, np_, ns).reshape(sb, sl_len, D_MODEL)
    return (y_prompt, y_sample,
            jnp.stack(new_attn_k, axis=1), jnp.stack(new_attn_v, axis=1),
            jnp.stack(new_diff_k, axis=1), jnp.stack(new_diff_v, axis=1),
            jnp.stack(new_ssd_f, axis=1), jnp.stack(new_ssd_b, axis=1))
```
